```python
import math, functools
import jax, jax.numpy as jnp
from jax import lax
import numpy as np

D_MODEL = 2048
BATCH = 8
SEQ = 2048
DEPTH = 2
DEC_BATCH = 32
DEC_SEQ = 1
PAST_LEN = 8192
PAGE_SIZE = 128

N_A_LAYERS = DEPTH // 2
N_B_LAYERS = DEPTH - N_A_LAYERS
CHUNK = 128
E_A = D_MODEL
G_A = E_A // 128
HEAD_DIM = 128
N_Q_HEADS = D_MODEL // HEAD_DIM
N_KV_HEADS = 4
Q_PER_KV = N_Q_HEADS // N_KV_HEADS
Q_WIDTH = N_Q_HEADS * HEAD_DIM
N_BRANCH = 3
CMP_LEN = 32
CMP_STRIDE = 16
CMP_RATIO = CMP_LEN // CMP_STRIDE
CMP_HIDDEN = HEAD_DIM
SEL_BLOCK = 64
N_SEL = 16
N_LOCAL_SEL = 2
FORCE_BONUS = 1e4
WINDOW = 512
Q_BLOCK = 128
ATTN_SCALE = HEAD_DIM ** -0.5
D_FF = -(-8 * D_MODEL // 768) * 256
DEEPNORM_ALPHA = (2 * DEPTH) ** 0.25
DEEPNORM_BETA = (8 * DEPTH) ** -0.25
LN_EPS = 1e-5

kernel_name = 'yoco_gmlp_nsa_decoder_step'


def layer_norm(x, g, b):
    xf = x.astype(jnp.float32)
    mu = jnp.mean(xf, axis=-1, keepdims=True)
    var = jnp.mean(jnp.square(xf - mu), axis=-1, keepdims=True)
    y = (xf - mu) * lax.rsqrt(var + LN_EPS) * g.astype(jnp.float32) + b.astype(jnp.float32)
    return y.astype(x.dtype)


def masked_softmax(s, valid):
    s = jnp.where(valid, s.astype(jnp.float32), -jnp.inf)
    m = jnp.max(s, axis=-1, keepdims=True)
    m = jnp.where(jnp.isfinite(m), m, 0.0)
    e = jnp.exp(s - m)
    return e / jnp.maximum(jnp.sum(e, axis=-1, keepdims=True), 1e-30)


def modulated_sublayer(x, c, w_ada, b_ada, ln_g, ln_b, fn):
    shift, scale, gate = jnp.split(jax.nn.silu(c) @ w_ada + b_ada, 3, axis=-1)
    h = x * (1.0 + scale[:, None]) + shift[:, None]
    out, aux = fn(h)
    y = layer_norm(DEEPNORM_ALPHA * x + (1.0 + gate[:, None]) * out, ln_g, ln_b)
    return y, aux


def swiglu_ffn(h, w_in, w_out):
    g, u = jnp.split(h @ w_in, 2, axis=-1)
    return (jax.nn.silu(g) * u) @ w_out, None


def chunk_gmlp(h, w_in, v_g, v_b, w_s, b_s, w_out):
    B, T, _ = h.shape
    u, v = jnp.split(jax.nn.gelu(h @ w_in), 2, axis=-1)
    v = layer_norm(v, v_g, v_b)
    L = min(T, CHUNK)
    vc = v.reshape(B, T // L, L, G_A, E_A // G_A)
    w_mix = jnp.tril(w_s[:, :L, :L])
    mixed = jnp.einsum('gij,bcjgd->bcigd', w_mix, vc) + b_s[:, :L].T[None, None, :, :, None]
    return (u * mixed.reshape(B, T, E_A)) @ w_out, v


def shared_kv(x_kv, w_kv):
    B, T, _ = x_kv.shape
    kv = (x_kv @ w_kv).reshape(B, T, N_BRANCH, N_KV_HEADS, 2, HEAD_DIM)
    return kv[:, :, 0], kv[:, :, 1], kv[:, :, 2]


def compress_blocks(rows, cmp_pe, cmp_w1, cmp_b1, cmp_w2, cmp_b2):
    B, T = rows.shape[:2]
    n_chunks = T // CMP_STRIDE
    n_cmp = n_chunks - CMP_RATIO + 1
    chunks = rows[:, :n_chunks * CMP_STRIDE].reshape(B, n_chunks, CMP_STRIDE, N_KV_HEADS, 2, HEAD_DIM)
    w1 = cmp_w1.reshape(2, CMP_RATIO, CMP_STRIDE, HEAD_DIM, CMP_HIDDEN)
    pre = jnp.einsum('kld,klde->ke', cmp_pe, cmp_w1) + cmp_b1
    for r in range(CMP_RATIO):
        pre = pre + jnp.einsum('bcshkd,ksde->bchke', chunks[:, r:r + n_cmp], w1[:, r])
    hid = jax.nn.gelu(pre)
    return jnp.einsum('bnhke,ked->bnhkd', hid, cmp_w2) + cmp_b2


def cmp_attention(q, q_pos, kv_cmp):
    n_cmp = kv_cmp.shape[1]
    s = jnp.einsum('bqgrd,bngd->bqgrn', q, kv_cmp[..., 0, :]) * ATTN_SCALE
    blk_last = jnp.arange(n_cmp) * CMP_STRIDE + CMP_LEN - 1
    valid = (blk_last[None, :] <= q_pos[:, None])[None, :, None, None, :]
    p = masked_softmax(s, valid)
    o = jnp.einsum('bqgrn,bngd->bqgrd', p.astype(q.dtype), kv_cmp[..., 1, :])
    return o, p


def select_blocks(p_cmp, q_pos, n_slc):
    n_cmp = p_cmp.shape[-1]
    cmp_lo = jnp.arange(n_cmp)[:, None] * CMP_STRIDE
    slc_lo = jnp.arange(n_slc)[None, :] * SEL_BLOCK
    overlap = jnp.minimum(cmp_lo + CMP_LEN, slc_lo + SEL_BLOCK) - jnp.maximum(cmp_lo, slc_lo)
    w_map = jnp.clip(overlap, 0, None).astype(jnp.float32) / CMP_LEN
    imp = jnp.einsum('bqgrn,nj->bqgj', p_cmp, w_map)
    q_blk = (q_pos // SEL_BLOCK)[:, None]
    j = jnp.arange(n_slc)[None, :]
    causal = j <= q_blk
    forced = (j == 0) | (causal & (j > q_blk - N_LOCAL_SEL))
    imp = jnp.where(forced[None, :, None, :], imp + FORCE_BONUS, imp)
    imp = jnp.where(causal[None, :, None, :], imp, -jnp.inf)
    top, idx = lax.top_k(imp, min(N_SEL, n_slc))
    return idx, jnp.isfinite(top)


def sel_attend(q, q_pos, kv, key_pos, key_valid):
    s = jnp.einsum('...grd,...gnd->...grn', q, kv[..., 0, :]) * ATTN_SCALE
    valid = key_valid & (key_pos <= q_pos[..., None, None])
    p = masked_softmax(s, valid[..., None, :])
    return jnp.einsum('...grn,...gnd->...grd', p.astype(q.dtype), kv[..., 1, :])


def prompt_selected(q, sel_idx, sel_valid, kv_sel):
    B, T = q.shape[:2]
    qb = min(Q_BLOCK, T)
    nqb = T // qb
    k = sel_idx.shape[-1]
    g_idx = jnp.arange(N_KV_HEADS)[None, :, None, None]
    offs = jnp.arange(SEL_BLOCK)

    def one_block(args):
        b, blk, q_blk, idx_blk, valid_blk = args
        rows = idx_blk[..., None] * SEL_BLOCK + offs
        kv = kv_sel[b][rows, g_idx]
        kv = kv.reshape(qb, N_KV_HEADS, k * SEL_BLOCK, 2, HEAD_DIM)
        key_pos = rows.reshape(qb, N_KV_HEADS, k * SEL_BLOCK)
        key_valid = jnp.repeat(valid_blk, SEL_BLOCK, axis=-1)
        q_pos = blk * qb + jnp.arange(qb)
        return sel_attend(q_blk, q_pos, kv, key_pos, key_valid)

    xs = (jnp.repeat(jnp.arange(B), nqb), jnp.tile(jnp.arange(nqb), B),
          q.reshape(B * nqb, qb, N_KV_HEADS, Q_PER_KV, HEAD_DIM),
          sel_idx.reshape(B * nqb, qb, N_KV_HEADS, k),
          sel_valid.reshape(B * nqb, qb, N_KV_HEADS, k))
    o = lax.map(one_block, xs)
    return o.reshape(B, T, N_KV_HEADS, Q_PER_KV, HEAD_DIM)


def sample_selected(q, q_pos, sel_idx, sel_valid, pool_sel, page_table, kv_sel_new):
    B, Tq = q.shape[:2]
    k = sel_idx.shape[-1]
    b_idx = jnp.arange(B)[:, None, None, None, None]
    g_idx = jnp.arange(N_KV_HEADS)[None, None, :, None, None]
    rows = sel_idx[..., None] * SEL_BLOCK + jnp.arange(SEL_BLOCK)
    past_rows = jnp.minimum(rows, PAST_LEN - 1)
    phys = page_table[b_idx, past_rows // PAGE_SIZE]
    kv_past = pool_sel[phys, past_rows % PAGE_SIZE, g_idx]
    new_rows = jnp.clip(rows - PAST_LEN, 0, Tq - 1)
    kv_new = kv_sel_new[b_idx, new_rows, g_idx]
    kv = jnp.where((rows < PAST_LEN)[..., None, None], kv_past, kv_new)
    kv = kv.reshape(B, Tq, N_KV_HEADS, k * SEL_BLOCK, 2, HEAD_DIM)
    key_pos = rows.reshape(B, Tq, N_KV_HEADS, k * SEL_BLOCK)
    key_valid = jnp.repeat(sel_valid, SEL_BLOCK, axis=-1)
    return sel_attend(q, q_pos, kv, key_pos, key_valid)


def prompt_window(q, kv_win):
    B, T = q.shape[:2]
    qb = min(Q_BLOCK, T)
    nqb = T // qb
    span = WINDOW + qb
    padded = jnp.pad(kv_win, ((0, 0), (WINDOW, 0), (0, 0), (0, 0), (0, 0)))
    key_pos = jnp.arange(nqb)[:, None] * qb - WINDOW + jnp.arange(span)[None, :]
    kv = padded[:, key_pos + WINDOW]
    q_pos = jnp.arange(nqb)[:, None] * qb + jnp.arange(qb)[None, :]
    qr = q.reshape(B, nqb, qb, N_KV_HEADS, Q_PER_KV, HEAD_DIM)
    s = jnp.einsum('bcqgrd,bckgd->bcqgrk', qr, kv[..., 0, :]) * ATTN_SCALE
    kp = key_pos[:, None, :]
    qp = q_pos[:, :, None]
    valid = (kp >= 0) & (kp <= qp) & (kp > qp - WINDOW)
    p = masked_softmax(s, valid[None, :, :, None, None, :])
    o = jnp.einsum('bcqgrk,bckgd->bcqgrd', p.astype(q.dtype), kv[..., 1, :])
    return o.reshape(B, T, N_KV_HEADS, Q_PER_KV, HEAD_DIM)


def sample_window(q, q_pos, win_keys):
    n_keys = win_keys.shape[1]
    key_pos = PAST_LEN + q.shape[1] - n_keys + jnp.arange(n_keys)
    s = jnp.einsum('bqgrd,bkgd->bqgrk', q, win_keys[..., 0, :]) * ATTN_SCALE
    valid = (key_pos[None, :] <= q_pos[:, None]) & (key_pos[None, :] > q_pos[:, None] - WINDOW)
    p = masked_softmax(s, valid[None, :, None, None, :])
    return jnp.einsum('bqgrk,bkgd->bqgrd', p.astype(q.dtype), win_keys[..., 1, :])


def nsa_queries(h, w_qg):
    B, T, _ = h.shape
    qg = h @ w_qg
    q = qg[..., :Q_WIDTH].reshape(B, T, N_KV_HEADS, Q_PER_KV, HEAD_DIM)
    gates = jax.nn.sigmoid(qg[..., Q_WIDTH:].astype(jnp.float32)).astype(h.dtype)
    return q, gates.reshape(B, T, N_KV_HEADS, Q_PER_KV, N_BRANCH)


def nsa_merge(o_cmp, o_sel, o_win, gates, w_o):
    o = gates[..., 0:1] * o_cmp + gates[..., 1:2] * o_sel + gates[..., 2:3] * o_win
    B, T = o.shape[:2]
    return o.reshape(B, T, Q_WIDTH) @ w_o


def nsa_prompt(h, w_qg, w_o, kv_cmp_blocks, kv_sel_rows, kv_win_rows):
    q, gates = nsa_queries(h, w_qg)
    T = h.shape[1]
    q_pos = jnp.arange(T)
    o_cmp, p_cmp = cmp_attention(q, q_pos, kv_cmp_blocks)
    sel_idx, sel_valid = select_blocks(p_cmp, q_pos, T // SEL_BLOCK)
    o_sel = prompt_selected(q, sel_idx, sel_valid, kv_sel_rows)
    o_win = prompt_window(q, kv_win_rows)
    return nsa_merge(o_cmp, o_sel, o_win, gates, w_o), None


def nsa_sample(h, w_qg, w_o, kv_cmp_blocks, pool_sel, page_table, kv_sel_new, win_keys):
    q, gates = nsa_queries(h, w_qg)
    Tq = h.shape[1]
    q_pos = PAST_LEN + jnp.arange(Tq)
    o_cmp, p_cmp = cmp_attention(q, q_pos, kv_cmp_blocks)
    sel_idx, sel_valid = select_blocks(p_cmp, q_pos, -(-(PAST_LEN + Tq) // SEL_BLOCK))
    o_sel = sample_selected(q, q_pos, sel_idx, sel_valid, pool_sel, page_table, kv_sel_new)
    o_win = sample_window(q, q_pos, win_keys)
    return nsa_merge(o_cmp, o_sel, o_win, gates, w_o), None


def setup_inputs(seed: int = 0) -> dict:
    key = jax.random.key(seed)
    keys = iter(jax.random.split(key, 40))

    def nrm(shape, scale):
        return jax.random.normal(next(keys), shape, jnp.float32) * scale

    n_pages = PAST_LEN // PAGE_SIZE
    n_used = DEC_BATCH * n_pages
    n_pool = n_used + -(-n_used // 4)
    page_table = jax.random.permutation(next(keys), n_pool)[:n_used].reshape(DEC_BATCH, n_pages).astype(jnp.int32)
    w_buf = min(WINDOW, PAST_LEN)
    kv_row = (N_KV_HEADS, 2, HEAD_DIM)
    return {
        'x_prompt': nrm((BATCH, SEQ, D_MODEL), 1.0),
        'x_sample': nrm((DEC_BATCH, DEC_SEQ, D_MODEL), 1.0),
        'cache_cmp_kv': nrm((n_pool, PAGE_SIZE) + kv_row, 1.0),
        'cache_sel_kv': nrm((n_pool, PAGE_SIZE) + kv_row, 1.0),
        'cache_win_kv': nrm((DEC_BATCH, w_buf) + kv_row, 1.0),
        'page_table': page_table,
        'c_prompt': nrm((BATCH, D_MODEL), 1.0),
        'c_sample': nrm((DEC_BATCH, D_MODEL), 1.0),
        'w_ada': nrm((DEPTH, 2, D_MODEL, 3 * D_MODEL), 0.5 * D_MODEL ** -0.5),
        'b_ada': nrm((DEPTH, 2, 3 * D_MODEL), 0.02),
        'ln_g': 1.0 + nrm((DEPTH, 2, D_MODEL), 0.02),
        'ln_b': nrm((DEPTH, 2, D_MODEL), 0.02),
        'a_w_in': nrm((N_A_LAYERS, D_MODEL, 2 * E_A), D_MODEL ** -0.5),
        'a_v_g': 1.0 + nrm((N_A_LAYERS, E_A), 0.02),
        'a_v_b': nrm((N_A_LAYERS, E_A), 0.02),
        'a_w_s': nrm((N_A_LAYERS, G_A, CHUNK, CHUNK), CHUNK ** -0.5),
        'a_b_s': 1.0 + nrm((N_A_LAYERS, G_A, CHUNK), 0.02),
        'a_w_out': nrm((N_A_LAYERS, E_A, D_MODEL), E_A ** -0.5 * DEEPNORM_BETA),
        'w_kv': nrm((D_MODEL, N_BRANCH * N_KV_HEADS * 2 * HEAD_DIM), D_MODEL ** -0.5),
        'cmp_pe': nrm((2, CMP_LEN, HEAD_DIM), 0.1),
        'cmp_w1': nrm((2, CMP_LEN, HEAD_DIM, CMP_HIDDEN), (CMP_LEN * HEAD_DIM) ** -0.5),
        'cmp_b1': nrm((2, CMP_HIDDEN), 0.02),
        'cmp_w2': nrm((2, CMP_HIDDEN, HEAD_DIM), CMP_HIDDEN ** -0.5),
        'cmp_b2': nrm((2, HEAD_DIM), 0.02),
        'b_w_qg': nrm((N_B_LAYERS, D_MODEL, Q_WIDTH + N_Q_HEADS * N_BRANCH), D_MODEL ** -0.5),
        'b_w_o': nrm((N_B_LAYERS, Q_WIDTH, D_MODEL), Q_WIDTH ** -0.5 * DEEPNORM_BETA),
        'ffn_w_in': nrm((DEPTH, D_MODEL, 2 * D_FF), D_MODEL ** -0.5),
        'ffn_w_out': nrm((DEPTH, D_FF, D_MODEL), D_FF ** -0.5 * DEEPNORM_BETA),
    }


def reference(x_prompt, x_sample, cache_cmp_kv, cache_sel_kv, cache_win_kv, page_table, c_prompt, c_sample,
              w_ada, b_ada, ln_g, ln_b, a_w_in, a_v_g, a_v_b, a_w_s, a_b_s, a_w_out,
              w_kv, cmp_pe, cmp_w1, cmp_b1, cmp_w2, cmp_b2, b_w_qg, b_w_o, ffn_w_in, ffn_w_out):
    cmp_params = (cmp_pe, cmp_w1, cmp_b1, cmp_w2, cmp_b2)
    xp, xs = x_prompt, x_sample
    chunk_v = []
    for layer in range(DEPTH):
        if layer < N_A_LAYERS:
            mix = functools.partial(chunk_gmlp, w_in=a_w_in[layer], v_g=a_v_g[layer], v_b=a_v_b[layer],
                                    w_s=a_w_s[layer], b_s=a_b_s[layer], w_out=a_w_out[layer])
            xp, _ = modulated_sublayer(xp, c_prompt, w_ada[layer, 0], b_ada[layer, 0], ln_g[layer, 0], ln_b[layer, 0], mix)
            xs, v_rows = modulated_sublayer(xs, c_sample, w_ada[layer, 0], b_ada[layer, 0], ln_g[layer, 0], ln_b[layer, 0], mix)
            chunk_v.append(v_rows)
        else:
            if layer == N_A_LAYERS:
                cmp_p, sel_p, win_p = shared_kv(xp, w_kv)
                cmp_s, sel_s, win_s = shared_kv(xs, w_kv)
                cmp_blocks_p = compress_blocks(cmp_p, *cmp_params)
                past_cmp = cache_cmp_kv[page_table].reshape(xs.shape[0], -1, N_KV_HEADS, 2, HEAD_DIM)
                cmp_blocks_s = compress_blocks(jnp.concatenate([past_cmp, cmp_s], axis=1), *cmp_params)
                win_keys_s = jnp.concatenate([cache_win_kv, win_s], axis=1)
            j = layer - N_A_LAYERS
            mix_p = functools.partial(nsa_prompt, w_qg=b_w_qg[j], w_o=b_w_o[j], kv_cmp_blocks=cmp_blocks_p,
                                      kv_sel_rows=sel_p, kv_win_rows=win_p)
            mix_s = functools.partial(nsa_sample, w_qg=b_w_qg[j], w_o=b_w_o[j], kv_cmp_blocks=cmp_blocks_s,
                                      pool_sel=cache_sel_kv, page_table=page_table, kv_sel_new=sel_s,
                                      win_keys=win_keys_s)
            xp, _ = modulated_sublayer(xp, c_prompt, w_ada[layer, 0], b_ada[layer, 0], ln_g[layer, 0], ln_b[layer, 0], mix_p)
            xs, _ = modulated_sublayer(xs, c_sample, w_ada[layer, 0], b_ada[layer, 0], ln_g[layer, 0], ln_b[layer, 0], mix_s)
        ffn = functools.partial(swiglu_ffn, w_in=ffn_w_in[layer], w_out=ffn_w_out[layer])
        xp, _ = modulated_sublayer(xp, c_prompt, w_ada[layer, 1], b_ada[layer, 1], ln_g[layer, 1], ln_b[layer, 1], ffn)
        xs, _ = modulated_sublayer(xs, c_sample, w_ada[layer, 1], b_ada[layer, 1], ln_g[layer, 1], ln_b[layer, 1], ffn)
    n_win_p = min(WINDOW, x_prompt.shape[1])
    n_win_s = cache_win_kv.shape[1]
    new_chunk_v_sample = jnp.stack(chunk_v)
    return (xp, xs, cmp_p, sel_p, win_p[:, -n_win_p:], cmp_s, sel_s, win_keys_s[:, -n_win_s:], new_chunk_v_sample)
```

```python
import functools

import numpy as np
import jax
import jax.numpy as jnp
from jax import lax
from jax.experimental import pallas as pl
from jax.experimental.pallas import tpu as pltpu

BF = jnp.bfloat16
F32 = jnp.float32

LN_EPS = 1e-5
CHUNK = 128
HEAD_DIM = 128
N_KV_HEADS = 4
CMP_LEN = 32
CMP_STRIDE = 16
SEL_BLOCK = 64
N_SEL = 16
N_LOCAL_SEL = 2
FORCE_BONUS = 1e4
WINDOW = 512
Q_BLOCK = 128
PAGE_SIZE = 128

LANES = 128
MIB = 1024 * 1024


def _cparams(n_axes, vmem_mib):
    return pltpu.CompilerParams(dimension_semantics=("arbitrary",) * n_axes,
                                vmem_limit_bytes=vmem_mib * MIB)


def _dot(a, b):
    return jnp.dot(a, b, preferred_element_type=F32)


def _dot_t(a, b):
    return lax.dot_general(a, b, (((1,), (1,)), ((), ())), preferred_element_type=F32)


def _layer_norm(x, g, b):
    mu = jnp.mean(x, axis=-1, keepdims=True)
    xc = x - mu
    var = jnp.mean(xc * xc, axis=-1, keepdims=True)
    return xc * lax.rsqrt(var + LN_EPS) * g + b


def _masked_softmax(s, valid):
    s = jnp.where(valid, s, -jnp.inf)
    m = jnp.max(s, axis=-1, keepdims=True)
    m = jnp.where(m == -jnp.inf, 0.0, m)
    e = jnp.exp(s - m)
    d = jnp.maximum(jnp.sum(e, axis=-1, keepdims=True), 1e-30)
    return e * (1.0 / d)


def _log2(n):
    assert n & (n - 1) == 0, "power of two expected"
    return n.bit_length() - 1


def _div_pow2(x, n):
    return lax.shift_right_logical(x, jnp.int32(_log2(n)))


def _mod_pow2(x, n):
    assert n & (n - 1) == 0, "power of two expected"
    return lax.bitwise_and(x, jnp.int32(n - 1))


def _rank_desc(imp, n):
    j_io = lax.broadcasted_iota(jnp.int32, (1, imp.shape[1]), 1)
    rank = jnp.zeros(imp.shape, F32)
    for jp in range(n):
        col = imp[:, jp:jp + 1]
        before = jnp.where(col == imp, jnp.where(jp < j_io, 1.0, 0.0), jnp.where(col > imp, 1.0, 0.0))
        rank = rank + before
    return rank


def _ada_kernel(c_ref, w_ref, b_ref, o_ref):
    a = jax.nn.silu(c_ref[...]).astype(BF)
    o_ref[...] = _dot(a, w_ref[...].astype(BF)) + b_ref[...]


def _ada_modulation(c_all, w_ada, b_ada):
    n_sub = w_ada.shape[0] * w_ada.shape[1]
    d, n3 = w_ada.shape[2], w_ada.shape[3]
    m = c_all.shape[0]
    tn = 512
    w = w_ada.reshape(n_sub, d, n3)
    b = b_ada.reshape(n_sub, 1, n3)
    return pl.pallas_call(
        _ada_kernel,
        out_shape=jax.ShapeDtypeStruct((n_sub, m, n3), F32),
        grid=(n_sub, n3 // tn),
        in_specs=[pl.BlockSpec((m, d), lambda s, j: (0, 0)),
                  pl.BlockSpec((None, d, tn), lambda s, j: (s, 0, j)),
                  pl.BlockSpec((None, 1, tn), lambda s, j: (s, 0, j))],
        out_specs=pl.BlockSpec((None, m, tn), lambda s, j: (s, 0, j)),
        compiler_params=_cparams(2, 32),
        name="ada_modulation",
    )(c_all, w, b)


def _mod_spec(mod, tiles_per_group):
    rb, d = mod.shape[1], mod.shape[2]
    return pl.BlockSpec((None, rb, d), lambda i, j: (i // tiles_per_group, 0, 0))


def _gmlp_in_kernel(x_ref, sc_ref, sh_ref, w_ref, vg_ref, vb_ref, ws_ref, bs_ref, *rest,
                    n_j, tn, chunked):
    if chunked:
        o_ref, h_scr, z_scr = rest
    else:
        o_ref, v_ref, h_scr, z_scr = rest
    j = pl.program_id(1)

    @pl.when(j == 0)
    def _():
        h_scr[...] = (x_ref[...] * (1.0 + sc_ref[...]) + sh_ref[...]).astype(BF)

    z_scr[j] = jax.nn.gelu(_dot(h_scr[...], w_ref[...]))

    @pl.when(j == n_j - 1)
    def _():
        tm = z_scr.shape[1]
        half = n_j // 2
        e_a = half * tn
        s1 = jnp.zeros((tm, 1), F32)
        for jj in range(half):
            s1 = s1 + jnp.sum(z_scr[half + jj], axis=-1, keepdims=True)
        mu = s1 / e_a
        s2 = jnp.zeros((tm, 1), F32)
        for jj in range(half):
            vc = z_scr[half + jj] - mu
            s2 = s2 + jnp.sum(vc * vc, axis=-1, keepdims=True)
        rstd = lax.rsqrt(s2 / e_a + LN_EPS)
        gpt = tn // CHUNK
        if chunked:
            row = lax.broadcasted_iota(jnp.int32, (CHUNK, CHUNK), 0)
            col = lax.broadcasted_iota(jnp.int32, (CHUNK, CHUNK), 1)
            for jj in range(half):
                for gi in range(gpt):
                    g = jj * gpt + gi
                    cs = slice(gi * CHUNK, (gi + 1) * CHUNK)
                    wm = jnp.where(row >= col, ws_ref[g], 0.0).astype(BF)
                    bcol = bs_ref[:, g:g + 1]
                    vg = vg_ref[:, g * CHUNK:(g + 1) * CHUNK]
                    vb = vb_ref[:, g * CHUNK:(g + 1) * CHUNK]
                    for c in range(tm // CHUNK):
                        rs = slice(c * CHUNK, (c + 1) * CHUNK)
                        vn = (z_scr[half + jj, rs, cs] - mu[rs]) * rstd[rs] * vg + vb
                        mixed = _dot(wm, vn.astype(BF)) + bcol
                        o_ref[rs, g * CHUNK:(g + 1) * CHUNK] = (z_scr[jj, rs, cs] * mixed).astype(BF)
        else:
            for jj in range(half):
                cs = slice(jj * tn, (jj + 1) * tn)
                vn = (z_scr[half + jj] - mu) * rstd * vg_ref[:, cs] + vb_ref[:, cs]
                v_ref[:, cs] = vn
                mixed = ws_ref[:, cs] * vn + bs_ref[:, cs]
                o_ref[:, cs] = (z_scr[jj] * mixed).astype(BF)


def _gmlp_in(x, sc, sh, w_in, v_g, v_b, w_s, b_s, *, tm, tiles_per_group, chunked):
    r, d = x.shape
    n2 = w_in.shape[1]
    e_a = n2 // 2
    tn = 512
    n_j = n2 // tn
    n_g = w_s.shape[0]
    if chunked:
        ws_arr = w_s
        bs_arr = b_s.T
        ws_spec = pl.BlockSpec(ws_arr.shape, lambda i, j: (0, 0, 0))
        bs_spec = pl.BlockSpec(bs_arr.shape, lambda i, j: (0, 0))
        out_shape = jax.ShapeDtypeStruct((r, e_a), BF)
        out_specs = pl.BlockSpec((tm, e_a), lambda i, j: (i, 0))
    else:
        ws_arr = jnp.repeat(w_s[:, 0, 0], e_a // n_g).reshape(1, e_a)
        bs_arr = jnp.repeat(b_s[:, 0], e_a // n_g).reshape(1, e_a)
        ws_spec = pl.BlockSpec((1, e_a), lambda i, j: (0, 0))
        bs_spec = pl.BlockSpec((1, e_a), lambda i, j: (0, 0))
        out_shape = (jax.ShapeDtypeStruct((r, e_a), BF), jax.ShapeDtypeStruct((r, e_a), F32))
        out_specs = (pl.BlockSpec((tm, e_a), lambda i, j: (i, 0)), pl.BlockSpec((tm, e_a), lambda i, j: (i, 0)))
    return pl.pallas_call(
        functools.partial(_gmlp_in_kernel, n_j=n_j, tn=tn, chunked=chunked),
        out_shape=out_shape,
        grid=(r // tm, n_j),
        in_specs=[pl.BlockSpec((tm, d), lambda i, j: (i, 0)),
                  _mod_spec(sc, tiles_per_group), _mod_spec(sh, tiles_per_group),
                  pl.BlockSpec((d, tn), lambda i, j: (0, j)),
                  pl.BlockSpec((1, e_a), lambda i, j: (0, 0)),
                  pl.BlockSpec((1, e_a), lambda i, j: (0, 0)),
                  ws_spec, bs_spec],
        out_specs=out_specs,
        scratch_shapes=[pltpu.VMEM((tm, d), BF), pltpu.VMEM((n_j, tm, tn), F32)],
        compiler_params=_cparams(2, 48),
        name="gmlp_in_chunked" if chunked else "gmlp_in_single",
    )(x, sc, sh, w_in, v_g.reshape(1, e_a), v_b.reshape(1, e_a), ws_arr, bs_arr)


def _proj_res_ln_kernel(a_ref, w_ref, x_ref, gate_ref, lg_ref, lb_ref, *rest, n_k, alpha, emit_next):
    if emit_next:
        sc_ref, sh_ref, y_ref, h_ref, acc = rest
    else:
        y_ref, acc = rest
    k = pl.program_id(1)

    @pl.when(k == 0)
    def _():
        acc[...] = jnp.zeros_like(acc)

    acc[...] += _dot(a_ref[...].astype(BF), w_ref[...])

    @pl.when(k == n_k - 1)
    def _():
        t = alpha * x_ref[...] + (1.0 + gate_ref[...]) * acc[...]
        y = _layer_norm(t, lg_ref[...], lb_ref[...])
        y_ref[...] = y
        if emit_next:
            h_ref[...] = (y * (1.0 + sc_ref[...]) + sh_ref[...]).astype(BF)


def _proj_res_ln(a, w, x, gate, ln_g, ln_b, nxt, *, alpha, tm, tk, tiles_per_group):
    r, kdim = a.shape
    d = w.shape[1]
    n_k = kdim // tk
    emit_next = nxt is not None
    in_specs = [pl.BlockSpec((tm, tk), lambda i, k: (i, k)),
                pl.BlockSpec((tk, d), lambda i, k: (k, 0)),
                pl.BlockSpec((tm, d), lambda i, k: (i, 0)),
                _mod_spec(gate, tiles_per_group),
                pl.BlockSpec((1, d), lambda i, k: (0, 0)),
                pl.BlockSpec((1, d), lambda i, k: (0, 0))]
    args = [a, w, x, gate, ln_g.reshape(1, d), ln_b.reshape(1, d)]
    out_shape = [jax.ShapeDtypeStruct((r, d), F32)]
    out_specs = [pl.BlockSpec((tm, d), lambda i, k: (i, 0))]
    if emit_next:
        in_specs += [_mod_spec(nxt[0], tiles_per_group), _mod_spec(nxt[1], tiles_per_group)]
        args += [nxt[0], nxt[1]]
        out_shape.append(jax.ShapeDtypeStruct((r, d), BF))
        out_specs.append(pl.BlockSpec((tm, d), lambda i, k: (i, 0)))
    res = pl.pallas_call(
        functools.partial(_proj_res_ln_kernel, n_k=n_k, alpha=alpha, emit_next=emit_next),
        out_shape=tuple(out_shape),
        grid=(r // tm, n_k),
        in_specs=in_specs,
        out_specs=tuple(out_specs),
        scratch_shapes=[pltpu.VMEM((tm, d), F32)],
        compiler_params=_cparams(2, 48),
        name="proj_res_ln",
    )(*args)
    return res if emit_next else (res[0], None)


def _ffn_in_kernel(h_ref, wg_ref, wu_ref, o_ref):
    h = h_ref[...]
    o_ref[...] = (jax.nn.silu(_dot(h, wg_ref[...])) * _dot(h, wu_ref[...])).astype(BF)


def _ffn_in(h, w_in, *, tm):
    r, d = h.shape
    d_ff = w_in.shape[1] // 2
    tn = 512
    n_j = d_ff // tn
    return pl.pallas_call(
        _ffn_in_kernel,
        out_shape=jax.ShapeDtypeStruct((r, d_ff), BF),
        grid=(r // tm, n_j),
        in_specs=[pl.BlockSpec((tm, d), lambda i, j: (i, 0)),
                  pl.BlockSpec((d, tn), lambda i, j: (0, j)),
                  pl.BlockSpec((d, tn), lambda i, j: (0, j + n_j))],
        out_specs=pl.BlockSpec((tm, tn), lambda i, j: (i, j)),
        compiler_params=_cparams(2, 40),
        name="ffn_in",
    )(h, w_in, w_in)


def _kv_proj_kernel(y_ref, w_ref, o0_ref, o1_ref, o2_ref, xb):
    j = pl.program_id(1)

    @pl.when(j == 0)
    def _():
        xb[...] = y_ref[...].astype(BF)

    r = _dot(xb[...], w_ref[...])
    for idx, o_ref in enumerate((o0_ref, o1_ref, o2_ref)):
        @pl.when(j == idx)
        def _(o_ref=o_ref):
            o_ref[...] = r


def _kv_proj(y, w_kv, *, tm):
    r, d = y.shape
    n = w_kv.shape[1] // 3
    out = jax.ShapeDtypeStruct((r, n), F32)
    ospec = pl.BlockSpec((tm, n), lambda i, j: (i, 0))
    return pl.pallas_call(
        _kv_proj_kernel,
        out_shape=(out, out, out),
        grid=(r // tm, 3),
        in_specs=[pl.BlockSpec((tm, d), lambda i, j: (i, 0)),
                  pl.BlockSpec((d, n), lambda i, j: (0, j))],
        out_specs=(ospec, ospec, ospec),
        scratch_shapes=[pltpu.VMEM((tm, d), BF)],
        compiler_params=_cparams(2, 48),
        name="kv_proj",
    )(y, w_kv)


def _qg_proj_kernel(h_ref, wq_ref, wg_ref, q_ref, g_ref):
    h = h_ref[...]
    q_ref[...] = _dot(h, wq_ref[...]).astype(BF)

    @pl.when(pl.program_id(1) == 0)
    def _():
        g_ref[...] = jax.nn.sigmoid(_dot(h, wg_ref[...]))


def _qg_proj(h, w_q, w_g, *, tm):
    r, d = h.shape
    qw = w_q.shape[1]
    gw = w_g.shape[1]
    tn = 512
    return pl.pallas_call(
        _qg_proj_kernel,
        out_shape=(jax.ShapeDtypeStruct((r, qw), BF), jax.ShapeDtypeStruct((r, gw), F32)),
        grid=(r // tm, qw // tn),
        in_specs=[pl.BlockSpec((tm, d), lambda i, j: (i, 0)),
                  pl.BlockSpec((d, tn), lambda i, j: (0, j)),
                  pl.BlockSpec((d, gw), lambda i, j: (0, 0))],
        out_specs=(pl.BlockSpec((tm, tn), lambda i, j: (i, j)),
                   pl.BlockSpec((tm, gw), lambda i, j: (i, 0))),
        compiler_params=_cparams(2, 40),
        name="qg_proj",
    )(h, w_q, w_g)


PAGES_PER_STEP = 16
CHUNKS_PER_PAGE = PAGE_SIZE // CMP_STRIDE


def _compress_kernel(pt_ref, *refs, n_steps):
    page_refs = refs[:PAGES_PER_STEP]
    w1_ref, pe_ref, b1_ref, w2_ref, b2_ref, o_ref, p0_scr, p1_scr = refs[PAGES_PER_STEP:]
    t = pl.program_id(3)
    ncs = PAGES_PER_STEP * CHUNKS_PER_PAGE
    nc = n_steps * ncs

    @pl.when(t == 0)
    def _():
        p1_scr[pl.ds(nc, 8), :] = jnp.zeros((8, p1_scr.shape[1]), F32)

    per_s = []
    for s in range(CMP_STRIDE):
        rows = [page_refs[p][pl.ds(s, CHUNKS_PER_PAGE, stride=CMP_STRIDE), :] for p in range(PAGES_PER_STEP)]
        per_s.append(jnp.concatenate(rows, axis=0))
    x = jnp.concatenate(per_s, axis=1).astype(BF)
    pr = _dot(x, w1_ref[...])
    hidden = pr.shape[1] // 2
    row0 = pl.multiple_of(t * ncs, ncs)
    p0_scr[pl.ds(row0, ncs), :] = pr[:, :hidden]
    p1_scr[pl.ds(row0, ncs), :] = pr[:, hidden:]

    @pl.when(t == n_steps - 1)
    def _():
        cst = _dot(pe_ref[...].astype(BF), w1_ref[...])
        const = cst[0:1, :hidden] + cst[1:2, hidden:] + b1_ref[...]
        pre = p0_scr[0:nc, :] + p1_scr[1:nc + 1, :] + const
        o_ref[...] = _dot(jax.nn.gelu(pre).astype(BF), w2_ref[...]) + b2_ref[...]


def _compress(pages, page_table, w1cat, pe16, b1, w2, b2):
    nb, n_pages = page_table.shape
    n_steps = n_pages // PAGES_PER_STEP
    nc = n_pages * CHUNKS_PER_PAGE
    width = pages.shape[2]
    hidden = w2.shape[1]

    def page_spec(p):
        return pl.BlockSpec((None, PAGE_SIZE, HEAD_DIM),
                            lambda b, k, h, t, pt: (pt[b, t * PAGES_PER_STEP + p], 0, h * 2 + k))

    per_k = lambda a: pl.BlockSpec((None,) + a.shape[1:], lambda b, k, h, t, pt: (k,) + (0,) * (a.ndim - 1))
    grid_spec = pltpu.PrefetchScalarGridSpec(
        num_scalar_prefetch=1,
        grid=(nb, 2, N_KV_HEADS, n_steps),
        in_specs=[page_spec(p) for p in range(PAGES_PER_STEP)] + [
            per_k(w1cat), per_k(pe16), per_k(b1), per_k(w2), per_k(b2)],
        out_specs=pl.BlockSpec((None, nc, HEAD_DIM), lambda b, k, h, t, pt: (b, 0, h * 2 + k)),
        scratch_shapes=[pltpu.VMEM((nc, hidden), F32), pltpu.VMEM((nc + 8, hidden), F32)],
    )
    return pl.pallas_call(
        functools.partial(_compress_kernel, n_steps=n_steps),
        out_shape=jax.ShapeDtypeStruct((nb, nc, width), F32),
        grid_spec=grid_spec,
        compiler_params=_cparams(4, 32),
        name="compress_blocks",
    )(page_table, *([pages] * PAGES_PER_STEP), w1cat, pe16, b1, w2, b2)


def _overlap_map(n_cmp_rows, n_slc, n_cols):
    cmp_lo = np.arange(n_cmp_rows)[:, None] * CMP_STRIDE
    slc_lo = np.arange(n_cols)[None, :] * SEL_BLOCK
    ov = np.minimum(cmp_lo + CMP_LEN, slc_lo + SEL_BLOCK) - np.maximum(cmp_lo, slc_lo)
    w = np.clip(ov, 0, None).astype(np.float32) / CMP_LEN
    w[:, n_slc:] = 0.0
    return w


def _attn_prompt_kernel(q_ref, gt_ref, cmp_ref, sel_ref, win_ref, wmap_ref, exp_ref, o_ref, *, scale, n_slc):
    qb = pl.program_id(2)
    q4 = q_ref[...]
    nq = q4.shape[0]
    rq = q4.shape[1] // HEAD_DIM
    qs = jnp.concatenate([q4[:, r * HEAD_DIM:(r + 1) * HEAD_DIM] for r in range(rq)], axis=0)
    qpos = qb * nq + lax.broadcasted_iota(jnp.int32, (nq, 1), 0)

    def attend(s, valid, v):
        ps = [_masked_softmax(s[r * nq:(r + 1) * nq], valid).astype(BF) for r in range(rq)]
        return ps, _dot(jnp.concatenate(ps, axis=0), v)

    kc = cmp_ref[:, :HEAD_DIM].astype(BF)
    vc = cmp_ref[:, HEAD_DIM:].astype(BF)
    n_io = lax.broadcasted_iota(jnp.int32, (1, kc.shape[0]), 1)
    valid_c = (n_io * CMP_STRIDE + (CMP_LEN - 1)) <= qpos
    ps, o_cmp = attend(_dot_t(qs, kc) * scale, valid_c, vc)

    imp = _dot(jnp.concatenate(ps, axis=1), wmap_ref[...])
    j_io = lax.broadcasted_iota(jnp.int32, (1, imp.shape[1]), 1)
    q_blk = _div_pow2(qpos, SEL_BLOCK)
    causal = j_io <= q_blk
    forced = (j_io == 0) | (causal & (j_io > q_blk - N_LOCAL_SEL))
    imp = jnp.where(forced, imp + FORCE_BONUS, imp)
    imp = jnp.where(causal, imp, -jnp.inf)
    rank = _rank_desc(imp, n_slc)
    chosen = jnp.where(rank < min(N_SEL, n_slc), jnp.where(imp > -jnp.inf, 1.0, 0.0), 0.0)
    in_sel = _dot(chosen.astype(BF), exp_ref[...])

    ks = sel_ref[:, :HEAD_DIM].astype(BF)
    vs = sel_ref[:, HEAD_DIM:].astype(BF)
    t_io = lax.broadcasted_iota(jnp.int32, (1, ks.shape[0]), 1)
    valid_s = (in_sel > 0.5) & (t_io <= qpos)
    _, o_sel = attend(_dot_t(qs, ks) * scale, valid_s, vs)

    span = WINDOW + nq
    k0 = pl.multiple_of(jnp.maximum(qb * nq - WINDOW, 0), nq)
    kw = win_ref[pl.ds(k0, span), :HEAD_DIM].astype(BF)
    vw = win_ref[pl.ds(k0, span), HEAD_DIM:].astype(BF)
    kp = k0 + lax.broadcasted_iota(jnp.int32, (1, span), 1)
    valid_w = (kp <= qpos) & (kp > qpos - WINDOW)
    _, o_win = attend(_dot_t(qs, kw) * scale, valid_w, vw)

    gt = gt_ref[...]
    outs = []
    for r in range(rq):
        rs = slice(r * nq, (r + 1) * nq)
        outs.append(gt[:, 3 * r:3 * r + 1] * o_cmp[rs] + gt[:, 3 * r + 1:3 * r + 2] * o_sel[rs]
                    + gt[:, 3 * r + 2:3 * r + 3] * o_win[rs])
    o_ref[...] = jnp.concatenate(outs, axis=1).astype(BF)


def _attn_prompt(q, gates, cmp_blocks, kv_sel, kv_win, *, scale):
    nb, t, qw = q.shape
    n_g = N_KV_HEADS
    gw = qw // n_g
    kvw = 2 * HEAD_DIM
    n_cmp_rows = cmp_blocks.shape[1]
    n_slc = t // SEL_BLOCK
    rq = gw // HEAD_DIM
    wmap = jnp.asarray(np.tile(_overlap_map(n_cmp_rows, n_slc, LANES), (rq, 1)), BF)
    expand = jnp.asarray((np.arange(t)[None, :] // SEL_BLOCK == np.arange(LANES)[:, None]).astype(np.float32), BF)
    return pl.pallas_call(
        functools.partial(_attn_prompt_kernel, scale=scale, n_slc=n_slc),
        out_shape=jax.ShapeDtypeStruct((nb, t, qw), BF),
        grid=(nb, n_g, t // Q_BLOCK),
        in_specs=[pl.BlockSpec((None, Q_BLOCK, gw), lambda b, g, i: (b, i, g)),
                  pl.BlockSpec((None, Q_BLOCK, LANES), lambda b, g, i: (b, i, g)),
                  pl.BlockSpec((None, n_cmp_rows, kvw), lambda b, g, i: (b, 0, g)),
                  pl.BlockSpec((None, t, kvw), lambda b, g, i: (b, 0, g)),
                  pl.BlockSpec((None, t, kvw), lambda b, g, i: (b, 0, g)),
                  pl.BlockSpec(wmap.shape, lambda b, g, i: (0, 0)),
                  pl.BlockSpec(expand.shape, lambda b, g, i: (0, 0))],
        out_specs=pl.BlockSpec((None, Q_BLOCK, gw), lambda b, g, i: (b, i, g)),
        compiler_params=_cparams(3, 56),
        name="attn_prompt",
    )(q, gates, cmp_blocks, kv_sel, kv_win, wmap, expand)


def _attn_sample_a_kernel(q_ref, cmp_ref, win_ref, wmap_ref, ocmp_ref, owin_ref, idx_ref, val_ref, *,
                          scale, q_pos, n_cmp, n_slc, win_pos0):
    q = q_ref[...]
    n_h = q.shape[0]
    rq = n_h // N_KV_HEADS
    row_g = _div_pow2(lax.broadcasted_iota(jnp.int32, (n_h, 1), 0), rq)
    n_io = lax.broadcasted_iota(jnp.int32, (1, cmp_ref.shape[0]), 1)
    valid_c = (n_io < n_cmp) & ((n_io * CMP_STRIDE + (CMP_LEN - 1)) <= q_pos)
    kp = win_pos0 + lax.broadcasted_iota(jnp.int32, (1, win_ref.shape[0]), 1)
    valid_w = (kp <= q_pos) & (kp > q_pos - WINDOW)
    o_cmp = jnp.zeros((n_h, HEAD_DIM), F32)
    o_win = jnp.zeros((n_h, HEAD_DIM), F32)
    imp = jnp.zeros((n_h, wmap_ref.shape[1]), F32)
    for g in range(N_KV_HEADS):
        in_g = row_g == g
        c0 = g * 2 * HEAD_DIM
        kc = cmp_ref[:, c0:c0 + HEAD_DIM].astype(BF)
        vc = cmp_ref[:, c0 + HEAD_DIM:c0 + 2 * HEAD_DIM].astype(BF)
        p = _masked_softmax(_dot_t(q, kc) * scale, valid_c).astype(BF)
        o_cmp = jnp.where(in_g, _dot(p, vc), o_cmp)
        imp_rows = _dot(p, wmap_ref[...])
        imp_g = jnp.sum(jnp.where(in_g, imp_rows, 0.0), axis=0, keepdims=True)
        imp = jnp.where(in_g, imp_g, imp)
        kw = win_ref[:, c0:c0 + HEAD_DIM].astype(BF)
        vw = win_ref[:, c0 + HEAD_DIM:c0 + 2 * HEAD_DIM].astype(BF)
        pw = _masked_softmax(_dot_t(q, kw) * scale, valid_w).astype(BF)
        o_win = jnp.where(in_g, _dot(pw, vw), o_win)
    ocmp_ref[...] = o_cmp
    owin_ref[...] = o_win

    j_io = lax.broadcasted_iota(jnp.int32, (1, imp.shape[1]), 1)
    q_blk = q_pos // SEL_BLOCK
    causal = j_io <= q_blk
    forced = (j_io == 0) | (causal & (j_io > q_blk - N_LOCAL_SEL))
    imp = jnp.where(forced, imp + FORCE_BONUS, imp)
    imp = jnp.where(causal, imp, -jnp.inf)
    rank = _rank_desc(imp, n_slc)
    jf = j_io.astype(F32)
    k_io = lax.broadcasted_iota(jnp.int32, (1, idx_ref.shape[1]), 1)
    idx = jnp.zeros(idx_ref.shape, F32)
    val = jnp.zeros(idx_ref.shape, F32)
    for k in range(min(N_SEL, n_slc)):
        hit = rank == k
        ik = jnp.sum(jnp.where(hit, jf, 0.0), axis=1, keepdims=True)
        vk = jnp.sum(jnp.where(hit, jnp.where(imp > -jnp.inf, 1.0, 0.0), 0.0), axis=1, keepdims=True)
        idx = jnp.where(k_io == k, ik, idx)
        val = jnp.where(k_io == k, vk, val)
    idx_ref[...] = idx.astype(jnp.int32)
    val_ref[...] = val.astype(jnp.int32)


def _attn_sample_a(q, cmp_blocks, win_rows, *, scale, q_pos, n_cmp, n_slc, win_pos0):
    nb, n_h, d = q.shape
    n_cols = -(-n_slc // LANES) * LANES
    wmap = jnp.asarray(_overlap_map(cmp_blocks.shape[1], n_slc, n_cols), BF)
    o_sds = jax.ShapeDtypeStruct((nb, n_h, d), F32)
    i_sds = jax.ShapeDtypeStruct((nb, n_h, LANES), jnp.int32)
    blk = lambda a: pl.BlockSpec((None,) + a.shape[1:], lambda b: (b, 0, 0))
    ospec = pl.BlockSpec((None, n_h, d), lambda b: (b, 0, 0))
    ispec = pl.BlockSpec((None, n_h, LANES), lambda b: (b, 0, 0))
    return pl.pallas_call(
        functools.partial(_attn_sample_a_kernel, scale=scale, q_pos=q_pos, n_cmp=n_cmp, n_slc=n_slc,
                          win_pos0=win_pos0),
        out_shape=(o_sds, o_sds, i_sds, i_sds),
        grid=(nb,),
        in_specs=[blk(q), blk(cmp_blocks), blk(win_rows), pl.BlockSpec(wmap.shape, lambda b: (0, 0))],
        out_specs=(ospec, ospec, ispec, ispec),
        compiler_params=_cparams(1, 32),
        name="attn_sample_a",
    )(q, cmp_blocks, win_rows, wmap)


def _attn_sample_b_kernel(idx_ref, val_ref, pt_ref, q_ref, new_ref, gt_ref, ocmp_ref, owin_ref, *rest,
                          scale, q_pos, past_len):
    blk_refs = rest[:N_SEL]
    o_ref = rest[N_SEL]
    b = pl.program_id(0)
    g = pl.program_id(1)
    q = q_ref[...]
    n_h = q.shape[0]
    rq = n_h // N_KV_HEADS
    base = (b * N_KV_HEADS + g) * N_SEL
    n_keys = N_SEL * SEL_BLOCK
    lane = lax.broadcasted_iota(jnp.int32, (1, n_keys), 1)
    k_new = new_ref[:, :HEAD_DIM]
    v_new = new_ref[:, HEAD_DIM:]
    ks, vs = [], []
    kpos = _mod_pow2(lane, SEL_BLOCK)
    kval = jnp.zeros((1, n_keys), jnp.int32)
    lane_blk = _div_pow2(lane, SEL_BLOCK)
    for k in range(N_SEL):
        blk = idx_ref[base + k]
        is_new = blk * SEL_BLOCK >= past_len
        ks.append(jnp.where(is_new, k_new, blk_refs[k][:, :HEAD_DIM]).astype(BF))
        vs.append(jnp.where(is_new, v_new, blk_refs[k][:, HEAD_DIM:]).astype(BF))
        in_k = lane_blk == k
        kpos = kpos + jnp.where(in_k, blk * SEL_BLOCK, 0)
        kval = jnp.where(in_k, val_ref[base + k], kval)
    valid = (kval > 0) & (kpos <= q_pos)
    p = _masked_softmax(_dot_t(q, jnp.concatenate(ks, axis=0)) * scale, valid).astype(BF)
    o_sel = _dot(p, jnp.concatenate(vs, axis=0))
    gt = gt_ref[...]
    in_g = _div_pow2(lax.broadcasted_iota(jnp.int32, (n_h, 1), 0), rq) == g

    @pl.when(g == 0)
    def _():
        o_ref[...] = gt[:, 0:1] * ocmp_ref[...] + gt[:, 2:3] * owin_ref[...]

    o_ref[...] += jnp.where(in_g, gt[:, 1:2] * o_sel, 0.0)


def _attn_sample_b(sel_idx, sel_val, page_table, q, kv_sel_new, gates, o_cmp, o_win, pool_sel, *,
                   scale, q_pos, past_len):
    nb, n_h, d = q.shape
    kvw = 2 * HEAD_DIM
    halves = PAGE_SIZE // SEL_BLOCK
    pool = pool_sel.reshape(pool_sel.shape[0] * halves, SEL_BLOCK, N_KV_HEADS * kvw)

    def blk_spec(k):
        def index_map(b, g, idx, val, pt):
            row0 = jnp.minimum(idx[(b * N_KV_HEADS + g) * N_SEL + k] * SEL_BLOCK, past_len - 1)
            return (pt[b, row0 // PAGE_SIZE] * halves + (row0 % PAGE_SIZE) // SEL_BLOCK, 0, g)
        return pl.BlockSpec((None, SEL_BLOCK, kvw), index_map)

    per_b = lambda a: pl.BlockSpec((None,) + a.shape[1:], lambda b, g, idx, val, pt: (b, 0, 0))
    grid_spec = pltpu.PrefetchScalarGridSpec(
        num_scalar_prefetch=3,
        grid=(nb, N_KV_HEADS),
        in_specs=[per_b(q),
                  pl.BlockSpec((None, None, 1, kvw), lambda b, g, idx, val, pt: (b, g, 0, 0)),
                  per_b(gates), per_b(o_cmp), per_b(o_win)] + [blk_spec(k) for k in range(N_SEL)],
        out_specs=pl.BlockSpec((None, n_h, d), lambda b, g, idx, val, pt: (b, 0, 0)),
    )
    return pl.pallas_call(
        functools.partial(_attn_sample_b_kernel, scale=scale, q_pos=q_pos, past_len=past_len),
        out_shape=jax.ShapeDtypeStruct((nb, n_h, d), F32),
        grid_spec=grid_spec,
        compiler_params=_cparams(2, 32),
        name="attn_sample_b",
    )(sel_idx, sel_val, page_table, q, kv_sel_new.reshape(nb, N_KV_HEADS, 1, kvw), gates, o_cmp, o_win,
      *([pool] * N_SEL))


def _largest_tile(n, cap):
    t = cap
    while n % t:
        t -= LANES
    return t


def kernel(x_prompt, x_sample, cache_cmp_kv, cache_sel_kv, cache_win_kv, page_table, c_prompt, c_sample, w_ada, b_ada, ln_g, ln_b, a_w_in, a_v_g, a_v_b, a_w_s, a_b_s, a_w_out, w_kv, cmp_pe, cmp_w1, cmp_b1, cmp_w2, cmp_b2, b_w_qg, b_w_o, ffn_w_in, ffn_w_out):
    nb, t, d = x_prompt.shape
    ns, tq, _ = x_sample.shape
    assert tq == 1, "the sample path handles one new token per sequence"
    depth = w_ada.shape[0]
    n_a = a_w_in.shape[0]
    alpha = float((2 * depth) ** 0.25)
    scale = float(HEAD_DIM ** -0.5)
    n_pool = cache_cmp_kv.shape[0]
    past_len = page_table.shape[1] * PAGE_SIZE
    kvw = N_KV_HEADS * 2 * HEAD_DIM
    q_width = b_w_o.shape[1]
    n_heads = q_width // HEAD_DIM
    rq = n_heads // N_KV_HEADS

    mod = _ada_modulation(jnp.concatenate([c_prompt, c_sample], axis=0), w_ada, b_ada)

    def mods(layer, sub):
        m = mod[layer * 2 + sub]
        out = []
        for part in range(3):
            sl = m[:, part * d:(part + 1) * d]
            out.append((sl[:nb].reshape(nb, 1, d), sl[nb:].reshape(1, ns, d)))
        return out

    wb = lambda w: w.astype(BF)
    hidden = cmp_w1.shape[-1]
    w1cat = wb(cmp_w1.reshape(2, CMP_LEN // CMP_STRIDE, CMP_STRIDE, HEAD_DIM, hidden)
               .transpose(0, 2, 3, 1, 4).reshape(2, CMP_STRIDE * HEAD_DIM, (CMP_LEN // CMP_STRIDE) * hidden))
    pe16 = jnp.pad(cmp_pe.reshape(2, CMP_LEN // CMP_STRIDE, CMP_STRIDE * HEAD_DIM), ((0, 0), (0, 14), (0, 0)))
    b1 = cmp_b1.reshape(2, 1, hidden)
    b2 = cmp_b2.reshape(2, 1, HEAD_DIM)
    w2 = wb(cmp_w2)

    tm_p = 512
    tpg_p = t // tm_p
    prompt = dict(tm=tm_p, tiles_per_group=tpg_p)
    sample = dict(tm=ns, tiles_per_group=1)

    xp = x_prompt.reshape(nb * t, d)
    xs = x_sample.reshape(ns, d)
    hp = hs = None
    chunk_v = []
    outs = {}
    for layer in range(depth):
        (sh_p, sh_s), (sc_p, sc_s), (gt_p, gt_s) = mods(layer, 0)
        (fsh_p, fsh_s), (fsc_p, fsc_s), (fgt_p, fgt_s) = mods(layer, 1)
        if layer < n_a:
            w_in = wb(a_w_in[layer])
            w_out = wb(a_w_out[layer])
            tk = _largest_tile(w_out.shape[0], 1024)
            ap = _gmlp_in(xp, sc_p, sh_p, w_in, a_v_g[layer], a_v_b[layer], a_w_s[layer], a_b_s[layer],
                          chunked=True, **prompt)
            as_, v_rows = _gmlp_in(xs, sc_s, sh_s, w_in, a_v_g[layer], a_v_b[layer], a_w_s[layer], a_b_s[layer],
                                   chunked=False, **sample)
            chunk_v.append(v_rows.reshape(ns, tq, -1))
            xp, hp = _proj_res_ln(ap, w_out, xp, gt_p, ln_g[layer, 0], ln_b[layer, 0], (fsc_p, fsh_p),
                                  alpha=alpha, tk=tk, **prompt)
            xs, hs = _proj_res_ln(as_, w_out, xs, gt_s, ln_g[layer, 0], ln_b[layer, 0], (fsc_s, fsh_s),
                                  alpha=alpha, tk=tk, **sample)
        else:
            if layer == n_a:
                w_kvb = wb(w_kv)
                cmp_p, sel_p, win_p = _kv_proj(xp, w_kvb, tm=tm_p)
                cmp_s, sel_s, win_s = _kv_proj(xs, w_kvb, tm=ns)
                pages_per_seq_p = t // PAGE_SIZE
                ident = jnp.arange(nb * pages_per_seq_p, dtype=jnp.int32).reshape(nb, pages_per_seq_p)
                cmp_blocks_p = _compress(cmp_p.reshape(nb * pages_per_seq_p, PAGE_SIZE, kvw), ident,
                                         w1cat, pe16, b1, w2, b2)
                assert (past_len + tq) // CMP_STRIDE == past_len // CMP_STRIDE
                cmp_blocks_s = _compress(cache_cmp_kv.reshape(n_pool, PAGE_SIZE, kvw), page_table,
                                         w1cat, pe16, b1, w2, b2)
                n_win = cache_win_kv.shape[1]
                win_keys_s = jnp.concatenate([cache_win_kv.reshape(ns, n_win, kvw), win_s.reshape(ns, tq, kvw)],
                                             axis=1)[:, -n_win:]
                kv_p = [a.reshape(nb, t, kvw) for a in (cmp_p, sel_p, win_p)]
                outs["kv_p"] = kv_p
                outs["kv_s"] = (cmp_s, sel_s, win_keys_s)
            j = layer - n_a
            w_qg = b_w_qg[j]
            w_q = wb(w_qg[:, :q_width])
            w_gate = w_qg[:, q_width:].reshape(d, N_KV_HEADS, rq * 3)
            w_gate = wb(jnp.pad(w_gate, ((0, 0), (0, 0), (0, LANES - rq * 3))).reshape(d, N_KV_HEADS * LANES))
            w_o = wb(b_w_o[j])
            tk = _largest_tile(w_o.shape[0], 1024)
            assert layer > 0, "an attention layer must follow another layer"
            q_p, g_p = _qg_proj(hp, w_q, w_gate, tm=tm_p)
            q_s, g_s = _qg_proj(hs, w_q, w_gate, tm=ns)
            o_p = _attn_prompt(q_p.reshape(nb, t, q_width), g_p.reshape(nb, t, N_KV_HEADS * LANES),
                               cmp_blocks_p, outs["kv_p"][1], outs["kv_p"][2], scale=scale)
            n_cmp_s = (past_len + tq) // CMP_STRIDE - CMP_LEN // CMP_STRIDE + 1
            n_slc_s = -(-(past_len + tq) // SEL_BLOCK)
            q_s3 = q_s.reshape(ns, n_heads, HEAD_DIM)
            o_cmp_s, o_win_s, idx_s, val_s = _attn_sample_a(
                q_s3, cmp_blocks_s, outs["kv_s"][2], scale=scale, q_pos=past_len, n_cmp=n_cmp_s, n_slc=n_slc_s,
                win_pos0=past_len + tq - outs["kv_s"][2].shape[1])
            sel_idx = idx_s[:, ::rq, :N_SEL].reshape(-1)
            sel_val = val_s[:, ::rq, :N_SEL].reshape(-1)
            gates_s = g_s.reshape(ns, N_KV_HEADS, LANES)[:, :, :rq * 3].reshape(ns, n_heads, 3)
            o_s = _attn_sample_b(sel_idx, sel_val, page_table, q_s3, outs["kv_s"][1], gates_s, o_cmp_s, o_win_s,
                                 cache_sel_kv.reshape(n_pool, PAGE_SIZE, kvw), scale=scale, q_pos=past_len,
                                 past_len=past_len)
            xp, hp = _proj_res_ln(o_p.reshape(nb * t, q_width), w_o, xp, gt_p, ln_g[layer, 0], ln_b[layer, 0],
                                  (fsc_p, fsh_p), alpha=alpha, tk=tk, **prompt)
            xs, hs = _proj_res_ln(o_s.reshape(ns, q_width), w_o, xs, gt_s, ln_g[layer, 0], ln_b[layer, 0],
                                  (fsc_s, fsh_s), alpha=alpha, tk=tk, **sample)
        f_in = wb(ffn_w_in[layer])
        f_out = wb(ffn_w_out[layer])
        tk = _largest_tile(f_out.shape[0], 1408)
        fp = _ffn_in(hp, f_in, tm=2 * tm_p)
        fs = _ffn_in(hs, f_in, tm=ns)
        nxt_p = nxt_s = None
        if n_a <= layer + 1 < depth:
            (nsh_p, nsh_s), (nsc_p, nsc_s), _ = mods(layer + 1, 0)
            nxt_p, nxt_s = (nsc_p, nsh_p), (nsc_s, nsh_s)
        xp, hp = _proj_res_ln(fp, f_out, xp, fgt_p, ln_g[layer, 1], ln_b[layer, 1], nxt_p, alpha=alpha, tk=tk, **prompt)
        xs, hs = _proj_res_ln(fs, f_out, xs, fgt_s, ln_g[layer, 1], ln_b[layer, 1], nxt_s, alpha=alpha, tk=tk, **sample)

    cmp_p, sel_p, win_p = outs["kv_p"]
    cmp_s, sel_s, win_keys_s = outs["kv_s"]
    kv_shape = (N_KV_HEADS, 2, HEAD_DIM)
    n_win_p = min(WINDOW, t)
    return (xp.reshape(nb, t, d), xs.reshape(ns, tq, d),
            cmp_p.reshape((nb, t) + kv_shape), sel_p.reshape((nb, t) + kv_shape),
            win_p[:, -n_win_p:].reshape((nb, n_win_p) + kv_shape),
            cmp_s.reshape((ns, tq) + kv_shape), sel_s.reshape((ns, tq) + kv_shape),
            win_keys_s.reshape((ns, win_keys_s.shape[1]) + kv_shape),
            jnp.stack(chunk_v))
```

```python
import functools

import numpy as np
import jax
import jax.numpy as jnp
from jax import lax
from jax.experimental import pallas as pl
from jax.experimental.pallas import tpu as pltpu

BF = jnp.bfloat16
F32 = jnp.float32

LN_EPS = 1e-5
CHUNK = 128
HEAD_DIM = 128
N_KV_HEADS = 4
CMP_LEN = 32
CMP_STRIDE = 16
SEL_BLOCK = 64
N_SEL = 16
N_LOCAL_SEL = 2
FORCE_BONUS = 1e4
LOG2_E = 1.4426950408889634
WINDOW = 512
Q_BLOCK = 128
PAGE_SIZE = 128

LANES = 128
MIB = 1024 * 1024


def _cparams(n_axes, vmem_mib):
    return pltpu.CompilerParams(dimension_semantics=("arbitrary",) * n_axes,
                                vmem_limit_bytes=vmem_mib * MIB)


def _dot(a, b):
    return jnp.dot(a, b, preferred_element_type=F32)


def _dot_t(a, b):
    return lax.dot_general(a, b, (((1,), (1,)), ((), ())), preferred_element_type=F32)


def _layer_norm(x, g, b):
    mu = jnp.mean(x, axis=-1, keepdims=True)
    xc = x - mu
    var = jnp.mean(xc * xc, axis=-1, keepdims=True)
    return xc * lax.rsqrt(var + LN_EPS) * g + b


def _masked_softmax(s, valid):
    s = jnp.where(valid, s, -jnp.inf)
    m = jnp.max(s, axis=-1, keepdims=True)
    m = jnp.where(m == -jnp.inf, 0.0, m)
    e = jnp.exp(s - m)
    d = jnp.maximum(jnp.sum(e, axis=-1, keepdims=True), 1e-30)
    return e * (1.0 / d)


def _log2(n):
    assert n & (n - 1) == 0, "power of two expected"
    return n.bit_length() - 1


def _div_pow2(x, n):
    return lax.shift_right_logical(x, jnp.int32(_log2(n)))


def _mod_pow2(x, n):
    assert n & (n - 1) == 0, "power of two expected"
    return lax.bitwise_and(x, jnp.int32(n - 1))


def _rank_desc(imp, n):
    j_io = lax.broadcasted_iota(jnp.int32, (1, imp.shape[1]), 1)
    rank = jnp.zeros(imp.shape, F32)
    for jp in range(n):
        col = imp[:, jp:jp + 1]
        before = jnp.where(col == imp, jnp.where(jp < j_io, 1.0, 0.0), jnp.where(col > imp, 1.0, 0.0))
        rank = rank + before
    return rank


def _ada_kernel(c_ref, w_ref, b_ref, o_ref):
    a = jax.nn.silu(c_ref[...]).astype(BF)
    o_ref[...] = _dot(a, w_ref[...].astype(BF)) + b_ref[...]


def _ada_modulation(c_all, w_ada, b_ada):
    n_sub = w_ada.shape[0] * w_ada.shape[1]
    d, n3 = w_ada.shape[2], w_ada.shape[3]
    m = c_all.shape[0]
    tn = 512
    w = w_ada.reshape(n_sub, d, n3)
    b = b_ada.reshape(n_sub, 1, n3)
    return pl.pallas_call(
        _ada_kernel,
        out_shape=jax.ShapeDtypeStruct((n_sub, m, n3), F32),
        grid=(n_sub, n3 // tn),
        in_specs=[pl.BlockSpec((m, d), lambda s, j: (0, 0)),
                  pl.BlockSpec((None, d, tn), lambda s, j: (s, 0, j)),
                  pl.BlockSpec((None, 1, tn), lambda s, j: (s, 0, j))],
        out_specs=pl.BlockSpec((None, m, tn), lambda s, j: (s, 0, j)),
        compiler_params=_cparams(2, 32),
        name="ada_modulation",
    )(c_all, w, b)


def _mod_spec(mod, tiles_per_group):
    rb, d = mod.shape[1], mod.shape[2]
    return pl.BlockSpec((None, rb, d), lambda i, j: (i // tiles_per_group, 0, 0))


def _gmlp_in_kernel(x_ref, sc_ref, sh_ref, w_ref, vg_ref, vb_ref, ws_ref, bs_ref, *rest,
                    n_j, tn, chunked):
    if chunked:
        o_ref, h_scr, z_scr = rest
    else:
        o_ref, v_ref, h_scr, z_scr = rest
    j = pl.program_id(1)

    @pl.when(j == 0)
    def _():
        h_scr[...] = (x_ref[...] * (1.0 + sc_ref[...]) + sh_ref[...]).astype(BF)

    z_scr[j] = jax.nn.gelu(_dot(h_scr[...], w_ref[...]))

    @pl.when(j == n_j - 1)
    def _():
        tm = z_scr.shape[1]
        half = n_j // 2
        e_a = half * tn
        s1 = jnp.zeros((tm, 1), F32)
        for jj in range(half):
            s1 = s1 + jnp.sum(z_scr[half + jj], axis=-1, keepdims=True)
        mu = s1 / e_a
        s2 = jnp.zeros((tm, 1), F32)
        for jj in range(half):
            vc = z_scr[half + jj] - mu
            s2 = s2 + jnp.sum(vc * vc, axis=-1, keepdims=True)
        rstd = lax.rsqrt(s2 / e_a + LN_EPS)
        gpt = tn // CHUNK
        if chunked:
            row = lax.broadcasted_iota(jnp.int32, (CHUNK, CHUNK), 0)
            col = lax.broadcasted_iota(jnp.int32, (CHUNK, CHUNK), 1)
            for jj in range(half):
                for gi in range(gpt):
                    g = jj * gpt + gi
                    cs = slice(gi * CHUNK, (gi + 1) * CHUNK)
                    wm = jnp.where(row >= col, ws_ref[g], 0.0).astype(BF)
                    bcol = bs_ref[:, g:g + 1]
                    vg = vg_ref[:, g * CHUNK:(g + 1) * CHUNK]
                    vb = vb_ref[:, g * CHUNK:(g + 1) * CHUNK]
                    for c in range(tm // CHUNK):
                        rs = slice(c * CHUNK, (c + 1) * CHUNK)
                        vn = (z_scr[half + jj, rs, cs] - mu[rs]) * rstd[rs] * vg + vb
                        mixed = _dot(wm, vn.astype(BF)) + bcol
                        o_ref[rs, g * CHUNK:(g + 1) * CHUNK] = (z_scr[jj, rs, cs] * mixed).astype(BF)
        else:
            for jj in range(half):
                cs = slice(jj * tn, (jj + 1) * tn)
                vn = (z_scr[half + jj] - mu) * rstd * vg_ref[:, cs] + vb_ref[:, cs]
                v_ref[:, cs] = vn
                mixed = ws_ref[:, cs] * vn + bs_ref[:, cs]
                o_ref[:, cs] = (z_scr[jj] * mixed).astype(BF)


def _gmlp_in(x, sc, sh, w_in, v_g, v_b, w_s, b_s, *, tm, tiles_per_group, chunked):
    r, d = x.shape
    n2 = w_in.shape[1]
    e_a = n2 // 2
    tn = 512
    n_j = n2 // tn
    n_g = w_s.shape[0]
    if chunked:
        ws_arr = w_s
        bs_arr = b_s.T
        ws_spec = pl.BlockSpec(ws_arr.shape, lambda i, j: (0, 0, 0))
        bs_spec = pl.BlockSpec(bs_arr.shape, lambda i, j: (0, 0))
        out_shape = jax.ShapeDtypeStruct((r, e_a), BF)
        out_specs = pl.BlockSpec((tm, e_a), lambda i, j: (i, 0))
    else:
        ws_arr = jnp.repeat(w_s[:, 0, 0], e_a // n_g).reshape(1, e_a)
        bs_arr = jnp.repeat(b_s[:, 0], e_a // n_g).reshape(1, e_a)
        ws_spec = pl.BlockSpec((1, e_a), lambda i, j: (0, 0))
        bs_spec = pl.BlockSpec((1, e_a), lambda i, j: (0, 0))
        out_shape = (jax.ShapeDtypeStruct((r, e_a), BF), jax.ShapeDtypeStruct((r, e_a), F32))
        out_specs = (pl.BlockSpec((tm, e_a), lambda i, j: (i, 0)), pl.BlockSpec((tm, e_a), lambda i, j: (i, 0)))
    return pl.pallas_call(
        functools.partial(_gmlp_in_kernel, n_j=n_j, tn=tn, chunked=chunked),
        out_shape=out_shape,
        grid=(r // tm, n_j),
        in_specs=[pl.BlockSpec((tm, d), lambda i, j: (i, 0)),
                  _mod_spec(sc, tiles_per_group), _mod_spec(sh, tiles_per_group),
                  pl.BlockSpec((d, tn), lambda i, j: (0, j)),
                  pl.BlockSpec((1, e_a), lambda i, j: (0, 0)),
                  pl.BlockSpec((1, e_a), lambda i, j: (0, 0)),
                  ws_spec, bs_spec],
        out_specs=out_specs,
        scratch_shapes=[pltpu.VMEM((tm, d), BF), pltpu.VMEM((n_j, tm, tn), F32)],
        compiler_params=_cparams(2, 48),
        name="gmlp_in_chunked" if chunked else "gmlp_in_single",
    )(x, sc, sh, w_in, v_g.reshape(1, e_a), v_b.reshape(1, e_a), ws_arr, bs_arr)


def _proj_res_ln_kernel(a_ref, w_ref, x_ref, gate_ref, lg_ref, lb_ref, *rest, n_k, alpha, emit_next):
    if emit_next:
        sc_ref, sh_ref, y_ref, h_ref, acc = rest
    else:
        y_ref, acc = rest
    k = pl.program_id(1)

    @pl.when(k == 0)
    def _():
        acc[...] = jnp.zeros_like(acc)

    acc[...] += _dot(a_ref[...].astype(BF), w_ref[...])

    @pl.when(k == n_k - 1)
    def _():
        t = alpha * x_ref[...] + (1.0 + gate_ref[...]) * acc[...]
        y = _layer_norm(t, lg_ref[...], lb_ref[...])
        y_ref[...] = y
        if emit_next:
            h_ref[...] = (y * (1.0 + sc_ref[...]) + sh_ref[...]).astype(BF)


def _proj_res_ln(a, w, x, gate, ln_g, ln_b, nxt, *, alpha, tm, tk, tiles_per_group):
    r, kdim = a.shape
    d = w.shape[1]
    n_k = kdim // tk
    emit_next = nxt is not None
    in_specs = [pl.BlockSpec((tm, tk), lambda i, k: (i, k)),
                pl.BlockSpec((tk, d), lambda i, k: (k, 0)),
                pl.BlockSpec((tm, d), lambda i, k: (i, 0)),
                _mod_spec(gate, tiles_per_group),
                pl.BlockSpec((1, d), lambda i, k: (0, 0)),
                pl.BlockSpec((1, d), lambda i, k: (0, 0))]
    args = [a, w, x, gate, ln_g.reshape(1, d), ln_b.reshape(1, d)]
    out_shape = [jax.ShapeDtypeStruct((r, d), F32)]
    out_specs = [pl.BlockSpec((tm, d), lambda i, k: (i, 0))]
    if emit_next:
        in_specs += [_mod_spec(nxt[0], tiles_per_group), _mod_spec(nxt[1], tiles_per_group)]
        args += [nxt[0], nxt[1]]
        out_shape.append(jax.ShapeDtypeStruct((r, d), BF))
        out_specs.append(pl.BlockSpec((tm, d), lambda i, k: (i, 0)))
    res = pl.pallas_call(
        functools.partial(_proj_res_ln_kernel, n_k=n_k, alpha=alpha, emit_next=emit_next),
        out_shape=tuple(out_shape),
        grid=(r // tm, n_k),
        in_specs=in_specs,
        out_specs=tuple(out_specs),
        scratch_shapes=[pltpu.VMEM((tm, d), F32)],
        compiler_params=_cparams(2, 48),
        name="proj_res_ln",
    )(*args)
    return res if emit_next else (res[0], None)


def _ffn_in_kernel(h_ref, wg_ref, wu_ref, o_ref):
    h = h_ref[...]
    o_ref[...] = (jax.nn.silu(_dot(h, wg_ref[...])) * _dot(h, wu_ref[...])).astype(BF)


def _ffn_in(h, w_in, *, tm):
    r, d = h.shape
    d_ff = w_in.shape[1] // 2
    tn = 512
    n_j = d_ff // tn
    return pl.pallas_call(
        _ffn_in_kernel,
        out_shape=jax.ShapeDtypeStruct((r, d_ff), BF),
        grid=(r // tm, n_j),
        in_specs=[pl.BlockSpec((tm, d), lambda i, j: (i, 0)),
                  pl.BlockSpec((d, tn), lambda i, j: (0, j)),
                  pl.BlockSpec((d, tn), lambda i, j: (0, j + n_j))],
        out_specs=pl.BlockSpec((tm, tn), lambda i, j: (i, j)),
        compiler_params=_cparams(2, 40),
        name="ffn_in",
    )(h, w_in, w_in)


KV_SLABS = N_KV_HEADS * 2


def _kv_proj_kernel(y_ref, w_ref, f0_ref, f1_ref, f2_ref, b1_ref, b2_ref, xb):
    j = pl.program_id(1)

    @pl.when(j == 0)
    def _():
        xb[...] = y_ref[...].astype(BF)

    r = _dot(xb[...], w_ref[...])
    tm = r.shape[0]
    for idx, f_ref in enumerate((f0_ref, f1_ref, f2_ref)):
        @pl.when(j == idx)
        def _(f_ref=f_ref):
            for hk in range(KV_SLABS):
                f_ref[pl.ds(hk, tm, stride=KV_SLABS), :] = r[:, hk * HEAD_DIM:(hk + 1) * HEAD_DIM]

    for idx, b_ref in ((1, b1_ref), (2, b2_ref)):
        @pl.when(j == idx)
        def _(b_ref=b_ref):
            b_ref[...] = r.astype(BF)


def _kv_proj(y, w_kv, *, tm):
    r, d = y.shape
    n = w_kv.shape[1] // 3
    flat = jax.ShapeDtypeStruct((r * KV_SLABS, HEAD_DIM), F32)
    fspec = pl.BlockSpec((tm * KV_SLABS, HEAD_DIM), lambda i, j: (i, 0))
    wide = jax.ShapeDtypeStruct((r, n), BF)
    wspec = pl.BlockSpec((tm, n), lambda i, j: (i, 0))
    return pl.pallas_call(
        _kv_proj_kernel,
        out_shape=(flat, flat, flat, wide, wide),
        grid=(r // tm, 3),
        in_specs=[pl.BlockSpec((tm, d), lambda i, j: (i, 0)),
                  pl.BlockSpec((d, n), lambda i, j: (0, j))],
        out_specs=(fspec, fspec, fspec, wspec, wspec),
        scratch_shapes=[pltpu.VMEM((tm, d), BF)],
        compiler_params=_cparams(2, 48),
        name="kv_proj",
    )(y, w_kv)


def _qg_proj_kernel(h_ref, wq_ref, wg_ref, q_ref, g_ref):
    h = h_ref[...]
    q_ref[...] = _dot(h, wq_ref[...]).astype(BF)

    @pl.when(pl.program_id(1) == 0)
    def _():
        g_ref[...] = jax.nn.sigmoid(_dot(h, wg_ref[...]))


def _qg_proj(h, w_q, w_g, *, tm):
    r, d = h.shape
    qw = w_q.shape[1]
    gw = w_g.shape[1]
    tn = 512
    return pl.pallas_call(
        _qg_proj_kernel,
        out_shape=(jax.ShapeDtypeStruct((r, qw), BF), jax.ShapeDtypeStruct((r, gw), F32)),
        grid=(r // tm, qw // tn),
        in_specs=[pl.BlockSpec((tm, d), lambda i, j: (i, 0)),
                  pl.BlockSpec((d, tn), lambda i, j: (0, j)),
                  pl.BlockSpec((d, gw), lambda i, j: (0, 0))],
        out_specs=(pl.BlockSpec((tm, tn), lambda i, j: (i, j)),
                   pl.BlockSpec((tm, gw), lambda i, j: (i, 0))),
        compiler_params=_cparams(2, 40),
        name="qg_proj",
    )(h, w_q, w_g)


PAGES_PER_STEP = 16
CHUNKS_PER_PAGE = PAGE_SIZE // CMP_STRIDE


PAGE_FLAT_ROWS = PAGE_SIZE * KV_SLABS
CHUNK_FLAT_ROWS = CMP_STRIDE * KV_SLABS


def _compress_kernel(pt_ref, *refs, n_steps):
    page_refs = refs[:PAGES_PER_STEP]
    w1_ref, pe_ref, b1_ref, w2_ref, b2_ref, o_ref, p0_scr, p1_scr = refs[PAGES_PER_STEP:]
    t = pl.program_id(1)
    rows_step = PAGES_PER_STEP * CHUNKS_PER_PAGE * KV_SLABS
    n_rows = n_steps * rows_step
    hidden = w2_ref.shape[0]

    @pl.when(t == 0)
    def _():
        p1_scr[pl.ds(n_rows, KV_SLABS), :] = jnp.zeros((KV_SLABS, hidden), F32)

    per_s = []
    for s in range(CMP_STRIDE):
        rows = [page_refs[p][pl.ds(c * CHUNK_FLAT_ROWS + s * KV_SLABS, KV_SLABS), :]
                for p in range(PAGES_PER_STEP) for c in range(CHUNKS_PER_PAGE)]
        per_s.append(jnp.concatenate(rows, axis=0))
    x = jnp.concatenate(per_s, axis=1).astype(BF)
    pr = _dot(x, w1_ref[...])
    is_v = _mod_pow2(lax.broadcasted_iota(jnp.int32, (rows_step, 1), 0), 2) == 1
    pk = jnp.where(is_v, pr[:, 2 * hidden:], pr[:, :2 * hidden])
    row0 = pl.multiple_of(t * rows_step, rows_step)
    p0_scr[pl.ds(row0, rows_step), :] = pk[:, :hidden]
    p1_scr[pl.ds(row0, rows_step), :] = pk[:, hidden:]

    @pl.when(t == n_steps - 1)
    def _():
        cst = _dot(pe_ref[...].astype(BF), w1_ref[...])
        const_k = cst[0:1, :hidden] + cst[1:2, hidden:2 * hidden] + b1_ref[0:1, :]
        const_v = cst[2:3, 2 * hidden:3 * hidden] + cst[3:4, 3 * hidden:] + b1_ref[1:2, :]
        v_rows = _mod_pow2(lax.broadcasted_iota(jnp.int32, (n_rows, 1), 0), 2) == 1
        pre = p0_scr[0:n_rows, :] + p1_scr[KV_SLABS:n_rows + KV_SLABS, :] + jnp.where(v_rows, const_v, const_k)
        o2 = _dot(jax.nn.gelu(pre).astype(BF), w2_ref[...])
        o_ref[...] = jnp.where(v_rows, o2[:, HEAD_DIM:] + b2_ref[1:2, :], o2[:, :HEAD_DIM] + b2_ref[0:1, :])


def _compress(pages_flat, page_table, w1all, pe16, b1, w2all, b2):
    nb, n_pages = page_table.shape
    n_steps = n_pages // PAGES_PER_STEP
    n_rows = n_pages * CHUNKS_PER_PAGE * KV_SLABS
    hidden = w2all.shape[0]

    def page_spec(p):
        return pl.BlockSpec((PAGE_FLAT_ROWS, HEAD_DIM), lambda b, t, pt: (pt[b, t * PAGES_PER_STEP + p], 0))

    full = lambda a: pl.BlockSpec(a.shape, lambda b, t, pt: (0,) * a.ndim)
    grid_spec = pltpu.PrefetchScalarGridSpec(
        num_scalar_prefetch=1,
        grid=(nb, n_steps),
        in_specs=[page_spec(p) for p in range(PAGES_PER_STEP)] + [
            full(w1all), full(pe16), full(b1), full(w2all), full(b2)],
        out_specs=pl.BlockSpec((None, n_rows, HEAD_DIM), lambda b, t, pt: (b, 0, 0)),
        scratch_shapes=[pltpu.VMEM((n_rows, hidden), F32), pltpu.VMEM((n_rows + KV_SLABS, hidden), F32)],
    )
    return pl.pallas_call(
        functools.partial(_compress_kernel, n_steps=n_steps),
        out_shape=jax.ShapeDtypeStruct((nb, n_rows, HEAD_DIM), F32),
        grid_spec=grid_spec,
        compiler_params=_cparams(2, 48),
        name="compress_blocks",
    )(page_table, *([pages_flat] * PAGES_PER_STEP), w1all, pe16, b1, w2all, b2)


def _overlap_map(n_cmp_rows, n_slc, n_cols):
    cmp_lo = np.arange(n_cmp_rows)[:, None] * CMP_STRIDE
    slc_lo = np.arange(n_cols)[None, :] * SEL_BLOCK
    ov = np.minimum(cmp_lo + CMP_LEN, slc_lo + SEL_BLOCK) - np.maximum(cmp_lo, slc_lo)
    w = np.clip(ov, 0, None).astype(np.float32) / CMP_LEN
    w[:, n_slc:] = 0.0
    return w


def _attn_prompt_kernel(q_ref, gt_ref, cmp_ref, sel_ref, win_ref, wmap_ref, exp_ref, o_ref, osel_scr, *,
                        scale, n_slc, n_qb, n_widths):
    g = pl.program_id(1)
    qb = pl.program_id(2)
    q4 = q_ref[...]
    nq = q4.shape[0]
    rq = q4.shape[1] // HEAD_DIM
    qs = jnp.concatenate([q4[:, r * HEAD_DIM:(r + 1) * HEAD_DIM] for r in range(rq)], axis=0)
    qpos = qb * nq + lax.broadcasted_iota(jnp.int32, (nq, 1), 0)

    def attend(s_raw, valid, v):
        es, inv = [], []
        for r in range(rq):
            sr = jnp.where(valid, s_raw[r * nq:(r + 1) * nq], -jnp.inf)
            m = jnp.max(sr, axis=-1, keepdims=True)
            m = jnp.where(m == -jnp.inf, 0.0, m)
            e = jnp.exp2((sr - m) * (scale * LOG2_E))
            inv.append(1.0 / jnp.maximum(jnp.sum(e, axis=-1, keepdims=True), 1e-30))
            es.append(e.astype(BF))
        return _dot(jnp.concatenate(es, axis=0), v) * jnp.concatenate(inv, axis=0)

    n_cmp_rows = cmp_ref.shape[0] // KV_SLABS
    kc = cmp_ref[pl.ds(g * 2, n_cmp_rows, stride=KV_SLABS), :].astype(BF)
    vc = cmp_ref[pl.ds(g * 2 + 1, n_cmp_rows, stride=KV_SLABS), :].astype(BF)
    n_io = lax.broadcasted_iota(jnp.int32, (1, n_cmp_rows), 1)
    valid_c = (n_io * CMP_STRIDE + (CMP_LEN - 1)) <= qpos
    s_c = _dot_t(qs, kc) * scale
    ps = [_masked_softmax(s_c[r * nq:(r + 1) * nq], valid_c).astype(BF) for r in range(rq)]
    o_cmp = _dot(jnp.concatenate(ps, axis=0), vc)

    imp = _dot(jnp.concatenate(ps, axis=1), wmap_ref[...])
    j_io = lax.broadcasted_iota(jnp.int32, (1, imp.shape[1]), 1)
    q_blk = _div_pow2(qpos, SEL_BLOCK)
    causal = j_io <= q_blk
    forced = (j_io == 0) | (causal & (j_io > q_blk - N_LOCAL_SEL))
    imp = jnp.where(forced, imp + FORCE_BONUS, imp)
    imp = jnp.where(causal, imp, -jnp.inf)
    rank = _rank_desc(imp, n_slc)
    chosen = jnp.where(rank < min(N_SEL, n_slc), jnp.where(imp > -jnp.inf, 1.0, 0.0), 0.0).astype(BF)

    t_all = sel_ref.shape[0]
    qb_per_width = n_qb // n_widths
    for wi in range(n_widths):
        width = (wi + 1) * (t_all // n_widths)

        @pl.when((qb >= wi * qb_per_width) & (qb < (wi + 1) * qb_per_width))
        def _(width=width):
            in_sel = _dot(chosen, exp_ref[:, :width])
            t_io = lax.broadcasted_iota(jnp.int32, (1, width), 1)
            valid_s = (in_sel > 0.5) & (t_io <= qpos)
            osel_scr[...] = attend(_dot_t(qs, sel_ref[:width, :HEAD_DIM]), valid_s, sel_ref[:width, HEAD_DIM:])

    span = WINDOW + nq
    k0 = pl.multiple_of(jnp.maximum(qb * nq - WINDOW, 0), nq)
    kp = k0 + lax.broadcasted_iota(jnp.int32, (1, span), 1)
    valid_w = (kp <= qpos) & (kp > qpos - WINDOW)
    o_win = attend(_dot_t(qs, win_ref[pl.ds(k0, span), :HEAD_DIM]), valid_w, win_ref[pl.ds(k0, span), HEAD_DIM:])

    gt = gt_ref[...]
    o_sel = osel_scr[...]
    outs = []
    for r in range(rq):
        rs = slice(r * nq, (r + 1) * nq)
        outs.append(gt[:, 3 * r:3 * r + 1] * o_cmp[rs] + gt[:, 3 * r + 1:3 * r + 2] * o_sel[rs]
                    + gt[:, 3 * r + 2:3 * r + 3] * o_win[rs])
    o_ref[...] = jnp.concatenate(outs, axis=1).astype(BF)


def _attn_prompt(q, gates, cmp_flat, kv_sel, kv_win, *, scale):
    nb, t, qw = q.shape
    n_g = N_KV_HEADS
    gw = qw // n_g
    kvw = 2 * HEAD_DIM
    cmp_rows = cmp_flat.shape[1]
    n_slc = t // SEL_BLOCK
    rq = gw // HEAD_DIM
    n_qb = t // Q_BLOCK
    n_widths = 4 if n_qb % 4 == 0 else 1
    wmap = jnp.asarray(np.tile(_overlap_map(cmp_rows // KV_SLABS, n_slc, LANES), (rq, 1)), BF)
    expand = jnp.asarray((np.arange(t)[None, :] // SEL_BLOCK == np.arange(LANES)[:, None]).astype(np.float32), BF)
    return pl.pallas_call(
        functools.partial(_attn_prompt_kernel, scale=scale, n_slc=n_slc, n_qb=n_qb, n_widths=n_widths),
        out_shape=jax.ShapeDtypeStruct((nb, t, qw), BF),
        grid=(nb, n_g, n_qb),
        in_specs=[pl.BlockSpec((None, Q_BLOCK, gw), lambda b, g, i: (b, i, g)),
                  pl.BlockSpec((None, Q_BLOCK, LANES), lambda b, g, i: (b, i, g)),
                  pl.BlockSpec((None, cmp_rows, HEAD_DIM), lambda b, g, i: (b, 0, 0)),
                  pl.BlockSpec((None, t, kvw), lambda b, g, i: (b, 0, g)),
                  pl.BlockSpec((None, t, kvw), lambda b, g, i: (b, 0, g)),
                  pl.BlockSpec(wmap.shape, lambda b, g, i: (0, 0)),
                  pl.BlockSpec(expand.shape, lambda b, g, i: (0, 0))],
        out_specs=pl.BlockSpec((None, Q_BLOCK, gw), lambda b, g, i: (b, i, g)),
        scratch_shapes=[pltpu.VMEM((rq * Q_BLOCK, HEAD_DIM), F32)],
        compiler_params=_cparams(3, 56),
        name="attn_prompt",
    )(q, gates, cmp_flat, kv_sel, kv_win, wmap, expand)


def _attn_sample_a_kernel(q_ref, cmp_ref, win_ref, wmap_ref, ocmp_ref, owin_ref, idx_ref, val_ref, *,
                          scale, q_pos, n_cmp, n_slc, win_pos0):
    q = q_ref[...]
    n_h = q.shape[0]
    rq = n_h // N_KV_HEADS
    row_g = _div_pow2(lax.broadcasted_iota(jnp.int32, (n_h, 1), 0), rq)
    cmp_rows = cmp_ref.shape[0] // KV_SLABS
    win_rows = win_ref.shape[0] // KV_SLABS
    n_io = lax.broadcasted_iota(jnp.int32, (1, cmp_rows), 1)
    valid_c = (n_io < n_cmp) & ((n_io * CMP_STRIDE + (CMP_LEN - 1)) <= q_pos)
    kp = win_pos0 + lax.broadcasted_iota(jnp.int32, (1, win_rows), 1)
    valid_w = (kp <= q_pos) & (kp > q_pos - WINDOW)
    o_cmp = jnp.zeros((n_h, HEAD_DIM), F32)
    o_win = jnp.zeros((n_h, HEAD_DIM), F32)
    imp = jnp.zeros((n_h, wmap_ref.shape[1]), F32)
    for g in range(N_KV_HEADS):
        in_g = row_g == g
        kc = cmp_ref[pl.ds(2 * g, cmp_rows, stride=KV_SLABS), :].astype(BF)
        vc = cmp_ref[pl.ds(2 * g + 1, cmp_rows, stride=KV_SLABS), :].astype(BF)
        p = _masked_softmax(_dot_t(q, kc) * scale, valid_c).astype(BF)
        o_cmp = jnp.where(in_g, _dot(p, vc), o_cmp)
        imp_rows = _dot(p, wmap_ref[...])
        imp_g = jnp.sum(jnp.where(in_g, imp_rows, 0.0), axis=0, keepdims=True)
        imp = jnp.where(in_g, imp_g, imp)
        kw = win_ref[pl.ds(2 * g, win_rows, stride=KV_SLABS), :].astype(BF)
        vw = win_ref[pl.ds(2 * g + 1, win_rows, stride=KV_SLABS), :].astype(BF)
        pw = _masked_softmax(_dot_t(q, kw) * scale, valid_w).astype(BF)
        o_win = jnp.where(in_g, _dot(pw, vw), o_win)
    ocmp_ref[...] = o_cmp
    owin_ref[...] = o_win

    j_io = lax.broadcasted_iota(jnp.int32, (1, imp.shape[1]), 1)
    q_blk = q_pos // SEL_BLOCK
    causal = j_io <= q_blk
    forced = (j_io == 0) | (causal & (j_io > q_blk - N_LOCAL_SEL))
    imp = jnp.where(forced, imp + FORCE_BONUS, imp)
    imp = jnp.where(causal, imp, -jnp.inf)
    rank = _rank_desc(imp, n_slc)
    jf = j_io.astype(F32)
    k_io = lax.broadcasted_iota(jnp.int32, (1, idx_ref.shape[1]), 1)
    idx = jnp.zeros(idx_ref.shape, F32)
    val = jnp.zeros(idx_ref.shape, F32)
    for k in range(min(N_SEL, n_slc)):
        hit = rank == k
        ik = jnp.sum(jnp.where(hit, jf, 0.0), axis=1, keepdims=True)
        vk = jnp.sum(jnp.where(hit, jnp.where(imp > -jnp.inf, 1.0, 0.0), 0.0), axis=1, keepdims=True)
        idx = jnp.where(k_io == k, ik, idx)
        val = jnp.where(k_io == k, vk, val)
    idx_ref[...] = idx.astype(jnp.int32)
    val_ref[...] = val.astype(jnp.int32)


def _attn_sample_a(q, cmp_blocks, win_rows, *, scale, q_pos, n_cmp, n_slc, win_pos0):
    nb, n_h, d = q.shape
    n_cols = -(-n_slc // LANES) * LANES
    wmap = jnp.asarray(_overlap_map(cmp_blocks.shape[1] // KV_SLABS, n_slc, n_cols), BF)
    o_sds = jax.ShapeDtypeStruct((nb, n_h, d), F32)
    i_sds = jax.ShapeDtypeStruct((nb, n_h, LANES), jnp.int32)
    blk = lambda a: pl.BlockSpec((None,) + a.shape[1:], lambda b: (b, 0, 0))
    ospec = pl.BlockSpec((None, n_h, d), lambda b: (b, 0, 0))
    ispec = pl.BlockSpec((None, n_h, LANES), lambda b: (b, 0, 0))
    return pl.pallas_call(
        functools.partial(_attn_sample_a_kernel, scale=scale, q_pos=q_pos, n_cmp=n_cmp, n_slc=n_slc,
                          win_pos0=win_pos0),
        out_shape=(o_sds, o_sds, i_sds, i_sds),
        grid=(nb,),
        in_specs=[blk(q), blk(cmp_blocks), blk(win_rows), pl.BlockSpec(wmap.shape, lambda b: (0, 0))],
        out_specs=(ospec, ospec, ispec, ispec),
        compiler_params=_cparams(1, 32),
        name="attn_sample_a",
    )(q, cmp_blocks, win_rows, wmap)


def _attn_sample_b_kernel(idx_ref, val_ref, pt_ref, q_ref, new_ref, gt_ref, ocmp_ref, owin_ref, *rest,
                          scale, q_pos, past_len):
    blk_refs = rest[:N_SEL]
    o_ref = rest[N_SEL]
    b = pl.program_id(0)
    g = pl.program_id(1)
    q = q_ref[...]
    n_h = q.shape[0]
    rq = n_h // N_KV_HEADS
    base = (b * N_KV_HEADS + g) * N_SEL
    n_keys = N_SEL * SEL_BLOCK
    lane = lax.broadcasted_iota(jnp.int32, (1, n_keys), 1)
    k_new = new_ref[pl.ds(2 * g, 1), :]
    v_new = new_ref[pl.ds(2 * g + 1, 1), :]
    ks, vs = [], []
    kpos = _mod_pow2(lane, SEL_BLOCK)
    kval = jnp.zeros((1, n_keys), jnp.int32)
    lane_blk = _div_pow2(lane, SEL_BLOCK)
    for k in range(N_SEL):
        blk = idx_ref[base + k]
        is_new = blk * SEL_BLOCK >= past_len
        ks.append(jnp.where(is_new, k_new, blk_refs[k][pl.ds(2 * g, SEL_BLOCK, stride=KV_SLABS), :]).astype(BF))
        vs.append(jnp.where(is_new, v_new, blk_refs[k][pl.ds(2 * g + 1, SEL_BLOCK, stride=KV_SLABS), :]).astype(BF))
        in_k = lane_blk == k
        kpos = kpos + jnp.where(in_k, blk * SEL_BLOCK, 0)
        kval = jnp.where(in_k, val_ref[base + k], kval)
    valid = (kval > 0) & (kpos <= q_pos)
    p = _masked_softmax(_dot_t(q, jnp.concatenate(ks, axis=0)) * scale, valid).astype(BF)
    o_sel = _dot(p, jnp.concatenate(vs, axis=0))
    gt = gt_ref[...]
    in_g = _div_pow2(lax.broadcasted_iota(jnp.int32, (n_h, 1), 0), rq) == g

    @pl.when(g == 0)
    def _():
        o_ref[...] = gt[:, 0:1] * ocmp_ref[...] + gt[:, 2:3] * owin_ref[...]

    o_ref[...] += jnp.where(in_g, gt[:, 1:2] * o_sel, 0.0)


def _attn_sample_b(sel_idx, sel_val, page_table, q, kv_sel_new, gates, o_cmp, o_win, pool_sel, *,
                   scale, q_pos, past_len):
    nb, n_h, d = q.shape
    halves = PAGE_SIZE // SEL_BLOCK
    blk_rows = SEL_BLOCK * KV_SLABS
    pool = pool_sel.reshape(pool_sel.shape[0] // blk_rows, blk_rows, d)

    def blk_spec(k):
        def index_map(b, g, idx, val, pt):
            row0 = jnp.minimum(idx[(b * N_KV_HEADS + g) * N_SEL + k] * SEL_BLOCK, past_len - 1)
            return (pt[b, row0 // PAGE_SIZE] * halves + (row0 % PAGE_SIZE) // SEL_BLOCK, 0, 0)
        return pl.BlockSpec((None, blk_rows, d), index_map)

    per_b = lambda a: pl.BlockSpec((None,) + a.shape[1:], lambda b, g, idx, val, pt: (b, 0, 0))
    new_rows = kv_sel_new.reshape(nb, KV_SLABS, d)
    grid_spec = pltpu.PrefetchScalarGridSpec(
        num_scalar_prefetch=3,
        grid=(nb, N_KV_HEADS),
        in_specs=[per_b(q), per_b(new_rows), per_b(gates), per_b(o_cmp), per_b(o_win)]
        + [blk_spec(k) for k in range(N_SEL)],
        out_specs=pl.BlockSpec((None, n_h, d), lambda b, g, idx, val, pt: (b, 0, 0)),
    )
    return pl.pallas_call(
        functools.partial(_attn_sample_b_kernel, scale=scale, q_pos=q_pos, past_len=past_len),
        out_shape=jax.ShapeDtypeStruct((nb, n_h, d), F32),
        grid_spec=grid_spec,
        compiler_params=_cparams(2, 32),
        name="attn_sample_b",
    )(sel_idx, sel_val, page_table, q, new_rows, gates, o_cmp, o_win, *([pool] * N_SEL))


def _largest_tile(n, cap):
    t = cap
    while n % t:
        t -= LANES
    return t


def kernel(x_prompt, x_sample, cache_cmp_kv, cache_sel_kv, cache_win_kv, page_table, c_prompt, c_sample, w_ada, b_ada, ln_g, ln_b, a_w_in, a_v_g, a_v_b, a_w_s, a_b_s, a_w_out, w_kv, cmp_pe, cmp_w1, cmp_b1, cmp_w2, cmp_b2, b_w_qg, b_w_o, ffn_w_in, ffn_w_out):
    nb, t, d = x_prompt.shape
    ns, tq, _ = x_sample.shape
    assert tq == 1, "the sample path handles one new token per sequence"
    depth = w_ada.shape[0]
    n_a = a_w_in.shape[0]
    alpha = float((2 * depth) ** 0.25)
    scale = float(HEAD_DIM ** -0.5)
    n_pool = cache_cmp_kv.shape[0]
    past_len = page_table.shape[1] * PAGE_SIZE
    kvw = N_KV_HEADS * 2 * HEAD_DIM
    q_width = b_w_o.shape[1]
    n_heads = q_width // HEAD_DIM
    rq = n_heads // N_KV_HEADS

    mod = _ada_modulation(jnp.concatenate([c_prompt, c_sample], axis=0), w_ada, b_ada)

    def mods(layer, sub):
        m = mod[layer * 2 + sub]
        out = []
        for part in range(3):
            sl = m[:, part * d:(part + 1) * d]
            out.append((sl[:nb].reshape(nb, 1, d), sl[nb:].reshape(1, ns, d)))
        return out

    wb = lambda w: w.astype(BF)
    hidden = cmp_w1.shape[-1]
    halves = CMP_LEN // CMP_STRIDE
    assert halves == 2, "a compression block spans two stride-chunks"
    w1all = wb(cmp_w1.reshape(2, halves, CMP_STRIDE, HEAD_DIM, hidden)
               .transpose(2, 3, 0, 1, 4).reshape(CMP_STRIDE * HEAD_DIM, 2 * halves * hidden))
    pe16 = jnp.pad(cmp_pe.reshape(2 * halves, CMP_STRIDE * HEAD_DIM), ((0, 16 - 2 * halves), (0, 0)))
    w2all = wb(cmp_w2.transpose(1, 0, 2).reshape(hidden, 2 * HEAD_DIM))

    tm_p = 512
    tpg_p = t // tm_p
    prompt = dict(tm=tm_p, tiles_per_group=tpg_p)
    sample = dict(tm=ns, tiles_per_group=1)

    xp = x_prompt.reshape(nb * t, d)
    xs = x_sample.reshape(ns, d)
    hp = hs = None
    chunk_v = []
    outs = {}
    for layer in range(depth):
        (sh_p, sh_s), (sc_p, sc_s), (gt_p, gt_s) = mods(layer, 0)
        (fsh_p, fsh_s), (fsc_p, fsc_s), (fgt_p, fgt_s) = mods(layer, 1)
        if layer < n_a:
            w_in = wb(a_w_in[layer])
            w_out = wb(a_w_out[layer])
            tk = _largest_tile(w_out.shape[0], 1024)
            ap = _gmlp_in(xp, sc_p, sh_p, w_in, a_v_g[layer], a_v_b[layer], a_w_s[layer], a_b_s[layer],
                          chunked=True, **prompt)
            as_, v_rows = _gmlp_in(xs, sc_s, sh_s, w_in, a_v_g[layer], a_v_b[layer], a_w_s[layer], a_b_s[layer],
                                   chunked=False, **sample)
            chunk_v.append(v_rows.reshape(ns, tq, -1))
            xp, hp = _proj_res_ln(ap, w_out, xp, gt_p, ln_g[layer, 0], ln_b[layer, 0], (fsc_p, fsh_p),
                                  alpha=alpha, tk=tk, **prompt)
            xs, hs = _proj_res_ln(as_, w_out, xs, gt_s, ln_g[layer, 0], ln_b[layer, 0], (fsc_s, fsh_s),
                                  alpha=alpha, tk=tk, **sample)
        else:
            if layer == n_a:
                w_kvb = wb(w_kv)
                cmp_p, sel_p, win_p, sel_pb, win_pb = _kv_proj(xp, w_kvb, tm=tm_p)
                cmp_s, sel_s, win_s, _, _ = _kv_proj(xs, w_kvb, tm=ns)
                pages_per_seq_p = t // PAGE_SIZE
                ident = jnp.arange(nb * pages_per_seq_p, dtype=jnp.int32).reshape(nb, pages_per_seq_p)
                cmp_blocks_p = _compress(cmp_p, ident, w1all, pe16, cmp_b1, w2all, cmp_b2)
                assert (past_len + tq) // CMP_STRIDE == past_len // CMP_STRIDE
                cmp_blocks_s = _compress(cache_cmp_kv.reshape(n_pool * PAGE_FLAT_ROWS, HEAD_DIM), page_table,
                                         w1all, pe16, cmp_b1, w2all, cmp_b2)
                n_win = cache_win_kv.shape[1]
                win_keys_s = jnp.concatenate([cache_win_kv.reshape(ns, n_win * KV_SLABS, HEAD_DIM),
                                              win_s.reshape(ns, tq * KV_SLABS, HEAD_DIM)],
                                             axis=1)[:, -n_win * KV_SLABS:]
                outs["kv_p"] = (cmp_p, sel_p, win_p)
                outs["kv_pb"] = (sel_pb.reshape(nb, t, kvw), win_pb.reshape(nb, t, kvw))
                outs["kv_s"] = (cmp_s, sel_s, win_keys_s)
            j = layer - n_a
            w_qg = b_w_qg[j]
            w_q = wb(w_qg[:, :q_width])
            w_gate = w_qg[:, q_width:].reshape(d, N_KV_HEADS, rq * 3)
            w_gate = wb(jnp.pad(w_gate, ((0, 0), (0, 0), (0, LANES - rq * 3))).reshape(d, N_KV_HEADS * LANES))
            w_o = wb(b_w_o[j])
            tk = _largest_tile(w_o.shape[0], 1024)
            assert layer > 0, "an attention layer must follow another layer"
            q_p, g_p = _qg_proj(hp, w_q, w_gate, tm=tm_p)
            q_s, g_s = _qg_proj(hs, w_q, w_gate, tm=ns)
            o_p = _attn_prompt(q_p.reshape(nb, t, q_width), g_p.reshape(nb, t, N_KV_HEADS * LANES),
                               cmp_blocks_p, outs["kv_pb"][0], outs["kv_pb"][1], scale=scale)
            n_cmp_s = (past_len + tq) // CMP_STRIDE - CMP_LEN // CMP_STRIDE + 1
            n_slc_s = -(-(past_len + tq) // SEL_BLOCK)
            q_s3 = q_s.reshape(ns, n_heads, HEAD_DIM)
            o_cmp_s, o_win_s, idx_s, val_s = _attn_sample_a(
                q_s3, cmp_blocks_s, outs["kv_s"][2], scale=scale, q_pos=past_len, n_cmp=n_cmp_s, n_slc=n_slc_s,
                win_pos0=past_len + tq - outs["kv_s"][2].shape[1] // KV_SLABS)
            sel_idx = idx_s[:, ::rq, :N_SEL].reshape(-1)
            sel_val = val_s[:, ::rq, :N_SEL].reshape(-1)
            gates_s = g_s.reshape(ns, N_KV_HEADS, LANES)[:, :, :rq * 3].reshape(ns, n_heads, 3)
            o_s = _attn_sample_b(sel_idx, sel_val, page_table, q_s3, outs["kv_s"][1], gates_s, o_cmp_s, o_win_s,
                                 cache_sel_kv.reshape(n_pool * PAGE_FLAT_ROWS, HEAD_DIM), scale=scale,
                                 q_pos=past_len, past_len=past_len)
            xp, hp = _proj_res_ln(o_p.reshape(nb * t, q_width), w_o, xp, gt_p, ln_g[layer, 0], ln_b[layer, 0],
                                  (fsc_p, fsh_p), alpha=alpha, tk=tk, **prompt)
            xs, hs = _proj_res_ln(o_s.reshape(ns, q_width), w_o, xs, gt_s, ln_g[layer, 0], ln_b[layer, 0],
                                  (fsc_s, fsh_s), alpha=alpha, tk=tk, **sample)
        f_in = wb(ffn_w_in[layer])
        f_out = wb(ffn_w_out[layer])
        tk = _largest_tile(f_out.shape[0], 1408)
        fp = _ffn_in(hp, f_in, tm=2 * tm_p)
        fs = _ffn_in(hs, f_in, tm=ns)
        nxt_p = nxt_s = None
        if n_a <= layer + 1 < depth:
            (nsh_p, nsh_s), (nsc_p, nsc_s), _ = mods(layer + 1, 0)
            nxt_p, nxt_s = (nsc_p, nsh_p), (nsc_s, nsh_s)
        xp, hp = _proj_res_ln(fp, f_out, xp, fgt_p, ln_g[layer, 1], ln_b[layer, 1], nxt_p, alpha=alpha, tk=tk, **prompt)
        xs, hs = _proj_res_ln(fs, f_out, xs, fgt_s, ln_g[layer, 1], ln_b[layer, 1], nxt_s, alpha=alpha, tk=tk, **sample)

    cmp_p, sel_p, win_p = outs["kv_p"]
    cmp_s, sel_s, win_keys_s = outs["kv_s"]
    kv_shape = (N_KV_HEADS, 2, HEAD_DIM)
    n_win_p = min(WINDOW, t)
    return (xp.reshape(nb, t, d), xs.reshape(ns, tq, d),
            cmp_p.reshape((nb, t) + kv_shape), sel_p.reshape((nb, t) + kv_shape),
            win_p.reshape((nb, t) + kv_shape)[:, -n_win_p:],
            cmp_s.reshape((ns, tq) + kv_shape), sel_s.reshape((ns, tq) + kv_shape),
            win_keys_s.reshape((ns, -1) + kv_shape),
            jnp.stack(chunk_v))
```

```python
import functools

import numpy as np
import jax
import jax.numpy as jnp
from jax import lax
from jax.experimental import pallas as pl
from jax.experimental.pallas import tpu as pltpu

BF = jnp.bfloat16
F32 = jnp.float32

LN_EPS = 1e-5
CHUNK = 128
HEAD_DIM = 128
N_KV_HEADS = 4
CMP_LEN = 32
CMP_STRIDE = 16
SEL_BLOCK = 64
N_SEL = 16
N_LOCAL_SEL = 2
FORCE_BONUS = 1e4
LOG2_E = 1.4426950408889634
WINDOW = 512
Q_BLOCK = 128
PAGE_SIZE = 128

LANES = 128
MIB = 1024 * 1024


def _cparams(n_axes, vmem_mib):
    return pltpu.CompilerParams(dimension_semantics=("arbitrary",) * n_axes,
                                vmem_limit_bytes=vmem_mib * MIB)


def _dot(a, b):
    return jnp.dot(a, b, preferred_element_type=F32)


def _dot_t(a, b):
    return lax.dot_general(a, b, (((1,), (1,)), ((), ())), preferred_element_type=F32)


def _layer_norm(x, g, b):
    mu = jnp.mean(x, axis=-1, keepdims=True)
    xc = x - mu
    var = jnp.mean(xc * xc, axis=-1, keepdims=True)
    return xc * lax.rsqrt(var + LN_EPS) * g + b


def _masked_softmax(s, valid):
    s = jnp.where(valid, s, -jnp.inf)
    m = jnp.max(s, axis=-1, keepdims=True)
    m = jnp.where(m == -jnp.inf, 0.0, m)
    e = jnp.exp(s - m)
    d = jnp.maximum(jnp.sum(e, axis=-1, keepdims=True), 1e-30)
    return e * (1.0 / d)


def _log2(n):
    assert n & (n - 1) == 0, "power of two expected"
    return n.bit_length() - 1


def _div_pow2(x, n):
    return lax.shift_right_logical(x, jnp.int32(_log2(n)))


def _mod_pow2(x, n):
    assert n & (n - 1) == 0, "power of two expected"
    return lax.bitwise_and(x, jnp.int32(n - 1))


def _rank_desc(imp, n):
    j_io = lax.broadcasted_iota(jnp.int32, (1, imp.shape[1]), 1)
    rank = jnp.zeros(imp.shape, F32)
    for jp in range(n):
        col = imp[:, jp:jp + 1]
        before = jnp.where(col == imp, jnp.where(jp < j_io, 1.0, 0.0), jnp.where(col > imp, 1.0, 0.0))
        rank = rank + before
    return rank


def _ada_kernel(c_ref, w_ref, b_ref, o_ref):
    a = jax.nn.silu(c_ref[...]).astype(BF)
    o_ref[...] = _dot(a, w_ref[...].astype(BF)) + b_ref[...]


def _ada_modulation(c_all, w_ada, b_ada):
    n_sub = w_ada.shape[0] * w_ada.shape[1]
    d, n3 = w_ada.shape[2], w_ada.shape[3]
    m = c_all.shape[0]
    tn = 512
    w = w_ada.reshape(n_sub, d, n3)
    b = b_ada.reshape(n_sub, 1, n3)
    return pl.pallas_call(
        _ada_kernel,
        out_shape=jax.ShapeDtypeStruct((n_sub, m, n3), F32),
        grid=(n_sub, n3 // tn),
        in_specs=[pl.BlockSpec((m, d), lambda s, j: (0, 0)),
                  pl.BlockSpec((None, d, tn), lambda s, j: (s, 0, j)),
                  pl.BlockSpec((None, 1, tn), lambda s, j: (s, 0, j))],
        out_specs=pl.BlockSpec((None, m, tn), lambda s, j: (s, 0, j)),
        compiler_params=_cparams(2, 32),
        name="ada_modulation",
    )(c_all, w, b)


def _mod_spec(mod, tiles_per_group):
    rb, d = mod.shape[1], mod.shape[2]
    return pl.BlockSpec((None, rb, d), lambda i, j: (i // tiles_per_group, 0, 0))


def _gmlp_in_kernel(x_ref, sc_ref, sh_ref, w_ref, vg_ref, vb_ref, ws_ref, bs_ref, *rest,
                    n_j, tn, chunked):
    if chunked:
        o_ref, h_scr, z_scr = rest
    else:
        o_ref, v_ref, h_scr, z_scr = rest
    j = pl.program_id(1)

    @pl.when(j == 0)
    def _():
        h_scr[...] = (x_ref[...] * (1.0 + sc_ref[...]) + sh_ref[...]).astype(BF)

    z_scr[j] = jax.nn.gelu(_dot(h_scr[...], w_ref[...]))

    @pl.when(j == n_j - 1)
    def _():
        tm = z_scr.shape[1]
        half = n_j // 2
        e_a = half * tn
        s1 = jnp.zeros((tm, 1), F32)
        for jj in range(half):
            s1 = s1 + jnp.sum(z_scr[half + jj], axis=-1, keepdims=True)
        mu = s1 / e_a
        s2 = jnp.zeros((tm, 1), F32)
        for jj in range(half):
            vc = z_scr[half + jj] - mu
            s2 = s2 + jnp.sum(vc * vc, axis=-1, keepdims=True)
        rstd = lax.rsqrt(s2 / e_a + LN_EPS)
        gpt = tn // CHUNK
        if chunked:
            row = lax.broadcasted_iota(jnp.int32, (CHUNK, CHUNK), 0)
            col = lax.broadcasted_iota(jnp.int32, (CHUNK, CHUNK), 1)
            for jj in range(half):
                for gi in range(gpt):
                    g = jj * gpt + gi
                    cs = slice(gi * CHUNK, (gi + 1) * CHUNK)
                    wm = jnp.where(row >= col, ws_ref[g], 0.0).astype(BF)
                    bcol = bs_ref[:, g:g + 1]
                    vg = vg_ref[:, g * CHUNK:(g + 1) * CHUNK]
                    vb = vb_ref[:, g * CHUNK:(g + 1) * CHUNK]
                    for c in range(tm // CHUNK):
                        rs = slice(c * CHUNK, (c + 1) * CHUNK)
                        vn = (z_scr[half + jj, rs, cs] - mu[rs]) * rstd[rs] * vg + vb
                        mixed = _dot(wm, vn.astype(BF)) + bcol
                        o_ref[rs, g * CHUNK:(g + 1) * CHUNK] = (z_scr[jj, rs, cs] * mixed).astype(BF)
        else:
            for jj in range(half):
                cs = slice(jj * tn, (jj + 1) * tn)
                vn = (z_scr[half + jj] - mu) * rstd * vg_ref[:, cs] + vb_ref[:, cs]
                v_ref[:, cs] = vn
                mixed = ws_ref[:, cs] * vn + bs_ref[:, cs]
                o_ref[:, cs] = (z_scr[jj] * mixed).astype(BF)


def _gmlp_in(x, sc, sh, w_in, v_g, v_b, w_s, b_s, *, tm, tiles_per_group, chunked):
    r, d = x.shape
    n2 = w_in.shape[1]
    e_a = n2 // 2
    tn = 512
    n_j = n2 // tn
    n_g = w_s.shape[0]
    if chunked:
        ws_arr = w_s
        bs_arr = b_s.T
        ws_spec = pl.BlockSpec(ws_arr.shape, lambda i, j: (0, 0, 0))
        bs_spec = pl.BlockSpec(bs_arr.shape, lambda i, j: (0, 0))
        out_shape = jax.ShapeDtypeStruct((r, e_a), BF)
        out_specs = pl.BlockSpec((tm, e_a), lambda i, j: (i, 0))
    else:
        ws_arr = jnp.repeat(w_s[:, 0, 0], e_a // n_g).reshape(1, e_a)
        bs_arr = jnp.repeat(b_s[:, 0], e_a // n_g).reshape(1, e_a)
        ws_spec = pl.BlockSpec((1, e_a), lambda i, j: (0, 0))
        bs_spec = pl.BlockSpec((1, e_a), lambda i, j: (0, 0))
        out_shape = (jax.ShapeDtypeStruct((r, e_a), BF), jax.ShapeDtypeStruct((r, e_a), F32))
        out_specs = (pl.BlockSpec((tm, e_a), lambda i, j: (i, 0)), pl.BlockSpec((tm, e_a), lambda i, j: (i, 0)))
    return pl.pallas_call(
        functools.partial(_gmlp_in_kernel, n_j=n_j, tn=tn, chunked=chunked),
        out_shape=out_shape,
        grid=(r // tm, n_j),
        in_specs=[pl.BlockSpec((tm, d), lambda i, j: (i, 0)),
                  _mod_spec(sc, tiles_per_group), _mod_spec(sh, tiles_per_group),
                  pl.BlockSpec((d, tn), lambda i, j: (0, j)),
                  pl.BlockSpec((1, e_a), lambda i, j: (0, 0)),
                  pl.BlockSpec((1, e_a), lambda i, j: (0, 0)),
                  ws_spec, bs_spec],
        out_specs=out_specs,
        scratch_shapes=[pltpu.VMEM((tm, d), BF), pltpu.VMEM((n_j, tm, tn), F32)],
        compiler_params=_cparams(2, 48),
        name="gmlp_in_chunked" if chunked else "gmlp_in_single",
    )(x, sc, sh, w_in, v_g.reshape(1, e_a), v_b.reshape(1, e_a), ws_arr, bs_arr)


LN_ROWS = 16


def _proj_res_ln_kernel(a_ref, w_ref, x_ref, gate_ref, lg_ref, lb_ref, *rest, alpha, emit_next):
    if emit_next:
        sc_ref, sh_ref, y_ref, h_ref, acc = rest
    else:
        y_ref, acc = rest
    acc[...] = _dot(a_ref[...].astype(BF), w_ref[...])
    tm = acc.shape[0]
    per_row_mod = gate_ref.shape[0] > 1
    step = min(LN_ROWS, tm)
    assert tm % step == 0
    for c in range(tm // step):
        rs = slice(c * step, (c + 1) * step)
        ms = rs if per_row_mod else slice(None)
        t = alpha * x_ref[rs, :] + (1.0 + gate_ref[ms, :]) * acc[rs, :]
        y = _layer_norm(t, lg_ref[...], lb_ref[...])
        y_ref[rs, :] = y
        if emit_next:
            h_ref[rs, :] = (y * (1.0 + sc_ref[ms, :]) + sh_ref[ms, :]).astype(BF)


def _proj_res_ln(a, w, x, gate, ln_g, ln_b, nxt, *, alpha, tm, tiles_per_group):
    r, kdim = a.shape
    d = w.shape[1]
    emit_next = nxt is not None
    mod_spec = lambda m: pl.BlockSpec((None,) + m.shape[1:], lambda i: (i // tiles_per_group, 0, 0))
    in_specs = [pl.BlockSpec((tm, kdim), lambda i: (i, 0)),
                pl.BlockSpec((kdim, d), lambda i: (0, 0), pipeline_mode=pl.Buffered(1)),
                pl.BlockSpec((tm, d), lambda i: (i, 0)),
                mod_spec(gate),
                pl.BlockSpec((1, d), lambda i: (0, 0)),
                pl.BlockSpec((1, d), lambda i: (0, 0))]
    args = [a, w, x, gate, ln_g.reshape(1, d), ln_b.reshape(1, d)]
    out_shape = [jax.ShapeDtypeStruct((r, d), F32)]
    out_specs = [pl.BlockSpec((tm, d), lambda i: (i, 0))]
    if emit_next:
        in_specs += [mod_spec(nxt[0]), mod_spec(nxt[1])]
        args += [nxt[0], nxt[1]]
        out_shape.append(jax.ShapeDtypeStruct((r, d), BF))
        out_specs.append(pl.BlockSpec((tm, d), lambda i: (i, 0)))
    res = pl.pallas_call(
        functools.partial(_proj_res_ln_kernel, alpha=alpha, emit_next=emit_next),
        out_shape=tuple(out_shape),
        grid=(r // tm,),
        in_specs=in_specs,
        out_specs=tuple(out_specs),
        scratch_shapes=[pltpu.VMEM((tm, d), F32)],
        compiler_params=_cparams(1, 56),
        name="proj_res_ln",
    )(*args)
    return res if emit_next else (res[0], None)


def _ffn_in_kernel(h_ref, wg_ref, wu_ref, o_ref):
    h = h_ref[...]
    o_ref[...] = (jax.nn.silu(_dot(h, wg_ref[...])) * _dot(h, wu_ref[...])).astype(BF)


def _ffn_in(h, w_in, *, tm):
    r, d = h.shape
    d_ff = w_in.shape[1] // 2
    tn = 512
    n_j = d_ff // tn
    return pl.pallas_call(
        _ffn_in_kernel,
        out_shape=jax.ShapeDtypeStruct((r, d_ff), BF),
        grid=(r // tm, n_j),
        in_specs=[pl.BlockSpec((tm, d), lambda i, j: (i, 0)),
                  pl.BlockSpec((d, tn), lambda i, j: (0, j)),
                  pl.BlockSpec((d, tn), lambda i, j: (0, j + n_j))],
        out_specs=pl.BlockSpec((tm, tn), lambda i, j: (i, j)),
        compiler_params=_cparams(2, 40),
        name="ffn_in",
    )(h, w_in, w_in)


KV_SLABS = N_KV_HEADS * 2


def _kv_proj_kernel(y_ref, w_ref, f0_ref, f1_ref, f2_ref, b1_ref, b2_ref, xb):
    j = pl.program_id(1)

    @pl.when(j == 0)
    def _():
        xb[...] = y_ref[...].astype(BF)

    r = _dot(xb[...], w_ref[...])
    tm = r.shape[0]
    for idx, f_ref in enumerate((f0_ref, f1_ref, f2_ref)):
        @pl.when(j == idx)
        def _(f_ref=f_ref):
            for hk in range(KV_SLABS):
                f_ref[pl.ds(hk, tm, stride=KV_SLABS), :] = r[:, hk * HEAD_DIM:(hk + 1) * HEAD_DIM]

    for idx, b_ref in ((1, b1_ref), (2, b2_ref)):
        @pl.when(j == idx)
        def _(b_ref=b_ref):
            b_ref[...] = r.astype(BF)


def _kv_proj(y, w_kv, *, tm):
    r, d = y.shape
    n = w_kv.shape[1] // 3
    flat = jax.ShapeDtypeStruct((r * KV_SLABS, HEAD_DIM), F32)
    fspec = pl.BlockSpec((tm * KV_SLABS, HEAD_DIM), lambda i, j: (i, 0))
    wide = jax.ShapeDtypeStruct((r, n), BF)
    wspec = pl.BlockSpec((tm, n), lambda i, j: (i, 0))
    return pl.pallas_call(
        _kv_proj_kernel,
        out_shape=(flat, flat, flat, wide, wide),
        grid=(r // tm, 3),
        in_specs=[pl.BlockSpec((tm, d), lambda i, j: (i, 0)),
                  pl.BlockSpec((d, n), lambda i, j: (0, j))],
        out_specs=(fspec, fspec, fspec, wspec, wspec),
        scratch_shapes=[pltpu.VMEM((tm, d), BF)],
        compiler_params=_cparams(2, 48),
        name="kv_proj",
    )(y, w_kv)


def _qg_proj_kernel(h_ref, wq_ref, wg_ref, q_ref, g_ref):
    h = h_ref[...]
    q_ref[...] = _dot(h, wq_ref[...]).astype(BF)

    @pl.when(pl.program_id(1) == 0)
    def _():
        g_ref[...] = jax.nn.sigmoid(_dot(h, wg_ref[...]))


def _qg_proj(h, w_q, w_g, *, tm):
    r, d = h.shape
    qw = w_q.shape[1]
    gw = w_g.shape[1]
    tn = 512
    return pl.pallas_call(
        _qg_proj_kernel,
        out_shape=(jax.ShapeDtypeStruct((r, qw), BF), jax.ShapeDtypeStruct((r, gw), F32)),
        grid=(r // tm, qw // tn),
        in_specs=[pl.BlockSpec((tm, d), lambda i, j: (i, 0)),
                  pl.BlockSpec((d, tn), lambda i, j: (0, j)),
                  pl.BlockSpec((d, gw), lambda i, j: (0, 0))],
        out_specs=(pl.BlockSpec((tm, tn), lambda i, j: (i, j)),
                   pl.BlockSpec((tm, gw), lambda i, j: (i, 0))),
        compiler_params=_cparams(2, 40),
        name="qg_proj",
    )(h, w_q, w_g)


PAGES_PER_STEP = 16
CHUNKS_PER_PAGE = PAGE_SIZE // CMP_STRIDE


PAGE_FLAT_ROWS = PAGE_SIZE * KV_SLABS
CHUNK_FLAT_ROWS = CMP_STRIDE * KV_SLABS


def _compress_kernel(pt_ref, *refs, n_steps):
    page_refs = refs[:PAGES_PER_STEP]
    w1_ref, pe_ref, b1_ref, w2_ref, b2_ref, o_ref, p0_scr, p1_scr = refs[PAGES_PER_STEP:]
    t = pl.program_id(1)
    rows_step = PAGES_PER_STEP * CHUNKS_PER_PAGE * KV_SLABS
    n_rows = n_steps * rows_step
    hidden = w2_ref.shape[0]

    @pl.when(t == 0)
    def _():
        p1_scr[pl.ds(n_rows, KV_SLABS), :] = jnp.zeros((KV_SLABS, hidden), F32)

    per_s = []
    for s in range(CMP_STRIDE):
        rows = [page_refs[p][pl.ds(c * CHUNK_FLAT_ROWS + s * KV_SLABS, KV_SLABS), :]
                for p in range(PAGES_PER_STEP) for c in range(CHUNKS_PER_PAGE)]
        per_s.append(jnp.concatenate(rows, axis=0))
    x = jnp.concatenate(per_s, axis=1).astype(BF)
    pr = _dot(x, w1_ref[...])
    is_v = _mod_pow2(lax.broadcasted_iota(jnp.int32, (rows_step, 1), 0), 2) == 1
    pk = jnp.where(is_v, pr[:, 2 * hidden:], pr[:, :2 * hidden])
    row0 = pl.multiple_of(t * rows_step, rows_step)
    p0_scr[pl.ds(row0, rows_step), :] = pk[:, :hidden]
    p1_scr[pl.ds(row0, rows_step), :] = pk[:, hidden:]

    @pl.when(t == n_steps - 1)
    def _():
        cst = _dot(pe_ref[...].astype(BF), w1_ref[...])
        const_k = cst[0:1, :hidden] + cst[1:2, hidden:2 * hidden] + b1_ref[0:1, :]
        const_v = cst[2:3, 2 * hidden:3 * hidden] + cst[3:4, 3 * hidden:] + b1_ref[1:2, :]
        v_rows = _mod_pow2(lax.broadcasted_iota(jnp.int32, (n_rows, 1), 0), 2) == 1
        pre = p0_scr[0:n_rows, :] + p1_scr[KV_SLABS:n_rows + KV_SLABS, :] + jnp.where(v_rows, const_v, const_k)
        o2 = _dot(jax.nn.gelu(pre).astype(BF), w2_ref[...])
        o_ref[...] = jnp.where(v_rows, o2[:, HEAD_DIM:] + b2_ref[1:2, :], o2[:, :HEAD_DIM] + b2_ref[0:1, :])


def _compress(pages_flat, page_table, w1all, pe16, b1, w2all, b2):
    nb, n_pages = page_table.shape
    n_steps = n_pages // PAGES_PER_STEP
    n_rows = n_pages * CHUNKS_PER_PAGE * KV_SLABS
    hidden = w2all.shape[0]

    def page_spec(p):
        return pl.BlockSpec((PAGE_FLAT_ROWS, HEAD_DIM), lambda b, t, pt: (pt[b, t * PAGES_PER_STEP + p], 0))

    full = lambda a: pl.BlockSpec(a.shape, lambda b, t, pt: (0,) * a.ndim)
    grid_spec = pltpu.PrefetchScalarGridSpec(
        num_scalar_prefetch=1,
        grid=(nb, n_steps),
        in_specs=[page_spec(p) for p in range(PAGES_PER_STEP)] + [
            full(w1all), full(pe16), full(b1), full(w2all), full(b2)],
        out_specs=pl.BlockSpec((None, n_rows, HEAD_DIM), lambda b, t, pt: (b, 0, 0)),
        scratch_shapes=[pltpu.VMEM((n_rows, hidden), F32), pltpu.VMEM((n_rows + KV_SLABS, hidden), F32)],
    )
    return pl.pallas_call(
        functools.partial(_compress_kernel, n_steps=n_steps),
        out_shape=jax.ShapeDtypeStruct((nb, n_rows, HEAD_DIM), F32),
        grid_spec=grid_spec,
        compiler_params=_cparams(2, 48),
        name="compress_blocks",
    )(page_table, *([pages_flat] * PAGES_PER_STEP), w1all, pe16, b1, w2all, b2)


def _overlap_map(n_cmp_rows, n_slc, n_cols):
    cmp_lo = np.arange(n_cmp_rows)[:, None] * CMP_STRIDE
    slc_lo = np.arange(n_cols)[None, :] * SEL_BLOCK
    ov = np.minimum(cmp_lo + CMP_LEN, slc_lo + SEL_BLOCK) - np.maximum(cmp_lo, slc_lo)
    w = np.clip(ov, 0, None).astype(np.float32) / CMP_LEN
    w[:, n_slc:] = 0.0
    return w


SEL_CHUNK = 256


def _attn_prompt_kernel(q_ref, gt_ref, cmp_ref, sel_ref, win_ref, wmap_t_ref, exp_ref, o_ref,
                        s_sel, s_win, m_run, l_run, acc, o_sel_scr, *, scale, n_slc, n_qb, n_widths):
    g = pl.program_id(1)
    qb = pl.program_id(2)
    q4 = q_ref[...]
    nq = q4.shape[0]
    rq = q4.shape[1] // HEAD_DIM
    qs = jnp.concatenate([q4[:, r * HEAD_DIM:(r + 1) * HEAD_DIM] for r in range(rq)], axis=0)
    qpos = qb * nq + lax.broadcasted_iota(jnp.int32, (nq, 1), 0)
    rows = [slice(r * nq, (r + 1) * nq) for r in range(rq)]

    def key_loop(n_chunks, body):
        for c in range(n_chunks):
            body(c)

    def attend(n_chunks, chunk, s_scr, scores, mask, values):
        m_run[...] = jnp.full(m_run.shape, -jnp.inf, F32)
        l_run[...] = jnp.zeros(l_run.shape, F32)
        acc[...] = jnp.zeros(acc.shape, F32)

        def scores_pass(c):
            s = scores(c)
            ok = mask(c)
            for rs in rows:
                sm = jnp.where(ok, s[rs], -jnp.inf)
                s_scr[c, rs, :] = sm
                mx = m_run[rs, :]
                for j in range(chunk // LANES):
                    mx = jnp.maximum(mx, sm[:, j * LANES:(j + 1) * LANES])
                m_run[rs, :] = mx

        key_loop(n_chunks, scores_pass)
        m = jnp.max(m_run[...], axis=-1, keepdims=True)
        m = jnp.where(m == -jnp.inf, 0.0, m)

        def values_pass(c):
            es = []
            for rs in rows:
                e = jnp.exp2((s_scr[c, rs, :] - m[rs]) * (scale * LOG2_E))
                ls = l_run[rs, :]
                for j in range(chunk // LANES):
                    ls = ls + e[:, j * LANES:(j + 1) * LANES]
                l_run[rs, :] = ls
                es.append(e.astype(BF))
            acc[...] += _dot(jnp.concatenate(es, axis=0), values(c))

        key_loop(n_chunks, values_pass)
        l = jnp.sum(l_run[...], axis=-1, keepdims=True)
        return acc[...] * (1.0 / jnp.maximum(l, 1e-30))

    n_cmp_rows = cmp_ref.shape[0] // KV_SLABS
    kc = cmp_ref[pl.ds(g * 2, n_cmp_rows, stride=KV_SLABS), :].astype(BF)
    vc = cmp_ref[pl.ds(g * 2 + 1, n_cmp_rows, stride=KV_SLABS), :].astype(BF)
    n_io = lax.broadcasted_iota(jnp.int32, (1, n_cmp_rows), 1)
    valid_c = (n_io * CMP_STRIDE + (CMP_LEN - 1)) <= qpos
    s_c = _dot_t(qs, kc) * scale
    ps = [_masked_softmax(s_c[rs], valid_c).astype(BF) for rs in rows]
    o_cmp = _dot(jnp.concatenate(ps, axis=0), vc)

    imp = _dot_t(wmap_t_ref[...], jnp.concatenate(ps, axis=1))
    n_rows = imp.shape[0]
    j_io = lax.broadcasted_iota(jnp.int32, (n_rows, 1), 0)
    q_blk = _div_pow2(qb * nq + lax.broadcasted_iota(jnp.int32, (1, nq), 1), SEL_BLOCK)
    causal = j_io <= q_blk
    forced = (j_io == 0) | (causal & (j_io > q_blk - N_LOCAL_SEL))
    imp = jnp.where(forced, imp + FORCE_BONUS, imp)
    imp = jnp.where(causal, imp, -jnp.inf)
    rank = jnp.zeros(imp.shape, F32)
    for jp in range(n_slc):
        other = imp[jp:jp + 1, :]
        rank = rank + jnp.where(other == imp, jnp.where(jp < j_io, 1.0, 0.0), jnp.where(other > imp, 1.0, 0.0))
    chosen_t = jnp.where(rank < min(N_SEL, n_slc), jnp.where(imp > -jnp.inf, 1.0, 0.0), 0.0)
    chosen_t = jnp.concatenate([chosen_t, jnp.zeros((LANES - n_rows, nq), F32)], axis=0)
    chosen = chosen_t.T.astype(BF)

    span = WINDOW + nq
    k0 = pl.multiple_of(jnp.maximum(qb * nq - WINDOW, 0), nq)
    kp = k0 + lax.broadcasted_iota(jnp.int32, (1, span), 1)
    valid_w = (kp <= qpos) & (kp > qpos - WINDOW)
    o_win = attend(1, span, s_win,
                   lambda c: _dot_t(qs, win_ref[pl.ds(k0, span), :HEAD_DIM]), lambda c: valid_w,
                   lambda c: win_ref[pl.ds(k0, span), HEAD_DIM:])

    def sel_rows(c):
        return slice(c * SEL_CHUNK, (c + 1) * SEL_CHUNK)

    def sel_mask(c):
        t_io = c * SEL_CHUNK + lax.broadcasted_iota(jnp.int32, (1, SEL_CHUNK), 1)
        return (_dot(chosen, exp_ref[c]) > 0.5) & (t_io <= qpos)

    qb_per_width = n_qb // n_widths
    for wi in range(n_widths):
        n_sel_chunks = -(-(wi + 1) * qb_per_width * nq // SEL_CHUNK)

        @pl.when((qb >= wi * qb_per_width) & (qb < (wi + 1) * qb_per_width))
        def _(n_sel_chunks=n_sel_chunks):
            o_sel_scr[...] = attend(n_sel_chunks, SEL_CHUNK, s_sel,
                                    lambda c: _dot_t(qs, sel_ref[sel_rows(c), :HEAD_DIM]), sel_mask,
                                    lambda c: sel_ref[sel_rows(c), HEAD_DIM:])

    o_sel = o_sel_scr[...]
    gt = gt_ref[...]
    outs = []
    for r, rs in enumerate(rows):
        outs.append(gt[:, 3 * r:3 * r + 1] * o_cmp[rs] + gt[:, 3 * r + 1:3 * r + 2] * o_sel[rs]
                    + gt[:, 3 * r + 2:3 * r + 3] * o_win[rs])
    o_ref[...] = jnp.concatenate(outs, axis=1).astype(BF)


def _attn_prompt(q, gates, cmp_flat, kv_sel, kv_win, *, scale):
    nb, t, qw = q.shape
    n_g = N_KV_HEADS
    gw = qw // n_g
    kvw = 2 * HEAD_DIM
    cmp_rows = cmp_flat.shape[1]
    n_slc = t // SEL_BLOCK
    n_slc_rows = -(-n_slc // 8) * 8
    rq = gw // HEAD_DIM
    n_qb = t // Q_BLOCK
    n_chunks = t // SEL_CHUNK
    wmap_t = jnp.asarray(np.tile(_overlap_map(cmp_rows // KV_SLABS, n_slc, n_slc_rows), (rq, 1)).T, BF)
    key_blk = (np.arange(t) // SEL_BLOCK).reshape(n_chunks, 1, SEL_CHUNK)
    expand = jnp.asarray((key_blk == np.arange(LANES)[None, :, None]).astype(np.float32), BF)
    rows = rq * Q_BLOCK
    n_widths = 4 if n_qb % 4 == 0 else 1
    return pl.pallas_call(
        functools.partial(_attn_prompt_kernel, scale=scale, n_slc=n_slc, n_qb=n_qb, n_widths=n_widths),
        out_shape=jax.ShapeDtypeStruct((nb, t, qw), BF),
        grid=(nb, n_g, n_qb),
        in_specs=[pl.BlockSpec((None, Q_BLOCK, gw), lambda b, g, i: (b, i, g)),
                  pl.BlockSpec((None, Q_BLOCK, LANES), lambda b, g, i: (b, i, g)),
                  pl.BlockSpec((None, cmp_rows, HEAD_DIM), lambda b, g, i: (b, 0, 0)),
                  pl.BlockSpec((None, t, kvw), lambda b, g, i: (b, 0, g)),
                  pl.BlockSpec((None, t, kvw), lambda b, g, i: (b, 0, g)),
                  pl.BlockSpec(wmap_t.shape, lambda b, g, i: (0, 0)),
                  pl.BlockSpec(expand.shape, lambda b, g, i: (0, 0, 0))],
        out_specs=pl.BlockSpec((None, Q_BLOCK, gw), lambda b, g, i: (b, i, g)),
        scratch_shapes=[pltpu.VMEM((n_chunks, rows, SEL_CHUNK), F32),
                        pltpu.VMEM((1, rows, WINDOW + Q_BLOCK), F32),
                        pltpu.VMEM((rows, LANES), F32), pltpu.VMEM((rows, LANES), F32),
                        pltpu.VMEM((rows, HEAD_DIM), F32), pltpu.VMEM((rows, HEAD_DIM), F32)],
        compiler_params=_cparams(3, 48),
        name="attn_prompt",
    )(q, gates, cmp_flat, kv_sel, kv_win, wmap_t, expand)


def _attn_sample_a_kernel(q_ref, cmp_ref, win_ref, wmap_ref, ocmp_ref, owin_ref, idx_ref, val_ref, *,
                          scale, q_pos, n_cmp, n_slc, win_pos0):
    q = q_ref[...]
    n_h = q.shape[0]
    rq = n_h // N_KV_HEADS
    row_g = _div_pow2(lax.broadcasted_iota(jnp.int32, (n_h, 1), 0), rq)
    cmp_rows = cmp_ref.shape[0] // KV_SLABS
    win_rows = win_ref.shape[0] // KV_SLABS
    n_io = lax.broadcasted_iota(jnp.int32, (1, cmp_rows), 1)
    valid_c = (n_io < n_cmp) & ((n_io * CMP_STRIDE + (CMP_LEN - 1)) <= q_pos)
    kp = win_pos0 + lax.broadcasted_iota(jnp.int32, (1, win_rows), 1)
    valid_w = (kp <= q_pos) & (kp > q_pos - WINDOW)
    o_cmp = jnp.zeros((n_h, HEAD_DIM), F32)
    o_win = jnp.zeros((n_h, HEAD_DIM), F32)
    imp = jnp.zeros((n_h, wmap_ref.shape[1]), F32)
    for g in range(N_KV_HEADS):
        in_g = row_g == g
        kc = cmp_ref[pl.ds(2 * g, cmp_rows, stride=KV_SLABS), :].astype(BF)
        vc = cmp_ref[pl.ds(2 * g + 1, cmp_rows, stride=KV_SLABS), :].astype(BF)
        p = _masked_softmax(_dot_t(q, kc) * scale, valid_c).astype(BF)
        o_cmp = jnp.where(in_g, _dot(p, vc), o_cmp)
        imp_rows = _dot(p, wmap_ref[...])
        imp_g = jnp.sum(jnp.where(in_g, imp_rows, 0.0), axis=0, keepdims=True)
        imp = jnp.where(in_g, imp_g, imp)
        kw = win_ref[pl.ds(2 * g, win_rows, stride=KV_SLABS), :].astype(BF)
        vw = win_ref[pl.ds(2 * g + 1, win_rows, stride=KV_SLABS), :].astype(BF)
        pw = _masked_softmax(_dot_t(q, kw) * scale, valid_w).astype(BF)
        o_win = jnp.where(in_g, _dot(pw, vw), o_win)
    ocmp_ref[...] = o_cmp
    owin_ref[...] = o_win

    j_io = lax.broadcasted_iota(jnp.int32, (1, imp.shape[1]), 1)
    q_blk = q_pos // SEL_BLOCK
    causal = j_io <= q_blk
    forced = (j_io == 0) | (causal & (j_io > q_blk - N_LOCAL_SEL))
    imp = jnp.where(forced, imp + FORCE_BONUS, imp)
    imp = jnp.where(causal, imp, -jnp.inf)
    rank = _rank_desc(imp, n_slc)
    jf = j_io.astype(F32)
    k_io = lax.broadcasted_iota(jnp.int32, (1, idx_ref.shape[1]), 1)
    idx = jnp.zeros(idx_ref.shape, F32)
    val = jnp.zeros(idx_ref.shape, F32)
    for k in range(min(N_SEL, n_slc)):
        hit = rank == k
        ik = jnp.sum(jnp.where(hit, jf, 0.0), axis=1, keepdims=True)
        vk = jnp.sum(jnp.where(hit, jnp.where(imp > -jnp.inf, 1.0, 0.0), 0.0), axis=1, keepdims=True)
        idx = jnp.where(k_io == k, ik, idx)
        val = jnp.where(k_io == k, vk, val)
    idx_ref[...] = idx.astype(jnp.int32)
    val_ref[...] = val.astype(jnp.int32)


def _attn_sample_a(q, cmp_blocks, win_rows, *, scale, q_pos, n_cmp, n_slc, win_pos0):
    nb, n_h, d = q.shape
    n_cols = -(-n_slc // LANES) * LANES
    wmap = jnp.asarray(_overlap_map(cmp_blocks.shape[1] // KV_SLABS, n_slc, n_cols), BF)
    o_sds = jax.ShapeDtypeStruct((nb, n_h, d), F32)
    i_sds = jax.ShapeDtypeStruct((nb, n_h, LANES), jnp.int32)
    blk = lambda a: pl.BlockSpec((None,) + a.shape[1:], lambda b: (b, 0, 0))
    ospec = pl.BlockSpec((None, n_h, d), lambda b: (b, 0, 0))
    ispec = pl.BlockSpec((None, n_h, LANES), lambda b: (b, 0, 0))
    return pl.pallas_call(
        functools.partial(_attn_sample_a_kernel, scale=scale, q_pos=q_pos, n_cmp=n_cmp, n_slc=n_slc,
                          win_pos0=win_pos0),
        out_shape=(o_sds, o_sds, i_sds, i_sds),
        grid=(nb,),
        in_specs=[blk(q), blk(cmp_blocks), blk(win_rows), pl.BlockSpec(wmap.shape, lambda b: (0, 0))],
        out_specs=(ospec, ospec, ispec, ispec),
        compiler_params=_cparams(1, 32),
        name="attn_sample_a",
    )(q, cmp_blocks, win_rows, wmap)


def _attn_sample_b_kernel(idx_ref, val_ref, pt_ref, q_ref, new_ref, gt_ref, ocmp_ref, owin_ref, *rest,
                          scale, q_pos, past_len):
    blk_refs = rest[:N_SEL]
    o_ref = rest[N_SEL]
    b = pl.program_id(0)
    g = pl.program_id(1)
    q = q_ref[...]
    n_h = q.shape[0]
    rq = n_h // N_KV_HEADS
    base = (b * N_KV_HEADS + g) * N_SEL
    n_keys = N_SEL * SEL_BLOCK
    lane = lax.broadcasted_iota(jnp.int32, (1, n_keys), 1)
    k_new = new_ref[pl.ds(2 * g, 1), :]
    v_new = new_ref[pl.ds(2 * g + 1, 1), :]
    ks, vs = [], []
    kpos = _mod_pow2(lane, SEL_BLOCK)
    kval = jnp.zeros((1, n_keys), jnp.int32)
    lane_blk = _div_pow2(lane, SEL_BLOCK)
    for k in range(N_SEL):
        blk = idx_ref[base + k]
        is_new = blk * SEL_BLOCK >= past_len
        ks.append(jnp.where(is_new, k_new, blk_refs[k][pl.ds(2 * g, SEL_BLOCK, stride=KV_SLABS), :]).astype(BF))
        vs.append(jnp.where(is_new, v_new, blk_refs[k][pl.ds(2 * g + 1, SEL_BLOCK, stride=KV_SLABS), :]).astype(BF))
        in_k = lane_blk == k
        kpos = kpos + jnp.where(in_k, blk * SEL_BLOCK, 0)
        kval = jnp.where(in_k, val_ref[base + k], kval)
    valid = (kval > 0) & (kpos <= q_pos)
    p = _masked_softmax(_dot_t(q, jnp.concatenate(ks, axis=0)) * scale, valid).astype(BF)
    o_sel = _dot(p, jnp.concatenate(vs, axis=0))
    gt = gt_ref[...]
    in_g = _div_pow2(lax.broadcasted_iota(jnp.int32, (n_h, 1), 0), rq) == g

    @pl.when(g == 0)
    def _():
        o_ref[...] = gt[:, 0:1] * ocmp_ref[...] + gt[:, 2:3] * owin_ref[...]

    o_ref[...] += jnp.where(in_g, gt[:, 1:2] * o_sel, 0.0)


def _attn_sample_b(sel_idx, sel_val, page_table, q, kv_sel_new, gates, o_cmp, o_win, pool_sel, *,
                   scale, q_pos, past_len):
    nb, n_h, d = q.shape
    halves = PAGE_SIZE // SEL_BLOCK
    blk_rows = SEL_BLOCK * KV_SLABS
    pool = pool_sel.reshape(pool_sel.shape[0] // blk_rows, blk_rows, d)

    def blk_spec(k):
        def index_map(b, g, idx, val, pt):
            row0 = jnp.minimum(idx[(b * N_KV_HEADS + g) * N_SEL + k] * SEL_BLOCK, past_len - 1)
            return (pt[b, row0 // PAGE_SIZE] * halves + (row0 % PAGE_SIZE) // SEL_BLOCK, 0, 0)
        return pl.BlockSpec((None, blk_rows, d), index_map)

    per_b = lambda a: pl.BlockSpec((None,) + a.shape[1:], lambda b, g, idx, val, pt: (b, 0, 0))
    new_rows = kv_sel_new.reshape(nb, KV_SLABS, d)
    grid_spec = pltpu.PrefetchScalarGridSpec(
        num_scalar_prefetch=3,
        grid=(nb, N_KV_HEADS),
        in_specs=[per_b(q), per_b(new_rows), per_b(gates), per_b(o_cmp), per_b(o_win)]
        + [blk_spec(k) for k in range(N_SEL)],
        out_specs=pl.BlockSpec((None, n_h, d), lambda b, g, idx, val, pt: (b, 0, 0)),
    )
    return pl.pallas_call(
        functools.partial(_attn_sample_b_kernel, scale=scale, q_pos=q_pos, past_len=past_len),
        out_shape=jax.ShapeDtypeStruct((nb, n_h, d), F32),
        grid_spec=grid_spec,
        compiler_params=_cparams(2, 32),
        name="attn_sample_b",
    )(sel_idx, sel_val, page_table, q, new_rows, gates, o_cmp, o_win, *([pool] * N_SEL))


def kernel(x_prompt, x_sample, cache_cmp_kv, cache_sel_kv, cache_win_kv, page_table, c_prompt, c_sample, w_ada, b_ada, ln_g, ln_b, a_w_in, a_v_g, a_v_b, a_w_s, a_b_s, a_w_out, w_kv, cmp_pe, cmp_w1, cmp_b1, cmp_w2, cmp_b2, b_w_qg, b_w_o, ffn_w_in, ffn_w_out):
    nb, t, d = x_prompt.shape
    ns, tq, _ = x_sample.shape
    assert tq == 1, "the sample path handles one new token per sequence"
    depth = w_ada.shape[0]
    n_a = a_w_in.shape[0]
    alpha = float((2 * depth) ** 0.25)
    scale = float(HEAD_DIM ** -0.5)
    n_pool = cache_cmp_kv.shape[0]
    past_len = page_table.shape[1] * PAGE_SIZE
    kvw = N_KV_HEADS * 2 * HEAD_DIM
    q_width = b_w_o.shape[1]
    n_heads = q_width // HEAD_DIM
    rq = n_heads // N_KV_HEADS

    mod = _ada_modulation(jnp.concatenate([c_prompt, c_sample], axis=0), w_ada, b_ada)

    def mods(layer, sub):
        m = mod[layer * 2 + sub]
        out = []
        for part in range(3):
            sl = m[:, part * d:(part + 1) * d]
            out.append((sl[:nb].reshape(nb, 1, d), sl[nb:].reshape(1, ns, d)))
        return out

    wb = lambda w: w.astype(BF)
    hidden = cmp_w1.shape[-1]
    halves = CMP_LEN // CMP_STRIDE
    assert halves == 2, "a compression block spans two stride-chunks"
    w1all = wb(cmp_w1.reshape(2, halves, CMP_STRIDE, HEAD_DIM, hidden)
               .transpose(2, 3, 0, 1, 4).reshape(CMP_STRIDE * HEAD_DIM, 2 * halves * hidden))
    pe16 = jnp.pad(cmp_pe.reshape(2 * halves, CMP_STRIDE * HEAD_DIM), ((0, 16 - 2 * halves), (0, 0)))
    w2all = wb(cmp_w2.transpose(1, 0, 2).reshape(hidden, 2 * HEAD_DIM))

    tm_p = 512
    tpg_p = t // tm_p
    prompt = dict(tm=tm_p, tiles_per_group=tpg_p)
    sample = dict(tm=ns, tiles_per_group=1)

    xp = x_prompt.reshape(nb * t, d)
    xs = x_sample.reshape(ns, d)
    hp = hs = None
    chunk_v = []
    outs = {}
    for layer in range(depth):
        (sh_p, sh_s), (sc_p, sc_s), (gt_p, gt_s) = mods(layer, 0)
        (fsh_p, fsh_s), (fsc_p, fsc_s), (fgt_p, fgt_s) = mods(layer, 1)
        if layer < n_a:
            w_in = wb(a_w_in[layer])
            w_out = wb(a_w_out[layer])
            ap = _gmlp_in(xp, sc_p, sh_p, w_in, a_v_g[layer], a_v_b[layer], a_w_s[layer], a_b_s[layer],
                          chunked=True, **prompt)
            as_, v_rows = _gmlp_in(xs, sc_s, sh_s, w_in, a_v_g[layer], a_v_b[layer], a_w_s[layer], a_b_s[layer],
                                   chunked=False, **sample)
            chunk_v.append(v_rows.reshape(ns, tq, -1))
            xp, hp = _proj_res_ln(ap, w_out, xp, gt_p, ln_g[layer, 0], ln_b[layer, 0], (fsc_p, fsh_p),
                                  alpha=alpha, **prompt)
            xs, hs = _proj_res_ln(as_, w_out, xs, gt_s, ln_g[layer, 0], ln_b[layer, 0], (fsc_s, fsh_s),
                                  alpha=alpha, **sample)
        else:
            if layer == n_a:
                w_kvb = wb(w_kv)
                cmp_p, sel_p, win_p, sel_pb, win_pb = _kv_proj(xp, w_kvb, tm=tm_p)
                cmp_s, sel_s, win_s, _, _ = _kv_proj(xs, w_kvb, tm=ns)
                pages_per_seq_p = t // PAGE_SIZE
                ident = jnp.arange(nb * pages_per_seq_p, dtype=jnp.int32).reshape(nb, pages_per_seq_p)
                cmp_blocks_p = _compress(cmp_p, ident, w1all, pe16, cmp_b1, w2all, cmp_b2)
                assert (past_len + tq) // CMP_STRIDE == past_len // CMP_STRIDE
                cmp_blocks_s = _compress(cache_cmp_kv.reshape(n_pool * PAGE_FLAT_ROWS, HEAD_DIM), page_table,
                                         w1all, pe16, cmp_b1, w2all, cmp_b2)
                n_win = cache_win_kv.shape[1]
                win_keys_s = jnp.concatenate([cache_win_kv.reshape(ns, n_win * KV_SLABS, HEAD_DIM),
                                              win_s.reshape(ns, tq * KV_SLABS, HEAD_DIM)],
                                             axis=1)[:, -n_win * KV_SLABS:]
                outs["kv_p"] = (cmp_p, sel_p, win_p)
                outs["kv_pb"] = (sel_pb.reshape(nb, t, kvw), win_pb.reshape(nb, t, kvw))
                outs["kv_s"] = (cmp_s, sel_s, win_keys_s)
            j = layer - n_a
            w_qg = b_w_qg[j]
            w_q = wb(w_qg[:, :q_width])
            w_gate = w_qg[:, q_width:].reshape(d, N_KV_HEADS, rq * 3)
            w_gate = wb(jnp.pad(w_gate, ((0, 0), (0, 0), (0, LANES - rq * 3))).reshape(d, N_KV_HEADS * LANES))
            w_o = wb(b_w_o[j])
            assert layer > 0, "an attention layer must follow another layer"
            q_p, g_p = _qg_proj(hp, w_q, w_gate, tm=tm_p)
            q_s, g_s = _qg_proj(hs, w_q, w_gate, tm=ns)
            o_p = _attn_prompt(q_p.reshape(nb, t, q_width), g_p.reshape(nb, t, N_KV_HEADS * LANES),
                               cmp_blocks_p, outs["kv_pb"][0], outs["kv_pb"][1], scale=scale)
            n_cmp_s = (past_len + tq) // CMP_STRIDE - CMP_LEN // CMP_STRIDE + 1
            n_slc_s = -(-(past_len + tq) // SEL_BLOCK)
            q_s3 = q_s.reshape(ns, n_heads, HEAD_DIM)
            o_cmp_s, o_win_s, idx_s, val_s = _attn_sample_a(
                q_s3, cmp_blocks_s, outs["kv_s"][2], scale=scale, q_pos=past_len, n_cmp=n_cmp_s, n_slc=n_slc_s,
                win_pos0=past_len + tq - outs["kv_s"][2].shape[1] // KV_SLABS)
            sel_idx = idx_s[:, ::rq, :N_SEL].reshape(-1)
            sel_val = val_s[:, ::rq, :N_SEL].reshape(-1)
            gates_s = g_s.reshape(ns, N_KV_HEADS, LANES)[:, :, :rq * 3].reshape(ns, n_heads, 3)
            o_s = _attn_sample_b(sel_idx, sel_val, page_table, q_s3, outs["kv_s"][1], gates_s, o_cmp_s, o_win_s,
                                 cache_sel_kv.reshape(n_pool * PAGE_FLAT_ROWS, HEAD_DIM), scale=scale,
                                 q_pos=past_len, past_len=past_len)
            xp, hp = _proj_res_ln(o_p.reshape(nb * t, q_width), w_o, xp, gt_p, ln_g[layer, 0], ln_b[layer, 0],
                                  (fsc_p, fsh_p), alpha=alpha, **prompt)
            xs, hs = _proj_res_ln(o_s.reshape(ns, q_width), w_o, xs, gt_s, ln_g[layer, 0], ln_b[layer, 0],
                                  (fsc_s, fsh_s), alpha=alpha, **sample)
        f_in = wb(ffn_w_in[layer])
        f_out = wb(ffn_w_out[layer])
        fp = _ffn_in(hp, f_in, tm=2 * tm_p)
        fs = _ffn_in(hs, f_in, tm=ns)
        nxt_p = nxt_s = None
        if n_a <= layer + 1 < depth:
            (nsh_p, nsh_s), (nsc_p, nsc_s), _ = mods(layer + 1, 0)
            nxt_p, nxt_s = (nsc_p, nsh_p), (nsc_s, nsh_s)
        tm_f = tm_p // 2
        xp, hp = _proj_res_ln(fp, f_out, xp, fgt_p, ln_g[layer, 1], ln_b[layer, 1], nxt_p, alpha=alpha,
                              tm=tm_f, tiles_per_group=t // tm_f)
        xs, hs = _proj_res_ln(fs, f_out, xs, fgt_s, ln_g[layer, 1], ln_b[layer, 1], nxt_s, alpha=alpha, **sample)

    cmp_p, sel_p, win_p = outs["kv_p"]
    cmp_s, sel_s, win_keys_s = outs["kv_s"]
    kv_shape = (N_KV_HEADS, 2, HEAD_DIM)
    n_win_p = min(WINDOW, t)
    return (xp.reshape(nb, t, d), xs.reshape(ns, tq, d),
            cmp_p.reshape((nb, t) + kv_shape), sel_p.reshape((nb, t) + kv_shape),
            win_p.reshape((nb, t) + kv_shape)[:, -n_win_p:],
            cmp_s.reshape((ns, tq) + kv_shape), sel_s.reshape((ns, tq) + kv_shape),
            win_keys_s.reshape((ns, -1) + kv_shape),
            jnp.stack(chunk_v))
```

```python
import functools

import numpy as np
import jax
import jax.numpy as jnp
from jax import lax
from jax.experimental import pallas as pl
from jax.experimental.pallas import tpu as pltpu

BF = jnp.bfloat16
F32 = jnp.float32

LN_EPS = 1e-5
CHUNK = 128
HEAD_DIM = 128
N_KV_HEADS = 4
CMP_LEN = 32
CMP_STRIDE = 16
SEL_BLOCK = 64
N_SEL = 16
N_LOCAL_SEL = 2
FORCE_BONUS = 1e4
LOG2_E = 1.4426950408889634
WINDOW = 512
Q_BLOCK = 128
PAGE_SIZE = 128

LANES = 128
MIB = 1024 * 1024


def _cparams(n_axes, vmem_mib):
    return pltpu.CompilerParams(dimension_semantics=("arbitrary",) * n_axes,
                                vmem_limit_bytes=vmem_mib * MIB)


def _dot(a, b):
    return jnp.dot(a, b, preferred_element_type=F32)


def _dot_t(a, b):
    return lax.dot_general(a, b, (((1,), (1,)), ((), ())), preferred_element_type=F32)


def _layer_norm(x, g, b):
    mu = jnp.mean(x, axis=-1, keepdims=True)
    xc = x - mu
    var = jnp.mean(xc * xc, axis=-1, keepdims=True)
    return xc * lax.rsqrt(var + LN_EPS) * g + b


def _masked_softmax(s, valid):
    s = jnp.where(valid, s, -jnp.inf)
    m = jnp.max(s, axis=-1, keepdims=True)
    m = jnp.where(m == -jnp.inf, 0.0, m)
    e = jnp.exp(s - m)
    d = jnp.maximum(jnp.sum(e, axis=-1, keepdims=True), 1e-30)
    return e * (1.0 / d)


def _log2(n):
    assert n & (n - 1) == 0, "power of two expected"
    return n.bit_length() - 1


def _div_pow2(x, n):
    return lax.shift_right_logical(x, jnp.int32(_log2(n)))


def _mod_pow2(x, n):
    assert n & (n - 1) == 0, "power of two expected"
    return lax.bitwise_and(x, jnp.int32(n - 1))


def _rank_desc(imp, n):
    j_io = lax.broadcasted_iota(jnp.int32, (1, imp.shape[1]), 1)
    rank = jnp.zeros(imp.shape, F32)
    for jp in range(n):
        col = imp[:, jp:jp + 1]
        before = jnp.where(col == imp, jnp.where(jp < j_io, 1.0, 0.0), jnp.where(col > imp, 1.0, 0.0))
        rank = rank + before
    return rank


def _ada_kernel(c_ref, w_ref, b_ref, o_ref):
    a = jax.nn.silu(c_ref[...]).astype(BF)
    o_ref[...] = _dot(a, w_ref[...].astype(BF)) + b_ref[...]


def _ada_modulation(c_all, w_ada, b_ada):
    n_sub = w_ada.shape[0] * w_ada.shape[1]
    d, n3 = w_ada.shape[2], w_ada.shape[3]
    m = c_all.shape[0]
    tn = 512
    w = w_ada.reshape(n_sub, d, n3)
    b = b_ada.reshape(n_sub, 1, n3)
    return pl.pallas_call(
        _ada_kernel,
        out_shape=jax.ShapeDtypeStruct((n_sub, m, n3), F32),
        grid=(n_sub, n3 // tn),
        in_specs=[pl.BlockSpec((m, d), lambda s, j: (0, 0)),
                  pl.BlockSpec((None, d, tn), lambda s, j: (s, 0, j)),
                  pl.BlockSpec((None, 1, tn), lambda s, j: (s, 0, j))],
        out_specs=pl.BlockSpec((None, m, tn), lambda s, j: (s, 0, j)),
        compiler_params=_cparams(2, 32),
        name="ada_modulation",
    )(c_all, w, b)


def _mod_spec(mod, tiles_per_group):
    rb, d = mod.shape[1], mod.shape[2]
    return pl.BlockSpec((None, rb, d), lambda i, j: (i // tiles_per_group, 0, 0))


def _gmlp_in_kernel(x_ref, sc_ref, sh_ref, w_ref, vg_ref, vb_ref, ws_ref, bs_ref, *rest,
                    n_j, tn, chunked):
    if chunked:
        o_ref, h_scr, z_scr = rest
    else:
        o_ref, v_ref, h_scr, z_scr = rest
    j = pl.program_id(1)

    @pl.when(j == 0)
    def _():
        h_scr[...] = (x_ref[...] * (1.0 + sc_ref[...]) + sh_ref[...]).astype(BF)

    z_scr[j] = jax.nn.gelu(_dot(h_scr[...], w_ref[...]))

    @pl.when(j == n_j - 1)
    def _():
        tm = z_scr.shape[1]
        half = n_j // 2
        e_a = half * tn
        s1 = jnp.zeros((tm, 1), F32)
        for jj in range(half):
            s1 = s1 + jnp.sum(z_scr[half + jj], axis=-1, keepdims=True)
        mu = s1 / e_a
        s2 = jnp.zeros((tm, 1), F32)
        for jj in range(half):
            vc = z_scr[half + jj] - mu
            s2 = s2 + jnp.sum(vc * vc, axis=-1, keepdims=True)
        rstd = lax.rsqrt(s2 / e_a + LN_EPS)
        gpt = tn // CHUNK
        if chunked:
            row = lax.broadcasted_iota(jnp.int32, (CHUNK, CHUNK), 0)
            col = lax.broadcasted_iota(jnp.int32, (CHUNK, CHUNK), 1)
            for jj in range(half):
                for gi in range(gpt):
                    g = jj * gpt + gi
                    cs = slice(gi * CHUNK, (gi + 1) * CHUNK)
                    wm = jnp.where(row >= col, ws_ref[g], 0.0).astype(BF)
                    bcol = bs_ref[:, g:g + 1]
                    vg = vg_ref[:, g * CHUNK:(g + 1) * CHUNK]
                    vb = vb_ref[:, g * CHUNK:(g + 1) * CHUNK]
                    for c in range(tm // CHUNK):
                        rs = slice(c * CHUNK, (c + 1) * CHUNK)
                        vn = (z_scr[half + jj, rs, cs] - mu[rs]) * rstd[rs] * vg + vb
                        mixed = _dot(wm, vn.astype(BF)) + bcol
                        o_ref[rs, g * CHUNK:(g + 1) * CHUNK] = (z_scr[jj, rs, cs] * mixed).astype(BF)
        else:
            for jj in range(half):
                cs = slice(jj * tn, (jj + 1) * tn)
                vn = (z_scr[half + jj] - mu) * rstd * vg_ref[:, cs] + vb_ref[:, cs]
                v_ref[:, cs] = vn
                mixed = ws_ref[:, cs] * vn + bs_ref[:, cs]
                o_ref[:, cs] = (z_scr[jj] * mixed).astype(BF)


def _gmlp_in(x, sc, sh, w_in, v_g, v_b, w_s, b_s, *, tm, tiles_per_group, chunked):
    r, d = x.shape
    n2 = w_in.shape[1]
    e_a = n2 // 2
    tn = 512
    n_j = n2 // tn
    n_g = w_s.shape[0]
    if chunked:
        ws_arr = w_s
        bs_arr = b_s.T
        ws_spec = pl.BlockSpec(ws_arr.shape, lambda i, j: (0, 0, 0))
        bs_spec = pl.BlockSpec(bs_arr.shape, lambda i, j: (0, 0))
        out_shape = jax.ShapeDtypeStruct((r, e_a), BF)
        out_specs = pl.BlockSpec((tm, e_a), lambda i, j: (i, 0))
    else:
        ws_arr = jnp.repeat(w_s[:, 0, 0], e_a // n_g).reshape(1, e_a)
        bs_arr = jnp.repeat(b_s[:, 0], e_a // n_g).reshape(1, e_a)
        ws_spec = pl.BlockSpec((1, e_a), lambda i, j: (0, 0))
        bs_spec = pl.BlockSpec((1, e_a), lambda i, j: (0, 0))
        out_shape = (jax.ShapeDtypeStruct((r, e_a), BF), jax.ShapeDtypeStruct((r, e_a), F32))
        out_specs = (pl.BlockSpec((tm, e_a), lambda i, j: (i, 0)), pl.BlockSpec((tm, e_a), lambda i, j: (i, 0)))
    return pl.pallas_call(
        functools.partial(_gmlp_in_kernel, n_j=n_j, tn=tn, chunked=chunked),
        out_shape=out_shape,
        grid=(r // tm, n_j),
        in_specs=[pl.BlockSpec((tm, d), lambda i, j: (i, 0)),
                  _mod_spec(sc, tiles_per_group), _mod_spec(sh, tiles_per_group),
                  pl.BlockSpec((d, tn), lambda i, j: (0, j)),
                  pl.BlockSpec((1, e_a), lambda i, j: (0, 0)),
                  pl.BlockSpec((1, e_a), lambda i, j: (0, 0)),
                  ws_spec, bs_spec],
        out_specs=out_specs,
        scratch_shapes=[pltpu.VMEM((tm, d), BF), pltpu.VMEM((n_j, tm, tn), F32)],
        compiler_params=_cparams(2, 48),
        name="gmlp_in_chunked" if chunked else "gmlp_in_single",
    )(x, sc, sh, w_in, v_g.reshape(1, e_a), v_b.reshape(1, e_a), ws_arr, bs_arr)


LN_ROWS = 16


def _proj_res_ln_kernel(a_ref, w_ref, x_ref, gate_ref, lg_ref, lb_ref, *rest, alpha, emit_next):
    if emit_next:
        sc_ref, sh_ref, y_ref, h_ref, acc = rest
    else:
        y_ref, acc = rest
    acc[...] = _dot(a_ref[...].astype(BF), w_ref[...])
    tm = acc.shape[0]
    per_row_mod = gate_ref.shape[0] > 1
    step = min(LN_ROWS, tm)
    assert tm % step == 0
    for c in range(tm // step):
        rs = slice(c * step, (c + 1) * step)
        ms = rs if per_row_mod else slice(None)
        t = alpha * x_ref[rs, :] + (1.0 + gate_ref[ms, :]) * acc[rs, :]
        y = _layer_norm(t, lg_ref[...], lb_ref[...])
        y_ref[rs, :] = y
        if emit_next:
            h_ref[rs, :] = (y * (1.0 + sc_ref[ms, :]) + sh_ref[ms, :]).astype(BF)


def _proj_res_ln(a, w, x, gate, ln_g, ln_b, nxt, *, alpha, tm, tiles_per_group, layer=0):
    r, kdim = a.shape
    d = w.shape[2]
    emit_next = nxt is not None
    mod_spec = lambda m: pl.BlockSpec((None,) + m.shape[1:], lambda i: (i // tiles_per_group, 0, 0))
    in_specs = [pl.BlockSpec((tm, kdim), lambda i: (i, 0)),
                pl.BlockSpec((None, kdim, d), lambda i: (layer, 0, 0), pipeline_mode=pl.Buffered(1)),
                pl.BlockSpec((tm, d), lambda i: (i, 0)),
                mod_spec(gate),
                pl.BlockSpec((1, d), lambda i: (0, 0)),
                pl.BlockSpec((1, d), lambda i: (0, 0))]
    args = [a, w, x, gate, ln_g.reshape(1, d), ln_b.reshape(1, d)]
    out_shape = [jax.ShapeDtypeStruct((r, d), F32)]
    out_specs = [pl.BlockSpec((tm, d), lambda i: (i, 0))]
    if emit_next:
        in_specs += [mod_spec(nxt[0]), mod_spec(nxt[1])]
        args += [nxt[0], nxt[1]]
        out_shape.append(jax.ShapeDtypeStruct((r, d), BF))
        out_specs.append(pl.BlockSpec((tm, d), lambda i: (i, 0)))
    res = pl.pallas_call(
        functools.partial(_proj_res_ln_kernel, alpha=alpha, emit_next=emit_next),
        out_shape=tuple(out_shape),
        grid=(r // tm,),
        in_specs=in_specs,
        out_specs=tuple(out_specs),
        scratch_shapes=[pltpu.VMEM((tm, d), F32)],
        compiler_params=_cparams(1, 56),
        name="proj_res_ln",
    )(*args)
    return res if emit_next else (res[0], None)


def _ffn_in_kernel(h_ref, wg_ref, wu_ref, o_ref):
    h = h_ref[...]
    o_ref[...] = (jax.nn.silu(_dot(h, wg_ref[...])) * _dot(h, wu_ref[...])).astype(BF)


def _ffn_in(h, w_in, layer, *, tm):
    r, d = h.shape
    d_ff = w_in.shape[2] // 2
    tn = 512
    n_j = d_ff // tn
    return pl.pallas_call(
        _ffn_in_kernel,
        out_shape=jax.ShapeDtypeStruct((r, d_ff), BF),
        grid=(r // tm, n_j),
        in_specs=[pl.BlockSpec((tm, d), lambda i, j: (i, 0)),
                  pl.BlockSpec((None, d, tn), lambda i, j: (layer, 0, j)),
                  pl.BlockSpec((None, d, tn), lambda i, j: (layer, 0, j + n_j))],
        out_specs=pl.BlockSpec((tm, tn), lambda i, j: (i, j)),
        compiler_params=_cparams(2, 40),
        name="ffn_in",
    )(h, w_in, w_in)


KV_SLABS = N_KV_HEADS * 2


def _kv_proj_kernel(y_ref, w_ref, f0_ref, f1_ref, f2_ref, b1_ref, b2_ref):
    xb = y_ref[...].astype(BF)
    tm = xb.shape[0]
    n = w_ref.shape[1] // 3
    for idx, (f_ref, b_ref) in enumerate(((f0_ref, None), (f1_ref, b1_ref), (f2_ref, b2_ref))):
        r = _dot(xb, w_ref[:, idx * n:(idx + 1) * n])
        for hk in range(KV_SLABS):
            f_ref[pl.ds(hk, tm, stride=KV_SLABS), :] = r[:, hk * HEAD_DIM:(hk + 1) * HEAD_DIM]
        if b_ref is not None:
            b_ref[...] = r.astype(BF)


def _kv_proj(y, w_kv, *, tm):
    r, d = y.shape
    n = w_kv.shape[1] // 3
    flat = jax.ShapeDtypeStruct((r * KV_SLABS, HEAD_DIM), F32)
    fspec = pl.BlockSpec((tm * KV_SLABS, HEAD_DIM), lambda i: (i, 0))
    wide = jax.ShapeDtypeStruct((r, n), BF)
    wspec = pl.BlockSpec((tm, n), lambda i: (i, 0))
    return pl.pallas_call(
        _kv_proj_kernel,
        out_shape=(flat, flat, flat, wide, wide),
        grid=(r // tm,),
        in_specs=[pl.BlockSpec((tm, d), lambda i: (i, 0)),
                  pl.BlockSpec(w_kv.shape, lambda i: (0, 0), pipeline_mode=pl.Buffered(1))],
        out_specs=(fspec, fspec, fspec, wspec, wspec),
        compiler_params=_cparams(1, 48),
        name="kv_proj",
    )(y, w_kv)


def _qg_proj_kernel(h_ref, wq_ref, wg_ref, q_ref, g_ref, *, gates_per_group):
    h = h_ref[...]
    q_ref[...] = _dot(h, wq_ref[...]).astype(BF)
    gates = jax.nn.sigmoid(_dot(h, wg_ref[...]))
    for grp in range(g_ref.shape[1] // LANES):
        shift = (LANES - grp * gates_per_group) % LANES
        g_ref[:, grp * LANES:(grp + 1) * LANES] = gates if shift == 0 else pltpu.roll(gates, shift, 1)


def _qg_proj(h, w_q, w_g, *, tm, gates_per_group):
    r, d = h.shape
    qw = w_q.shape[1]
    gw = N_KV_HEADS * LANES
    resident = lambda w: pl.BlockSpec(w.shape, lambda i: (0, 0), pipeline_mode=pl.Buffered(1))
    return pl.pallas_call(
        functools.partial(_qg_proj_kernel, gates_per_group=gates_per_group),
        out_shape=(jax.ShapeDtypeStruct((r, qw), BF), jax.ShapeDtypeStruct((r, gw), F32)),
        grid=(r // tm,),
        in_specs=[pl.BlockSpec((tm, d), lambda i: (i, 0)), resident(w_q), resident(w_g)],
        out_specs=(pl.BlockSpec((tm, qw), lambda i: (i, 0)),
                   pl.BlockSpec((tm, gw), lambda i: (i, 0))),
        compiler_params=_cparams(1, 40),
        name="qg_proj",
    )(h, w_q, w_g)


PAGES_PER_STEP = 16
CHUNKS_PER_PAGE = PAGE_SIZE // CMP_STRIDE


PAGE_FLAT_ROWS = PAGE_SIZE * KV_SLABS
CHUNK_FLAT_ROWS = CMP_STRIDE * KV_SLABS


def _compress_kernel(pt_ref, *refs, n_steps):
    page_refs = refs[:PAGES_PER_STEP]
    w1_ref, pe_ref, b1_ref, w2_ref, b2_ref, o_ref, p0_scr, p1_scr = refs[PAGES_PER_STEP:]
    t = pl.program_id(1)
    rows_step = PAGES_PER_STEP * CHUNKS_PER_PAGE * KV_SLABS
    n_rows = n_steps * rows_step
    hidden = w2_ref.shape[0]

    @pl.when(t == 0)
    def _():
        p1_scr[pl.ds(n_rows, KV_SLABS), :] = jnp.zeros((KV_SLABS, hidden), F32)

    per_s = []
    for s in range(CMP_STRIDE):
        rows = [page_refs[p][pl.ds(c * CHUNK_FLAT_ROWS + s * KV_SLABS, KV_SLABS), :]
                for p in range(PAGES_PER_STEP) for c in range(CHUNKS_PER_PAGE)]
        per_s.append(jnp.concatenate(rows, axis=0))
    x = jnp.concatenate(per_s, axis=1).astype(BF)
    pr = _dot(x, w1_ref[...])
    is_v = _mod_pow2(lax.broadcasted_iota(jnp.int32, (rows_step, 1), 0), 2) == 1
    pk = jnp.where(is_v, pr[:, 2 * hidden:], pr[:, :2 * hidden])
    row0 = pl.multiple_of(t * rows_step, rows_step)
    p0_scr[pl.ds(row0, rows_step), :] = pk[:, :hidden]
    p1_scr[pl.ds(row0, rows_step), :] = pk[:, hidden:]

    @pl.when(t == n_steps - 1)
    def _():
        cst = _dot(pe_ref[...].astype(BF), w1_ref[...])
        const_k = cst[0:1, :hidden] + cst[1:2, hidden:2 * hidden] + b1_ref[0:1, :]
        const_v = cst[2:3, 2 * hidden:3 * hidden] + cst[3:4, 3 * hidden:] + b1_ref[1:2, :]
        v_rows = _mod_pow2(lax.broadcasted_iota(jnp.int32, (n_rows, 1), 0), 2) == 1
        pre = p0_scr[0:n_rows, :] + p1_scr[KV_SLABS:n_rows + KV_SLABS, :] + jnp.where(v_rows, const_v, const_k)
        o2 = _dot(jax.nn.gelu(pre).astype(BF), w2_ref[...])
        o_ref[...] = jnp.where(v_rows, o2[:, HEAD_DIM:] + b2_ref[1:2, :], o2[:, :HEAD_DIM] + b2_ref[0:1, :])


def _compress(pages_flat, page_table, w1all, pe16, b1, w2all, b2):
    nb, n_pages = page_table.shape
    n_steps = n_pages // PAGES_PER_STEP
    n_rows = n_pages * CHUNKS_PER_PAGE * KV_SLABS
    hidden = w2all.shape[0]

    def page_spec(p):
        return pl.BlockSpec((PAGE_FLAT_ROWS, HEAD_DIM), lambda b, t, pt: (pt[b, t * PAGES_PER_STEP + p], 0))

    full = lambda a: pl.BlockSpec(a.shape, lambda b, t, pt: (0,) * a.ndim)
    grid_spec = pltpu.PrefetchScalarGridSpec(
        num_scalar_prefetch=1,
        grid=(nb, n_steps),
        in_specs=[page_spec(p) for p in range(PAGES_PER_STEP)] + [
            full(w1all), full(pe16), full(b1), full(w2all), full(b2)],
        out_specs=pl.BlockSpec((None, n_rows, HEAD_DIM), lambda b, t, pt: (b, 0, 0)),
        scratch_shapes=[pltpu.VMEM((n_rows, hidden), F32), pltpu.VMEM((n_rows + KV_SLABS, hidden), F32)],
    )
    return pl.pallas_call(
        functools.partial(_compress_kernel, n_steps=n_steps),
        out_shape=jax.ShapeDtypeStruct((nb, n_rows, HEAD_DIM), F32),
        grid_spec=grid_spec,
        compiler_params=_cparams(2, 48),
        name="compress_blocks",
    )(page_table, *([pages_flat] * PAGES_PER_STEP), w1all, pe16, b1, w2all, b2)


def _overlap_map(n_cmp_rows, n_slc, n_cols):
    cmp_lo = np.arange(n_cmp_rows)[:, None] * CMP_STRIDE
    slc_lo = np.arange(n_cols)[None, :] * SEL_BLOCK
    ov = np.minimum(cmp_lo + CMP_LEN, slc_lo + SEL_BLOCK) - np.maximum(cmp_lo, slc_lo)
    w = np.clip(ov, 0, None).astype(np.float32) / CMP_LEN
    w[:, n_slc:] = 0.0
    return w


SEL_CHUNK = 512


def _attn_prompt_kernel(q_ref, gt_ref, cmp_ref, sel_ref, win_ref, wmap_t_ref, exp_ref, o_ref,
                        s_sel, s_win, m_run, l_run, e_scr, o_sel_scr, *, scale, n_slc, n_qb, n_widths):
    g = pl.program_id(1)
    qb = pl.program_id(2)
    q4 = q_ref[...]
    nq = q4.shape[0]
    rq = q4.shape[1] // HEAD_DIM
    qs = jnp.concatenate([q4[:, r * HEAD_DIM:(r + 1) * HEAD_DIM] for r in range(rq)], axis=0)
    qpos = qb * nq + lax.broadcasted_iota(jnp.int32, (nq, 1), 0)
    rows = [slice(r * nq, (r + 1) * nq) for r in range(rq)]

    head_groups = [rows[:rq // 2], rows[rq // 2:]] if rq % 2 == 0 else [rows]

    def attend(n_chunks, chunk, s_scr, keys, mask, values):
        width = n_chunks * chunk
        m_run[...] = jnp.full(m_run.shape, -jnp.inf, F32)
        l_run[...] = jnp.zeros(l_run.shape, F32)
        for grp in head_groups:
            g_rows = slice(grp[0].start, grp[-1].stop)
            for c in range(n_chunks):
                s = _dot_t(qs[g_rows], keys(c))
                ok = mask(c)
                for i, rs in enumerate(grp):
                    sm = jnp.where(ok, s[i * nq:(i + 1) * nq], -jnp.inf) * (scale * LOG2_E)
                    s_scr[c, rs, :] = sm
                    mx = m_run[rs, :]
                    for j in range(chunk // LANES):
                        mx = jnp.maximum(mx, sm[:, j * LANES:(j + 1) * LANES])
                    m_run[rs, :] = mx
        outs = []
        for grp in head_groups:
            g_rows = slice(grp[0].start, grp[-1].stop)
            m = jnp.max(m_run[g_rows, :], axis=-1, keepdims=True)
            m = jnp.where(m == -jnp.inf, 0.0, m)
            for c in range(n_chunks):
                for i, rs in enumerate(grp):
                    e = jnp.exp2(s_scr[c, rs, :] - m[i * nq:(i + 1) * nq])
                    ls = l_run[rs, :]
                    for j in range(chunk // LANES):
                        ls = ls + e[:, j * LANES:(j + 1) * LANES]
                    l_run[rs, :] = ls
                    e_scr[rs, c * chunk:(c + 1) * chunk] = e.astype(BF)
            l = jnp.sum(l_run[g_rows, :], axis=-1, keepdims=True)
            outs.append(_dot(e_scr[g_rows, :width], values) * (1.0 / jnp.maximum(l, 1e-30)))
        return jnp.concatenate(outs, axis=0)

    n_cmp_rows = cmp_ref.shape[0] // KV_SLABS
    kc = cmp_ref[pl.ds(g * 2, n_cmp_rows, stride=KV_SLABS), :].astype(BF)
    vc = cmp_ref[pl.ds(g * 2 + 1, n_cmp_rows, stride=KV_SLABS), :].astype(BF)
    n_io = lax.broadcasted_iota(jnp.int32, (1, n_cmp_rows), 1)
    valid_c = (n_io * CMP_STRIDE + (CMP_LEN - 1)) <= qpos
    s_c = _dot_t(qs, kc) * scale
    ps = [_masked_softmax(s_c[rs], valid_c).astype(BF) for rs in rows]
    o_cmp = _dot(jnp.concatenate(ps, axis=0), vc)

    imp = _dot_t(wmap_t_ref[...], jnp.concatenate(ps, axis=1))
    n_rows = imp.shape[0]
    j_io = lax.broadcasted_iota(jnp.int32, (n_rows, 1), 0)
    q_blk = _div_pow2(qb * nq + lax.broadcasted_iota(jnp.int32, (1, nq), 1), SEL_BLOCK)
    causal = j_io <= q_blk
    forced = (j_io == 0) | (causal & (j_io > q_blk - N_LOCAL_SEL))
    imp = jnp.where(forced, imp + FORCE_BONUS, imp)
    imp = jnp.where(causal, imp, -jnp.inf)
    rank = jnp.zeros(imp.shape, F32)
    for jp in range(n_slc):
        other = imp[jp:jp + 1, :]
        rank = rank + jnp.where(other == imp, jnp.where(jp < j_io, 1.0, 0.0), jnp.where(other > imp, 1.0, 0.0))
    chosen_t = jnp.where(rank < min(N_SEL, n_slc), jnp.where(imp > -jnp.inf, 1.0, 0.0), 0.0)
    chosen_t = jnp.concatenate([chosen_t, jnp.zeros((LANES - n_rows, nq), F32)], axis=0)
    chosen = chosen_t.T.astype(BF)

    span = WINDOW + nq
    k0 = pl.multiple_of(jnp.maximum(qb * nq - WINDOW, 0), nq)
    kp = k0 + lax.broadcasted_iota(jnp.int32, (1, span), 1)
    valid_w = (kp <= qpos) & (kp > qpos - WINDOW)
    o_win = attend(1, span, s_win, lambda c: win_ref[pl.ds(k0, span), :HEAD_DIM], lambda c: valid_w,
                   win_ref[pl.ds(k0, span), HEAD_DIM:])

    def sel_rows(c):
        return slice(c * SEL_CHUNK, (c + 1) * SEL_CHUNK)

    def sel_mask(c):
        t_io = c * SEL_CHUNK + lax.broadcasted_iota(jnp.int32, (1, SEL_CHUNK), 1)
        return (_dot(chosen, exp_ref[c]) > 0.5) & (t_io <= qpos)

    qb_per_width = n_qb // n_widths
    for wi in range(n_widths):
        n_sel_chunks = -(-(wi + 1) * qb_per_width * nq // SEL_CHUNK)

        @pl.when((qb >= wi * qb_per_width) & (qb < (wi + 1) * qb_per_width))
        def _(n_sel_chunks=n_sel_chunks):
            o_sel_scr[...] = attend(n_sel_chunks, SEL_CHUNK, s_sel,
                                    lambda c: sel_ref[sel_rows(c), :HEAD_DIM], sel_mask,
                                    sel_ref[:n_sel_chunks * SEL_CHUNK, HEAD_DIM:])

    o_sel = o_sel_scr[...]
    gt = gt_ref[...]
    outs = []
    for r, rs in enumerate(rows):
        outs.append(gt[:, 3 * r:3 * r + 1] * o_cmp[rs] + gt[:, 3 * r + 1:3 * r + 2] * o_sel[rs]
                    + gt[:, 3 * r + 2:3 * r + 3] * o_win[rs])
    o_ref[...] = jnp.concatenate(outs, axis=1).astype(BF)


def _attn_prompt(q, gates, cmp_flat, kv_sel, kv_win, *, scale):
    nb, t, qw = q.shape
    n_g = N_KV_HEADS
    gw = qw // n_g
    kvw = 2 * HEAD_DIM
    cmp_rows = cmp_flat.shape[1]
    n_slc = t // SEL_BLOCK
    n_slc_rows = -(-n_slc // 8) * 8
    rq = gw // HEAD_DIM
    n_qb = t // Q_BLOCK
    n_chunks = t // SEL_CHUNK
    wmap_t = jnp.asarray(np.tile(_overlap_map(cmp_rows // KV_SLABS, n_slc, n_slc_rows), (rq, 1)).T, BF)
    key_blk = (np.arange(t) // SEL_BLOCK).reshape(n_chunks, 1, SEL_CHUNK)
    expand = jnp.asarray((key_blk == np.arange(LANES)[None, :, None]).astype(np.float32), BF)
    rows = rq * Q_BLOCK
    n_widths = 4 if n_qb % 4 == 0 else 1
    return pl.pallas_call(
        functools.partial(_attn_prompt_kernel, scale=scale, n_slc=n_slc, n_qb=n_qb, n_widths=n_widths),
        out_shape=jax.ShapeDtypeStruct((nb, t, qw), BF),
        grid=(nb, n_g, n_qb),
        in_specs=[pl.BlockSpec((None, Q_BLOCK, gw), lambda b, g, i: (b, i, g)),
                  pl.BlockSpec((None, Q_BLOCK, LANES), lambda b, g, i: (b, i, g)),
                  pl.BlockSpec((None, cmp_rows, HEAD_DIM), lambda b, g, i: (b, 0, 0)),
                  pl.BlockSpec((None, t, kvw), lambda b, g, i: (b, 0, g)),
                  pl.BlockSpec((None, t, kvw), lambda b, g, i: (b, 0, g)),
                  pl.BlockSpec(wmap_t.shape, lambda b, g, i: (0, 0)),
                  pl.BlockSpec(expand.shape, lambda b, g, i: (0, 0, 0))],
        out_specs=pl.BlockSpec((None, Q_BLOCK, gw), lambda b, g, i: (b, i, g)),
        scratch_shapes=[pltpu.VMEM((n_chunks, rows, SEL_CHUNK), F32),
                        pltpu.VMEM((1, rows, WINDOW + Q_BLOCK), F32),
                        pltpu.VMEM((rows, LANES), F32), pltpu.VMEM((rows, LANES), F32),
                        pltpu.VMEM((rows, max(t, WINDOW + Q_BLOCK)), BF), pltpu.VMEM((rows, HEAD_DIM), F32)],
        compiler_params=_cparams(3, 48),
        name="attn_prompt",
    )(q, gates, cmp_flat, kv_sel, kv_win, wmap_t, expand)


def _attn_sample_a_kernel(q_ref, cmp_ref, win_ref, wmap_ref, ocmp_ref, owin_ref, idx_ref, val_ref, *,
                          scale, q_pos, n_cmp, n_slc, win_pos0):
    q = q_ref[...]
    n_h = q.shape[0]
    rq = n_h // N_KV_HEADS
    row_g = _div_pow2(lax.broadcasted_iota(jnp.int32, (n_h, 1), 0), rq)
    cmp_rows = cmp_ref.shape[0] // KV_SLABS
    win_rows = win_ref.shape[0] // KV_SLABS
    n_io = lax.broadcasted_iota(jnp.int32, (1, cmp_rows), 1)
    valid_c = (n_io < n_cmp) & ((n_io * CMP_STRIDE + (CMP_LEN - 1)) <= q_pos)
    kp = win_pos0 + lax.broadcasted_iota(jnp.int32, (1, win_rows), 1)
    valid_w = (kp <= q_pos) & (kp > q_pos - WINDOW)
    o_cmp = jnp.zeros((n_h, HEAD_DIM), F32)
    o_win = jnp.zeros((n_h, HEAD_DIM), F32)
    imp = jnp.zeros((n_h, wmap_ref.shape[1]), F32)
    for g in range(N_KV_HEADS):
        in_g = row_g == g
        kc = cmp_ref[pl.ds(2 * g, cmp_rows, stride=KV_SLABS), :].astype(BF)
        vc = cmp_ref[pl.ds(2 * g + 1, cmp_rows, stride=KV_SLABS), :].astype(BF)
        p = _masked_softmax(_dot_t(q, kc) * scale, valid_c).astype(BF)
        o_cmp = jnp.where(in_g, _dot(p, vc), o_cmp)
        imp_rows = _dot(p, wmap_ref[...])
        imp_g = jnp.sum(jnp.where(in_g, imp_rows, 0.0), axis=0, keepdims=True)
        imp = jnp.where(in_g, imp_g, imp)
        kw = win_ref[pl.ds(2 * g, win_rows, stride=KV_SLABS), :].astype(BF)
        vw = win_ref[pl.ds(2 * g + 1, win_rows, stride=KV_SLABS), :].astype(BF)
        pw = _masked_softmax(_dot_t(q, kw) * scale, valid_w).astype(BF)
        o_win = jnp.where(in_g, _dot(pw, vw), o_win)
    ocmp_ref[...] = o_cmp
    owin_ref[...] = o_win

    j_io = lax.broadcasted_iota(jnp.int32, (1, imp.shape[1]), 1)
    q_blk = q_pos // SEL_BLOCK
    causal = j_io <= q_blk
    forced = (j_io == 0) | (causal & (j_io > q_blk - N_LOCAL_SEL))
    imp = jnp.where(forced, imp + FORCE_BONUS, imp)
    imp = jnp.where(causal, imp, -jnp.inf)
    rank = _rank_desc(imp, n_slc)
    jf = j_io.astype(F32)
    k_io = lax.broadcasted_iota(jnp.int32, (1, idx_ref.shape[1]), 1)
    idx = jnp.zeros(idx_ref.shape, F32)
    val = jnp.zeros(idx_ref.shape, F32)
    for k in range(min(N_SEL, n_slc)):
        hit = rank == k
        ik = jnp.sum(jnp.where(hit, jf, 0.0), axis=1, keepdims=True)
        vk = jnp.sum(jnp.where(hit, jnp.where(imp > -jnp.inf, 1.0, 0.0), 0.0), axis=1, keepdims=True)
        idx = jnp.where(k_io == k, ik, idx)
        val = jnp.where(k_io == k, vk, val)
    idx_ref[...] = idx.astype(jnp.int32)
    val_ref[...] = val.astype(jnp.int32)


def _attn_sample_a(q, cmp_blocks, win_rows, *, scale, q_pos, n_cmp, n_slc, win_pos0):
    nb, n_h, d = q.shape
    n_cols = -(-n_slc // LANES) * LANES
    wmap = jnp.asarray(_overlap_map(cmp_blocks.shape[1] // KV_SLABS, n_slc, n_cols), BF)
    o_sds = jax.ShapeDtypeStruct((nb, n_h, d), F32)
    i_sds = jax.ShapeDtypeStruct((nb, n_h, LANES), jnp.int32)
    blk = lambda a: pl.BlockSpec((None,) + a.shape[1:], lambda b: (b, 0, 0))
    ospec = pl.BlockSpec((None, n_h, d), lambda b: (b, 0, 0))
    ispec = pl.BlockSpec((None, n_h, LANES), lambda b: (b, 0, 0))
    return pl.pallas_call(
        functools.partial(_attn_sample_a_kernel, scale=scale, q_pos=q_pos, n_cmp=n_cmp, n_slc=n_slc,
                          win_pos0=win_pos0),
        out_shape=(o_sds, o_sds, i_sds, i_sds),
        grid=(nb,),
        in_specs=[blk(q), blk(cmp_blocks), blk(win_rows), pl.BlockSpec(wmap.shape, lambda b: (0, 0))],
        out_specs=(ospec, ospec, ispec, ispec),
        compiler_params=_cparams(1, 32),
        name="attn_sample_a",
    )(q, cmp_blocks, win_rows, wmap)


def _attn_sample_b_kernel(idx_ref, val_ref, pt_ref, q_ref, new_ref, gt_ref, ocmp_ref, owin_ref, *rest,
                          scale, q_pos, past_len):
    blk_refs = rest[:N_SEL]
    o_ref = rest[N_SEL]
    b = pl.program_id(0)
    g = pl.program_id(1)
    q = q_ref[...]
    n_h = q.shape[0]
    rq = n_h // N_KV_HEADS
    base = (b * N_KV_HEADS + g) * N_SEL
    n_keys = N_SEL * SEL_BLOCK
    lane = lax.broadcasted_iota(jnp.int32, (1, n_keys), 1)
    k_new = new_ref[pl.ds(2 * g, 1), :]
    v_new = new_ref[pl.ds(2 * g + 1, 1), :]
    ks, vs = [], []
    kpos = _mod_pow2(lane, SEL_BLOCK)
    kval = jnp.zeros((1, n_keys), jnp.int32)
    lane_blk = _div_pow2(lane, SEL_BLOCK)
    for k in range(N_SEL):
        blk = idx_ref[base + k]
        is_new = blk * SEL_BLOCK >= past_len
        ks.append(jnp.where(is_new, k_new, blk_refs[k][pl.ds(2 * g, SEL_BLOCK, stride=KV_SLABS), :]).astype(BF))
        vs.append(jnp.where(is_new, v_new, blk_refs[k][pl.ds(2 * g + 1, SEL_BLOCK, stride=KV_SLABS), :]).astype(BF))
        in_k = lane_blk == k
        kpos = kpos + jnp.where(in_k, blk * SEL_BLOCK, 0)
        kval = jnp.where(in_k, val_ref[base + k], kval)
    valid = (kval > 0) & (kpos <= q_pos)
    p = _masked_softmax(_dot_t(q, jnp.concatenate(ks, axis=0)) * scale, valid).astype(BF)
    o_sel = _dot(p, jnp.concatenate(vs, axis=0))
    gt = gt_ref[...]
    in_g = _div_pow2(lax.broadcasted_iota(jnp.int32, (n_h, 1), 0), rq) == g

    @pl.when(g == 0)
    def _():
        o_ref[...] = gt[:, 0:1] * ocmp_ref[...] + gt[:, 2:3] * owin_ref[...]

    o_ref[...] += jnp.where(in_g, gt[:, 1:2] * o_sel, 0.0)


def _attn_sample_b(sel_idx, sel_val, page_table, q, kv_sel_new, gates, o_cmp, o_win, pool_sel, *,
                   scale, q_pos, past_len):
    nb, n_h, d = q.shape
    halves = PAGE_SIZE // SEL_BLOCK
    blk_rows = SEL_BLOCK * KV_SLABS
    pool = pool_sel.reshape(pool_sel.shape[0] // blk_rows, blk_rows, d)

    def blk_spec(k):
        def index_map(b, g, idx, val, pt):
            row0 = jnp.minimum(idx[(b * N_KV_HEADS + g) * N_SEL + k] * SEL_BLOCK, past_len - 1)
            return (pt[b, row0 // PAGE_SIZE] * halves + (row0 % PAGE_SIZE) // SEL_BLOCK, 0, 0)
        return pl.BlockSpec((None, blk_rows, d), index_map)

    per_b = lambda a: pl.BlockSpec((None,) + a.shape[1:], lambda b, g, idx, val, pt: (b, 0, 0))
    new_rows = kv_sel_new.reshape(nb, KV_SLABS, d)
    grid_spec = pltpu.PrefetchScalarGridSpec(
        num_scalar_prefetch=3,
        grid=(nb, N_KV_HEADS),
        in_specs=[per_b(q), per_b(new_rows), per_b(gates), per_b(o_cmp), per_b(o_win)]
        + [blk_spec(k) for k in range(N_SEL)],
        out_specs=pl.BlockSpec((None, n_h, d), lambda b, g, idx, val, pt: (b, 0, 0)),
    )
    return pl.pallas_call(
        functools.partial(_attn_sample_b_kernel, scale=scale, q_pos=q_pos, past_len=past_len),
        out_shape=jax.ShapeDtypeStruct((nb, n_h, d), F32),
        grid_spec=grid_spec,
        compiler_params=_cparams(2, 32),
        name="attn_sample_b",
    )(sel_idx, sel_val, page_table, q, new_rows, gates, o_cmp, o_win, *([pool] * N_SEL))


def kernel(x_prompt, x_sample, cache_cmp_kv, cache_sel_kv, cache_win_kv, page_table, c_prompt, c_sample, w_ada, b_ada, ln_g, ln_b, a_w_in, a_v_g, a_v_b, a_w_s, a_b_s, a_w_out, w_kv, cmp_pe, cmp_w1, cmp_b1, cmp_w2, cmp_b2, b_w_qg, b_w_o, ffn_w_in, ffn_w_out):
    nb, t, d = x_prompt.shape
    ns, tq, _ = x_sample.shape
    assert tq == 1, "the sample path handles one new token per sequence"
    depth = w_ada.shape[0]
    n_a = a_w_in.shape[0]
    alpha = float((2 * depth) ** 0.25)
    scale = float(HEAD_DIM ** -0.5)
    n_pool = cache_cmp_kv.shape[0]
    past_len = page_table.shape[1] * PAGE_SIZE
    kvw = N_KV_HEADS * 2 * HEAD_DIM
    q_width = b_w_o.shape[1]
    n_heads = q_width // HEAD_DIM
    rq = n_heads // N_KV_HEADS

    mod = _ada_modulation(jnp.concatenate([c_prompt, c_sample], axis=0), w_ada, b_ada)

    def mods(layer, sub):
        m = mod[layer * 2 + sub]
        out = []
        for part in range(3):
            sl = m[:, part * d:(part + 1) * d]
            out.append((sl[:nb].reshape(nb, 1, d), sl[nb:].reshape(1, ns, d)))
        return out

    wb = lambda w: w.astype(BF)
    hidden = cmp_w1.shape[-1]
    halves = CMP_LEN // CMP_STRIDE
    assert halves == 2, "a compression block spans two stride-chunks"
    w1all = wb(cmp_w1.reshape(2, halves, CMP_STRIDE, HEAD_DIM, hidden)
               .transpose(2, 3, 0, 1, 4).reshape(CMP_STRIDE * HEAD_DIM, 2 * halves * hidden))
    pe16 = jnp.pad(cmp_pe.reshape(2 * halves, CMP_STRIDE * HEAD_DIM), ((0, 16 - 2 * halves), (0, 0)))
    w2all = wb(cmp_w2.transpose(1, 0, 2).reshape(hidden, 2 * HEAD_DIM))

    a_w_out_b, b_w_o_b, ffn_w_in_b, ffn_w_out_b = wb(a_w_out), wb(b_w_o), wb(ffn_w_in), wb(ffn_w_out)

    tm_p = 512
    tpg_p = t // tm_p
    prompt = dict(tm=tm_p, tiles_per_group=tpg_p)
    sample = dict(tm=ns, tiles_per_group=1)

    xp = x_prompt.reshape(nb * t, d)
    xs = x_sample.reshape(ns, d)
    hp = hs = None
    chunk_v = []
    outs = {}
    for layer in range(depth):
        (sh_p, sh_s), (sc_p, sc_s), (gt_p, gt_s) = mods(layer, 0)
        (fsh_p, fsh_s), (fsc_p, fsc_s), (fgt_p, fgt_s) = mods(layer, 1)
        if layer < n_a:
            w_in = wb(a_w_in[layer])
            ap = _gmlp_in(xp, sc_p, sh_p, w_in, a_v_g[layer], a_v_b[layer], a_w_s[layer], a_b_s[layer],
                          chunked=True, **prompt)
            as_, v_rows = _gmlp_in(xs, sc_s, sh_s, w_in, a_v_g[layer], a_v_b[layer], a_w_s[layer], a_b_s[layer],
                                   chunked=False, **sample)
            chunk_v.append(v_rows.reshape(ns, tq, -1))
            xp, hp = _proj_res_ln(ap, a_w_out_b, xp, gt_p, ln_g[layer, 0], ln_b[layer, 0], (fsc_p, fsh_p),
                                  alpha=alpha, layer=layer, **prompt)
            xs, hs = _proj_res_ln(as_, a_w_out_b, xs, gt_s, ln_g[layer, 0], ln_b[layer, 0], (fsc_s, fsh_s),
                                  alpha=alpha, layer=layer, **sample)
        else:
            if layer == n_a:
                w_kvb = wb(w_kv)
                cmp_p, sel_p, win_p, sel_pb, win_pb = _kv_proj(xp, w_kvb, tm=tm_p)
                cmp_s, sel_s, win_s, _, _ = _kv_proj(xs, w_kvb, tm=ns)
                pages_per_seq_p = t // PAGE_SIZE
                ident = jnp.arange(nb * pages_per_seq_p, dtype=jnp.int32).reshape(nb, pages_per_seq_p)
                cmp_blocks_p = _compress(cmp_p, ident, w1all, pe16, cmp_b1, w2all, cmp_b2)
                assert (past_len + tq) // CMP_STRIDE == past_len // CMP_STRIDE
                cmp_blocks_s = _compress(cache_cmp_kv.reshape(n_pool * PAGE_FLAT_ROWS, HEAD_DIM), page_table,
                                         w1all, pe16, cmp_b1, w2all, cmp_b2)
                n_win = cache_win_kv.shape[1]
                win_keys_s = jnp.concatenate([cache_win_kv.reshape(ns, n_win * KV_SLABS, HEAD_DIM),
                                              win_s.reshape(ns, tq * KV_SLABS, HEAD_DIM)],
                                             axis=1)[:, -n_win * KV_SLABS:]
                outs["kv_p"] = (cmp_p, sel_p, win_p)
                outs["kv_pb"] = (sel_pb.reshape(nb, t, kvw), win_pb.reshape(nb, t, kvw))
                outs["kv_s"] = (cmp_s, sel_s, win_keys_s)
            j = layer - n_a
            w_qg = b_w_qg[j]
            w_q = wb(w_qg[:, :q_width])
            n_gates = w_qg.shape[1] - q_width
            w_gate = wb(jnp.pad(w_qg[:, q_width:], ((0, 0), (0, LANES - n_gates))))
            assert layer > 0, "an attention layer must follow another layer"
            q_p, g_p = _qg_proj(hp, w_q, w_gate, tm=tm_p, gates_per_group=rq * 3)
            q_s, g_s = _qg_proj(hs, w_q, w_gate, tm=ns, gates_per_group=rq * 3)
            o_p = _attn_prompt(q_p.reshape(nb, t, q_width), g_p.reshape(nb, t, N_KV_HEADS * LANES),
                               cmp_blocks_p, outs["kv_pb"][0], outs["kv_pb"][1], scale=scale)
            n_cmp_s = (past_len + tq) // CMP_STRIDE - CMP_LEN // CMP_STRIDE + 1
            n_slc_s = -(-(past_len + tq) // SEL_BLOCK)
            q_s3 = q_s.reshape(ns, n_heads, HEAD_DIM)
            o_cmp_s, o_win_s, idx_s, val_s = _attn_sample_a(
                q_s3, cmp_blocks_s, outs["kv_s"][2], scale=scale, q_pos=past_len, n_cmp=n_cmp_s, n_slc=n_slc_s,
                win_pos0=past_len + tq - outs["kv_s"][2].shape[1] // KV_SLABS)
            sel_idx = idx_s[:, ::rq, :N_SEL].reshape(-1)
            sel_val = val_s[:, ::rq, :N_SEL].reshape(-1)
            gates_s = g_s.reshape(ns, N_KV_HEADS, LANES)[:, :, :rq * 3].reshape(ns, n_heads, 3)
            o_s = _attn_sample_b(sel_idx, sel_val, page_table, q_s3, outs["kv_s"][1], gates_s, o_cmp_s, o_win_s,
                                 cache_sel_kv.reshape(n_pool * PAGE_FLAT_ROWS, HEAD_DIM), scale=scale,
                                 q_pos=past_len, past_len=past_len)
            xp, hp = _proj_res_ln(o_p.reshape(nb * t, q_width), b_w_o_b, xp, gt_p, ln_g[layer, 0], ln_b[layer, 0],
                                  (fsc_p, fsh_p), alpha=alpha, layer=j, **prompt)
            xs, hs = _proj_res_ln(o_s.reshape(ns, q_width), b_w_o_b, xs, gt_s, ln_g[layer, 0], ln_b[layer, 0],
                                  (fsc_s, fsh_s), alpha=alpha, layer=j, **sample)
        fp = _ffn_in(hp, ffn_w_in_b, layer, tm=2 * tm_p)
        fs = _ffn_in(hs, ffn_w_in_b, layer, tm=ns)
        nxt_p = nxt_s = None
        if n_a <= layer + 1 < depth:
            (nsh_p, nsh_s), (nsc_p, nsc_s), _ = mods(layer + 1, 0)
            nxt_p, nxt_s = (nsc_p, nsh_p), (nsc_s, nsh_s)
        tm_f = tm_p // 2
        xp, hp = _proj_res_ln(fp, ffn_w_out_b, xp, fgt_p, ln_g[layer, 1], ln_b[layer, 1], nxt_p, alpha=alpha,
                              layer=layer, tm=tm_f, tiles_per_group=t // tm_f)
        xs, hs = _proj_res_ln(fs, ffn_w_out_b, xs, fgt_s, ln_g[layer, 1], ln_b[layer, 1], nxt_s, alpha=alpha,
                              layer=layer, **sample)

    cmp_p, sel_p, win_p = outs["kv_p"]
    cmp_s, sel_s, win_keys_s = outs["kv_s"]
    kv_shape = (N_KV_HEADS, 2, HEAD_DIM)
    n_win_p = min(WINDOW, t)
    return (xp.reshape(nb, t, d), xs.reshape(ns, tq, d),
            cmp_p.reshape((nb, t) + kv_shape), sel_p.reshape((nb, t) + kv_shape),
            win_p.reshape((nb, t) + kv_shape)[:, -n_win_p:],
            cmp_s.reshape((ns, tq) + kv_shape), sel_s.reshape((ns, tq) + kv_shape),
            win_keys_s.reshape((ns, -1) + kv_shape),
            jnp.stack(chunk_v))
```

```python
import functools

import numpy as np
import jax
import jax.numpy as jnp
from jax import lax
from jax.experimental import pallas as pl
from jax.experimental.pallas import tpu as pltpu

BF = jnp.bfloat16
F32 = jnp.float32

LN_EPS = 1e-5
CHUNK = 128
HEAD_DIM = 128
N_KV_HEADS = 4
CMP_LEN = 32
CMP_STRIDE = 16
SEL_BLOCK = 64
N_SEL = 16
N_LOCAL_SEL = 2
FORCE_BONUS = 1e4
LOG2_E = 1.4426950408889634
WINDOW = 512
Q_BLOCK = 128
PAGE_SIZE = 128

LANES = 128
MIB = 1024 * 1024


def _cparams(n_axes, vmem_mib):
    return pltpu.CompilerParams(dimension_semantics=("arbitrary",) * n_axes,
                                vmem_limit_bytes=vmem_mib * MIB)


def _dot(a, b):
    return jnp.dot(a, b, preferred_element_type=F32)


def _dot_t(a, b):
    return lax.dot_general(a, b, (((1,), (1,)), ((), ())), preferred_element_type=F32)


def _layer_norm(x, g, b):
    mu = jnp.mean(x, axis=-1, keepdims=True)
    xc = x - mu
    var = jnp.mean(xc * xc, axis=-1, keepdims=True)
    return xc * lax.rsqrt(var + LN_EPS) * g + b


def _masked_softmax(s, valid):
    s = jnp.where(valid, s, -jnp.inf)
    m = jnp.max(s, axis=-1, keepdims=True)
    m = jnp.where(m == -jnp.inf, 0.0, m)
    e = jnp.exp(s - m)
    d = jnp.maximum(jnp.sum(e, axis=-1, keepdims=True), 1e-30)
    return e * (1.0 / d)


def _log2(n):
    assert n & (n - 1) == 0, "power of two expected"
    return n.bit_length() - 1


def _div_pow2(x, n):
    return lax.shift_right_logical(x, jnp.int32(_log2(n)))


def _mod_pow2(x, n):
    assert n & (n - 1) == 0, "power of two expected"
    return lax.bitwise_and(x, jnp.int32(n - 1))


def _rank_desc(imp, n):
    j_io = lax.broadcasted_iota(jnp.int32, (1, imp.shape[1]), 1)
    rank = jnp.zeros(imp.shape, F32)
    for jp in range(n):
        col = imp[:, jp:jp + 1]
        before = jnp.where(col == imp, jnp.where(jp < j_io, 1.0, 0.0), jnp.where(col > imp, 1.0, 0.0))
        rank = rank + before
    return rank


def _ada_kernel(c_ref, w_ref, b_ref, o_ref):
    a = jax.nn.silu(c_ref[...]).astype(BF)
    o_ref[...] = _dot(a, w_ref[...].astype(BF)) + b_ref[...]


def _ada_modulation(c_all, w_ada, b_ada):
    n_sub = w_ada.shape[0] * w_ada.shape[1]
    d, n3 = w_ada.shape[2], w_ada.shape[3]
    m = c_all.shape[0]
    tn = 512
    w = w_ada.reshape(n_sub, d, n3)
    b = b_ada.reshape(n_sub, 1, n3)
    return pl.pallas_call(
        _ada_kernel,
        out_shape=jax.ShapeDtypeStruct((n_sub, m, n3), F32),
        grid=(n_sub, n3 // tn),
        in_specs=[pl.BlockSpec((m, d), lambda s, j: (0, 0)),
                  pl.BlockSpec((None, d, tn), lambda s, j: (s, 0, j)),
                  pl.BlockSpec((None, 1, tn), lambda s, j: (s, 0, j))],
        out_specs=pl.BlockSpec((None, m, tn), lambda s, j: (s, 0, j)),
        compiler_params=_cparams(2, 32),
        name="ada_modulation",
    )(c_all, w, b)


def _mod_spec(mod, tiles_per_group):
    rb, d = mod.shape[1], mod.shape[2]
    return pl.BlockSpec((None, rb, d), lambda i: (i // tiles_per_group, 0, 0))


def _gmlp_in_kernel(x_ref, sc_ref, sh_ref, w_ref, vg_ref, vb_ref, ws_ref, bs_ref, *rest,
                    n_j, tn, chunked):
    if chunked:
        o_ref, z_scr = rest
    else:
        o_ref, v_ref, z_scr = rest
    h = (x_ref[...] * (1.0 + sc_ref[...]) + sh_ref[...]).astype(BF)
    for jj in range(n_j):
        z_scr[jj] = jax.nn.gelu(_dot(h, w_ref[:, jj * tn:(jj + 1) * tn]))

    tm = z_scr.shape[1]
    half = n_j // 2
    e_a = half * tn
    s1 = jnp.zeros((tm, 1), F32)
    for jj in range(half):
        s1 = s1 + jnp.sum(z_scr[half + jj], axis=-1, keepdims=True)
    mu = s1 / e_a
    s2 = jnp.zeros((tm, 1), F32)
    for jj in range(half):
        vc = z_scr[half + jj] - mu
        s2 = s2 + jnp.sum(vc * vc, axis=-1, keepdims=True)
    rstd = lax.rsqrt(s2 / e_a + LN_EPS)
    gpt = tn // CHUNK
    if chunked:
        row = lax.broadcasted_iota(jnp.int32, (CHUNK, CHUNK), 0)
        col = lax.broadcasted_iota(jnp.int32, (CHUNK, CHUNK), 1)
        for jj in range(half):
            for gi in range(gpt):
                g = jj * gpt + gi
                cs = slice(gi * CHUNK, (gi + 1) * CHUNK)
                wm = jnp.where(row >= col, ws_ref[g], 0.0).astype(BF)
                bcol = bs_ref[:, g:g + 1]
                vg = vg_ref[:, g * CHUNK:(g + 1) * CHUNK]
                vb = vb_ref[:, g * CHUNK:(g + 1) * CHUNK]
                for c in range(tm // CHUNK):
                    rs = slice(c * CHUNK, (c + 1) * CHUNK)
                    vn = (z_scr[half + jj, rs, cs] - mu[rs]) * rstd[rs] * vg + vb
                    mixed = _dot(wm, vn.astype(BF)) + bcol
                    o_ref[rs, g * CHUNK:(g + 1) * CHUNK] = (z_scr[jj, rs, cs] * mixed).astype(BF)
    else:
        for jj in range(half):
            cs = slice(jj * tn, (jj + 1) * tn)
            vn = (z_scr[half + jj] - mu) * rstd * vg_ref[:, cs] + vb_ref[:, cs]
            v_ref[:, cs] = vn
            mixed = ws_ref[:, cs] * vn + bs_ref[:, cs]
            o_ref[:, cs] = (z_scr[jj] * mixed).astype(BF)


def _gmlp_in(x, sc, sh, w_in, v_g, v_b, w_s, b_s, *, tm, tiles_per_group, chunked):
    r, d = x.shape
    n2 = w_in.shape[1]
    e_a = n2 // 2
    tn = 512
    n_j = n2 // tn
    n_g = w_s.shape[0]
    full = lambda a: pl.BlockSpec(a.shape, lambda i: (0,) * a.ndim)
    row_tile = pl.BlockSpec((tm, e_a), lambda i: (i, 0))
    if chunked:
        ws_arr = w_s
        bs_arr = b_s.T
        out_shape = jax.ShapeDtypeStruct((r, e_a), BF)
        out_specs = row_tile
    else:
        ws_arr = jnp.repeat(w_s[:, 0, 0], e_a // n_g).reshape(1, e_a)
        bs_arr = jnp.repeat(b_s[:, 0], e_a // n_g).reshape(1, e_a)
        out_shape = (jax.ShapeDtypeStruct((r, e_a), BF), jax.ShapeDtypeStruct((r, e_a), F32))
        out_specs = (row_tile, row_tile)
    vg2, vb2 = v_g.reshape(1, e_a), v_b.reshape(1, e_a)
    return pl.pallas_call(
        functools.partial(_gmlp_in_kernel, n_j=n_j, tn=tn, chunked=chunked),
        out_shape=out_shape,
        grid=(r // tm,),
        in_specs=[pl.BlockSpec((tm, d), lambda i: (i, 0)),
                  _mod_spec(sc, tiles_per_group), _mod_spec(sh, tiles_per_group),
                  pl.BlockSpec(w_in.shape, lambda i: (0, 0), pipeline_mode=pl.Buffered(1)),
                  full(vg2), full(vb2), full(ws_arr), full(bs_arr)],
        out_specs=out_specs,
        scratch_shapes=[pltpu.VMEM((n_j, tm, tn), F32)],
        compiler_params=_cparams(1, 48),
        name="gmlp_in_chunked" if chunked else "gmlp_in_single",
    )(x, sc, sh, w_in, vg2, vb2, ws_arr, bs_arr)


LN_ROWS = 16


def _proj_res_ln_kernel(a_ref, w_ref, x_ref, gate_ref, lg_ref, lb_ref, *rest, alpha, emit_next):
    if emit_next:
        sc_ref, sh_ref, y_ref, h_ref, acc = rest
    else:
        y_ref, acc = rest
    acc[...] = _dot(a_ref[...].astype(BF), w_ref[...])
    tm = acc.shape[0]
    per_row_mod = gate_ref.shape[0] > 1
    step = min(LN_ROWS, tm)
    assert tm % step == 0
    for c in range(tm // step):
        rs = slice(c * step, (c + 1) * step)
        ms = rs if per_row_mod else slice(None)
        t = alpha * x_ref[rs, :] + (1.0 + gate_ref[ms, :]) * acc[rs, :]
        y = _layer_norm(t, lg_ref[...], lb_ref[...])
        y_ref[rs, :] = y
        if emit_next:
            h_ref[rs, :] = (y * (1.0 + sc_ref[ms, :]) + sh_ref[ms, :]).astype(BF)


def _proj_res_ln(a, w, x, gate, ln_g, ln_b, nxt, *, alpha, tm, tiles_per_group, layer=0):
    r, kdim = a.shape
    d = w.shape[2]
    emit_next = nxt is not None
    mod_spec = lambda m: _mod_spec(m, tiles_per_group)
    in_specs = [pl.BlockSpec((tm, kdim), lambda i: (i, 0)),
                pl.BlockSpec((None, kdim, d), lambda i: (layer, 0, 0), pipeline_mode=pl.Buffered(1)),
                pl.BlockSpec((tm, d), lambda i: (i, 0)),
                mod_spec(gate),
                pl.BlockSpec((1, d), lambda i: (0, 0)),
                pl.BlockSpec((1, d), lambda i: (0, 0))]
    args = [a, w, x, gate, ln_g.reshape(1, d), ln_b.reshape(1, d)]
    out_shape = [jax.ShapeDtypeStruct((r, d), F32)]
    out_specs = [pl.BlockSpec((tm, d), lambda i: (i, 0))]
    if emit_next:
        in_specs += [mod_spec(nxt[0]), mod_spec(nxt[1])]
        args += [nxt[0], nxt[1]]
        out_shape.append(jax.ShapeDtypeStruct((r, d), BF))
        out_specs.append(pl.BlockSpec((tm, d), lambda i: (i, 0)))
    res = pl.pallas_call(
        functools.partial(_proj_res_ln_kernel, alpha=alpha, emit_next=emit_next),
        out_shape=tuple(out_shape),
        grid=(r // tm,),
        in_specs=in_specs,
        out_specs=tuple(out_specs),
        scratch_shapes=[pltpu.VMEM((tm, d), F32)],
        compiler_params=_cparams(1, 56),
        name="proj_res_ln",
    )(*args)
    return res if emit_next else (res[0], None)


def _ffn_in_kernel(h_ref, wg_ref, wu_ref, o_ref):
    h = h_ref[...]
    o_ref[...] = (jax.nn.silu(_dot(h, wg_ref[...])) * _dot(h, wu_ref[...])).astype(BF)


def _ffn_in(h, w_in, layer, *, tm):
    r, d = h.shape
    d_ff = w_in.shape[2] // 2
    tn = 512
    n_j = d_ff // tn
    return pl.pallas_call(
        _ffn_in_kernel,
        out_shape=jax.ShapeDtypeStruct((r, d_ff), BF),
        grid=(r // tm, n_j),
        in_specs=[pl.BlockSpec((tm, d), lambda i, j: (i, 0)),
                  pl.BlockSpec((None, d, tn), lambda i, j: (layer, 0, j)),
                  pl.BlockSpec((None, d, tn), lambda i, j: (layer, 0, j + n_j))],
        out_specs=pl.BlockSpec((tm, tn), lambda i, j: (i, j)),
        compiler_params=_cparams(2, 40),
        name="ffn_in",
    )(h, w_in, w_in)


KV_SLABS = N_KV_HEADS * 2


def _kv_proj_kernel(y_ref, w_ref, f0_ref, f1_ref, f2_ref, b1_ref, b2_ref):
    xb = y_ref[...].astype(BF)
    tm = xb.shape[0]
    n = w_ref.shape[1] // 3
    for idx, (f_ref, b_ref) in enumerate(((f0_ref, None), (f1_ref, b1_ref), (f2_ref, b2_ref))):
        r = _dot(xb, w_ref[:, idx * n:(idx + 1) * n])
        for hk in range(KV_SLABS):
            f_ref[pl.ds(hk, tm, stride=KV_SLABS), :] = r[:, hk * HEAD_DIM:(hk + 1) * HEAD_DIM]
        if b_ref is not None:
            b_ref[...] = r.astype(BF)


def _kv_proj(y, w_kv, *, tm):
    r, d = y.shape
    n = w_kv.shape[1] // 3
    flat = jax.ShapeDtypeStruct((r * KV_SLABS, HEAD_DIM), F32)
    fspec = pl.BlockSpec((tm * KV_SLABS, HEAD_DIM), lambda i: (i, 0))
    wide = jax.ShapeDtypeStruct((r, n), BF)
    wspec = pl.BlockSpec((tm, n), lambda i: (i, 0))
    return pl.pallas_call(
        _kv_proj_kernel,
        out_shape=(flat, flat, flat, wide, wide),
        grid=(r // tm,),
        in_specs=[pl.BlockSpec((tm, d), lambda i: (i, 0)),
                  pl.BlockSpec(w_kv.shape, lambda i: (0, 0), pipeline_mode=pl.Buffered(1))],
        out_specs=(fspec, fspec, fspec, wspec, wspec),
        compiler_params=_cparams(1, 48),
        name="kv_proj",
    )(y, w_kv)


def _qg_proj_kernel(h_ref, wq_ref, wg_ref, q_ref, g_ref, *, gates_per_group):
    h = h_ref[...]
    q_ref[...] = _dot(h, wq_ref[...]).astype(BF)
    gates = jax.nn.sigmoid(_dot(h, wg_ref[...]))
    for grp in range(g_ref.shape[1] // LANES):
        shift = (LANES - grp * gates_per_group) % LANES
        g_ref[:, grp * LANES:(grp + 1) * LANES] = gates if shift == 0 else pltpu.roll(gates, shift, 1)


def _qg_proj(h, w_q, w_g, *, tm, gates_per_group):
    r, d = h.shape
    qw = w_q.shape[1]
    gw = N_KV_HEADS * LANES
    resident = lambda w: pl.BlockSpec(w.shape, lambda i: (0, 0), pipeline_mode=pl.Buffered(1))
    return pl.pallas_call(
        functools.partial(_qg_proj_kernel, gates_per_group=gates_per_group),
        out_shape=(jax.ShapeDtypeStruct((r, qw), BF), jax.ShapeDtypeStruct((r, gw), F32)),
        grid=(r // tm,),
        in_specs=[pl.BlockSpec((tm, d), lambda i: (i, 0)), resident(w_q), resident(w_g)],
        out_specs=(pl.BlockSpec((tm, qw), lambda i: (i, 0)),
                   pl.BlockSpec((tm, gw), lambda i: (i, 0))),
        compiler_params=_cparams(1, 40),
        name="qg_proj",
    )(h, w_q, w_g)


PAGES_PER_STEP = 16
CHUNKS_PER_PAGE = PAGE_SIZE // CMP_STRIDE


PAGE_FLAT_ROWS = PAGE_SIZE * KV_SLABS
CHUNK_FLAT_ROWS = CMP_STRIDE * KV_SLABS


def _compress_kernel(pt_ref, *refs, n_steps):
    page_refs = refs[:PAGES_PER_STEP]
    w1_ref, pe_ref, b1_ref, w2_ref, b2_ref, o_ref, p0_scr, p1_scr = refs[PAGES_PER_STEP:]
    t = pl.program_id(1)
    rows_step = PAGES_PER_STEP * CHUNKS_PER_PAGE * KV_SLABS
    n_rows = n_steps * rows_step
    hidden = w2_ref.shape[0]

    @pl.when(t == 0)
    def _():
        p1_scr[pl.ds(n_rows, KV_SLABS), :] = jnp.zeros((KV_SLABS, hidden), F32)

    per_s = []
    for s in range(CMP_STRIDE):
        rows = [page_refs[p][pl.ds(c * CHUNK_FLAT_ROWS + s * KV_SLABS, KV_SLABS), :]
                for p in range(PAGES_PER_STEP) for c in range(CHUNKS_PER_PAGE)]
        per_s.append(jnp.concatenate(rows, axis=0))
    x = jnp.concatenate(per_s, axis=1).astype(BF)
    pr = _dot(x, w1_ref[...])
    is_v = _mod_pow2(lax.broadcasted_iota(jnp.int32, (rows_step, 1), 0), 2) == 1
    pk = jnp.where(is_v, pr[:, 2 * hidden:], pr[:, :2 * hidden])
    row0 = pl.multiple_of(t * rows_step, rows_step)
    p0_scr[pl.ds(row0, rows_step), :] = pk[:, :hidden]
    p1_scr[pl.ds(row0, rows_step), :] = pk[:, hidden:]

    @pl.when(t == n_steps - 1)
    def _():
        cst = _dot(pe_ref[...].astype(BF), w1_ref[...])
        const_k = cst[0:1, :hidden] + cst[1:2, hidden:2 * hidden] + b1_ref[0:1, :]
        const_v = cst[2:3, 2 * hidden:3 * hidden] + cst[3:4, 3 * hidden:] + b1_ref[1:2, :]
        v_rows = _mod_pow2(lax.broadcasted_iota(jnp.int32, (n_rows, 1), 0), 2) == 1
        pre = p0_scr[0:n_rows, :] + p1_scr[KV_SLABS:n_rows + KV_SLABS, :] + jnp.where(v_rows, const_v, const_k)
        o2 = _dot(jax.nn.gelu(pre).astype(BF), w2_ref[...])
        o_ref[...] = jnp.where(v_rows, o2[:, HEAD_DIM:] + b2_ref[1:2, :], o2[:, :HEAD_DIM] + b2_ref[0:1, :])


def _compress(pages_flat, page_table, w1all, pe16, b1, w2all, b2):
    nb, n_pages = page_table.shape
    n_steps = n_pages // PAGES_PER_STEP
    n_rows = n_pages * CHUNKS_PER_PAGE * KV_SLABS
    hidden = w2all.shape[0]

    def page_spec(p):
        return pl.BlockSpec((PAGE_FLAT_ROWS, HEAD_DIM), lambda b, t, pt: (pt[b, t * PAGES_PER_STEP + p], 0))

    full = lambda a: pl.BlockSpec(a.shape, lambda b, t, pt: (0,) * a.ndim)
    grid_spec = pltpu.PrefetchScalarGridSpec(
        num_scalar_prefetch=1,
        grid=(nb, n_steps),
        in_specs=[page_spec(p) for p in range(PAGES_PER_STEP)] + [
            full(w1all), full(pe16), full(b1), full(w2all), full(b2)],
        out_specs=pl.BlockSpec((None, n_rows, HEAD_DIM), lambda b, t, pt: (b, 0, 0)),
        scratch_shapes=[pltpu.VMEM((n_rows, hidden), F32), pltpu.VMEM((n_rows + KV_SLABS, hidden), F32)],
    )
    return pl.pallas_call(
        functools.partial(_compress_kernel, n_steps=n_steps),
        out_shape=jax.ShapeDtypeStruct((nb, n_rows, HEAD_DIM), F32),
        grid_spec=grid_spec,
        compiler_params=_cparams(2, 48),
        name="compress_blocks",
    )(page_table, *([pages_flat] * PAGES_PER_STEP), w1all, pe16, b1, w2all, b2)


def _overlap_map(n_cmp_rows, n_slc, n_cols):
    cmp_lo = np.arange(n_cmp_rows)[:, None] * CMP_STRIDE
    slc_lo = np.arange(n_cols)[None, :] * SEL_BLOCK
    ov = np.minimum(cmp_lo + CMP_LEN, slc_lo + SEL_BLOCK) - np.maximum(cmp_lo, slc_lo)
    w = np.clip(ov, 0, None).astype(np.float32) / CMP_LEN
    w[:, n_slc:] = 0.0
    return w


SEL_CHUNK = 512


def _attn_prompt_kernel(q_ref, gt_ref, cmp_ref, sel_ref, win_ref, wmap_t_ref, exp_ref, o_ref,
                        s_sel, m_run, l_run, e_scr, s_win, m_win, l_win, e_win, *, scale, n_slc, n_qb, n_widths):
    g = pl.program_id(1)
    qb = pl.program_id(2)
    q4 = q_ref[...]
    nq = q4.shape[0]
    rq = q4.shape[1] // HEAD_DIM
    qs = jnp.concatenate([q4[:, r * HEAD_DIM:(r + 1) * HEAD_DIM] for r in range(rq)], axis=0)
    qpos = qb * nq + lax.broadcasted_iota(jnp.int32, (nq, 1), 0)
    rows = [slice(r * nq, (r + 1) * nq) for r in range(rq)]

    head_groups = [rows[:rq // 2], rows[rq // 2:]] if rq % 2 == 0 else [rows]

    def attend(n_chunks, chunk, s_scr, m_run, l_run, e_scr, keys, mask, values):
        width = n_chunks * chunk
        m_run[...] = jnp.full(m_run.shape, -jnp.inf, F32)
        l_run[...] = jnp.zeros(l_run.shape, F32)
        for grp in head_groups:
            g_rows = slice(grp[0].start, grp[-1].stop)
            for c in range(n_chunks):
                s = _dot_t(qs[g_rows], keys(c))
                ok = mask(c)
                for i, rs in enumerate(grp):
                    sm = jnp.where(ok, s[i * nq:(i + 1) * nq], -jnp.inf) * (scale * LOG2_E)
                    s_scr[c, rs, :] = sm
                    mx = m_run[rs, :]
                    for j in range(chunk // LANES):
                        mx = jnp.maximum(mx, sm[:, j * LANES:(j + 1) * LANES])
                    m_run[rs, :] = mx
        outs = []
        for grp in head_groups:
            g_rows = slice(grp[0].start, grp[-1].stop)
            m = jnp.max(m_run[g_rows, :], axis=-1, keepdims=True)
            m = jnp.where(m == -jnp.inf, 0.0, m)
            for c in range(n_chunks):
                for i, rs in enumerate(grp):
                    e = jnp.exp2(s_scr[c, rs, :] - m[i * nq:(i + 1) * nq])
                    ls = l_run[rs, :]
                    for j in range(chunk // LANES):
                        ls = ls + e[:, j * LANES:(j + 1) * LANES]
                    l_run[rs, :] = ls
                    e_scr[rs, c * chunk:(c + 1) * chunk] = e.astype(BF)
            l = jnp.sum(l_run[g_rows, :], axis=-1, keepdims=True)
            outs.append(_dot(e_scr[g_rows, :width], values) * (1.0 / jnp.maximum(l, 1e-30)))
        return jnp.concatenate(outs, axis=0)

    def step(n_sel_chunks, first_q):
        span = WINDOW + nq
        k0 = pl.multiple_of(jnp.maximum(qb * nq - WINDOW, 0), nq)
        kp = k0 + lax.broadcasted_iota(jnp.int32, (1, span), 1)
        valid_w = lax.bitcast_convert_type(qpos - kp, jnp.uint32) < jnp.uint32(WINDOW)
        o_win = attend(1, span, s_win, m_win, l_win, e_win, lambda c: win_ref[pl.ds(k0, span), :HEAD_DIM],
                       lambda c: valid_w, win_ref[pl.ds(k0, span), HEAD_DIM:])

        n_cmp_rows = cmp_ref.shape[0] // KV_SLABS
        kc = cmp_ref[pl.ds(g * 2, n_cmp_rows, stride=KV_SLABS), :].astype(BF)
        vc = cmp_ref[pl.ds(g * 2 + 1, n_cmp_rows, stride=KV_SLABS), :].astype(BF)
        n_io = lax.broadcasted_iota(jnp.int32, (1, n_cmp_rows), 1)
        valid_c = (n_io * CMP_STRIDE + (CMP_LEN - 1)) <= qpos
        s_c = _dot_t(qs, kc) * scale
        ps = [_masked_softmax(s_c[rs], valid_c).astype(BF) for rs in rows]
        o_cmp = _dot(jnp.concatenate(ps, axis=0), vc)

        imp = _dot_t(wmap_t_ref[...], jnp.concatenate(ps, axis=1))
        n_rows = imp.shape[0]
        j_io = lax.broadcasted_iota(jnp.int32, (n_rows, 1), 0)
        q_blk = _div_pow2(qb * nq + lax.broadcasted_iota(jnp.int32, (1, nq), 1), SEL_BLOCK)
        causal = j_io <= q_blk
        forced = (j_io == 0) | (causal & (j_io > q_blk - N_LOCAL_SEL))
        imp = jnp.where(forced, imp + FORCE_BONUS, imp)
        imp = jnp.where(causal, imp, -jnp.inf)
        rank = jnp.zeros(imp.shape, F32)
        for jp in range(n_slc):
            other = imp[jp:jp + 1, :]
            rank = rank + jnp.where(other == imp, jnp.where(jp < j_io, 1.0, 0.0),
                                    jnp.where(other > imp, 1.0, 0.0))
        chosen_t = jnp.where(rank < min(N_SEL, n_slc), jnp.where(imp > -jnp.inf, 1.0, 0.0), 0.0)
        chosen_t = jnp.concatenate([chosen_t, jnp.zeros((LANES - n_rows, nq), F32)], axis=0)
        chosen = chosen_t.T.astype(BF)

        def sel_mask(c):
            in_sel = _dot(chosen, exp_ref[c])
            if (c + 1) * SEL_CHUNK <= first_q:
                return in_sel > 0.5
            t_io = c * SEL_CHUNK + lax.broadcasted_iota(jnp.int32, (1, SEL_CHUNK), 1)
            return jnp.where(t_io <= qpos, in_sel, 0.0) > 0.5

        o_sel = attend(n_sel_chunks, SEL_CHUNK, s_sel, m_run, l_run, e_scr,
                       lambda c: sel_ref[c * SEL_CHUNK:(c + 1) * SEL_CHUNK, :HEAD_DIM], sel_mask,
                       sel_ref[:n_sel_chunks * SEL_CHUNK, HEAD_DIM:])

        gt = gt_ref[...]
        outs = []
        for r, rs in enumerate(rows):
            outs.append(gt[:, 3 * r:3 * r + 1] * o_cmp[rs] + gt[:, 3 * r + 1:3 * r + 2] * o_sel[rs]
                        + gt[:, 3 * r + 2:3 * r + 3] * o_win[rs])
        o_ref[...] = jnp.concatenate(outs, axis=1).astype(BF)

    qb_per_width = n_qb // n_widths
    for wi in range(n_widths):
        n_sel_chunks = -(-(wi + 1) * qb_per_width * nq // SEL_CHUNK)
        pl.when((qb >= wi * qb_per_width) & (qb < (wi + 1) * qb_per_width))(
            functools.partial(step, n_sel_chunks, wi * qb_per_width * nq))


def _attn_prompt(q, gates, cmp_flat, kv_sel, kv_win, *, scale):
    nb, t, qw = q.shape
    n_g = N_KV_HEADS
    gw = qw // n_g
    kvw = 2 * HEAD_DIM
    cmp_rows = cmp_flat.shape[1]
    n_slc = t // SEL_BLOCK
    n_slc_rows = -(-n_slc // 8) * 8
    rq = gw // HEAD_DIM
    n_qb = t // Q_BLOCK
    n_chunks = t // SEL_CHUNK
    wmap_t = jnp.asarray(np.tile(_overlap_map(cmp_rows // KV_SLABS, n_slc, n_slc_rows), (rq, 1)).T, BF)
    key_blk = (np.arange(t) // SEL_BLOCK).reshape(n_chunks, 1, SEL_CHUNK)
    expand = jnp.asarray((key_blk == np.arange(LANES)[None, :, None]).astype(np.float32), BF)
    rows = rq * Q_BLOCK
    n_widths = 4 if n_qb % 4 == 0 else 1
    return pl.pallas_call(
        functools.partial(_attn_prompt_kernel, scale=scale, n_slc=n_slc, n_qb=n_qb, n_widths=n_widths),
        out_shape=jax.ShapeDtypeStruct((nb, t, qw), BF),
        grid=(nb, n_g, n_qb),
        in_specs=[pl.BlockSpec((None, Q_BLOCK, gw), lambda b, g, i: (b, i, g)),
                  pl.BlockSpec((None, Q_BLOCK, LANES), lambda b, g, i: (b, i, g)),
                  pl.BlockSpec((None, cmp_rows, HEAD_DIM), lambda b, g, i: (b, 0, 0)),
                  pl.BlockSpec((None, t, kvw), lambda b, g, i: (b, 0, g)),
                  pl.BlockSpec((None, t, kvw), lambda b, g, i: (b, 0, g)),
                  pl.BlockSpec(wmap_t.shape, lambda b, g, i: (0, 0)),
                  pl.BlockSpec(expand.shape, lambda b, g, i: (0, 0, 0))],
        out_specs=pl.BlockSpec((None, Q_BLOCK, gw), lambda b, g, i: (b, i, g)),
        scratch_shapes=[pltpu.VMEM((n_chunks, rows, SEL_CHUNK), F32),
                        pltpu.VMEM((rows, LANES), F32), pltpu.VMEM((rows, LANES), F32),
                        pltpu.VMEM((rows, t), BF),
                        pltpu.VMEM((1, rows, WINDOW + Q_BLOCK), F32),
                        pltpu.VMEM((rows, LANES), F32), pltpu.VMEM((rows, LANES), F32),
                        pltpu.VMEM((rows, WINDOW + Q_BLOCK), BF)],
        compiler_params=_cparams(3, 48),
        name="attn_prompt",
    )(q, gates, cmp_flat, kv_sel, kv_win, wmap_t, expand)


def _attn_sample_a_kernel(q_ref, cmp_ref, win_ref, wmap_ref, ocmp_ref, owin_ref, idx_ref, val_ref, *,
                          scale, q_pos, n_cmp, n_slc, win_pos0):
    q = q_ref[...]
    n_h = q.shape[0]
    rq = n_h // N_KV_HEADS
    row_g = _div_pow2(lax.broadcasted_iota(jnp.int32, (n_h, 1), 0), rq)
    cmp_rows = cmp_ref.shape[0] // KV_SLABS
    win_rows = win_ref.shape[0] // KV_SLABS
    n_io = lax.broadcasted_iota(jnp.int32, (1, cmp_rows), 1)
    valid_c = (n_io < n_cmp) & ((n_io * CMP_STRIDE + (CMP_LEN - 1)) <= q_pos)
    kp = win_pos0 + lax.broadcasted_iota(jnp.int32, (1, win_rows), 1)
    valid_w = (kp <= q_pos) & (kp > q_pos - WINDOW)
    o_cmp = jnp.zeros((n_h, HEAD_DIM), F32)
    o_win = jnp.zeros((n_h, HEAD_DIM), F32)
    imp = jnp.zeros((n_h, wmap_ref.shape[1]), F32)
    for g in range(N_KV_HEADS):
        in_g = row_g == g
        kc = cmp_ref[pl.ds(2 * g, cmp_rows, stride=KV_SLABS), :].astype(BF)
        vc = cmp_ref[pl.ds(2 * g + 1, cmp_rows, stride=KV_SLABS), :].astype(BF)
        p = _masked_softmax(_dot_t(q, kc) * scale, valid_c).astype(BF)
        o_cmp = jnp.where(in_g, _dot(p, vc), o_cmp)
        imp_rows = _dot(p, wmap_ref[...])
        imp_g = jnp.sum(jnp.where(in_g, imp_rows, 0.0), axis=0, keepdims=True)
        imp = jnp.where(in_g, imp_g, imp)
        kw = win_ref[pl.ds(2 * g, win_rows, stride=KV_SLABS), :].astype(BF)
        vw = win_ref[pl.ds(2 * g + 1, win_rows, stride=KV_SLABS), :].astype(BF)
        pw = _masked_softmax(_dot_t(q, kw) * scale, valid_w).astype(BF)
        o_win = jnp.where(in_g, _dot(pw, vw), o_win)
    ocmp_ref[...] = o_cmp
    owin_ref[...] = o_win

    j_io = lax.broadcasted_iota(jnp.int32, (1, imp.shape[1]), 1)
    q_blk = q_pos // SEL_BLOCK
    causal = j_io <= q_blk
    forced = (j_io == 0) | (causal & (j_io > q_blk - N_LOCAL_SEL))
    imp = jnp.where(forced, imp + FORCE_BONUS, imp)
    imp = jnp.where(causal, imp, -jnp.inf)
    rank = _rank_desc(imp, n_slc)
    jf = j_io.astype(F32)
    k_io = lax.broadcasted_iota(jnp.int32, (1, idx_ref.shape[1]), 1)
    idx = jnp.zeros(idx_ref.shape, F32)
    val = jnp.zeros(idx_ref.shape, F32)
    for k in range(min(N_SEL, n_slc)):
        hit = rank == k
        ik = jnp.sum(jnp.where(hit, jf, 0.0), axis=1, keepdims=True)
        vk = jnp.sum(jnp.where(hit, jnp.where(imp > -jnp.inf, 1.0, 0.0), 0.0), axis=1, keepdims=True)
        idx = jnp.where(k_io == k, ik, idx)
        val = jnp.where(k_io == k, vk, val)
    idx_ref[...] = idx.astype(jnp.int32)
    val_ref[...] = val.astype(jnp.int32)


def _attn_sample_a(q, cmp_blocks, win_rows, *, scale, q_pos, n_cmp, n_slc, win_pos0):
    nb, n_h, d = q.shape
    n_cols = -(-n_slc // LANES) * LANES
    wmap = jnp.asarray(_overlap_map(cmp_blocks.shape[1] // KV_SLABS, n_slc, n_cols), BF)
    o_sds = jax.ShapeDtypeStruct((nb, n_h, d), F32)
    i_sds = jax.ShapeDtypeStruct((nb, n_h, LANES), jnp.int32)
    blk = lambda a: pl.BlockSpec((None,) + a.shape[1:], lambda b: (b, 0, 0))
    ospec = pl.BlockSpec((None, n_h, d), lambda b: (b, 0, 0))
    ispec = pl.BlockSpec((None, n_h, LANES), lambda b: (b, 0, 0))
    return pl.pallas_call(
        functools.partial(_attn_sample_a_kernel, scale=scale, q_pos=q_pos, n_cmp=n_cmp, n_slc=n_slc,
                          win_pos0=win_pos0),
        out_shape=(o_sds, o_sds, i_sds, i_sds),
        grid=(nb,),
        in_specs=[blk(q), blk(cmp_blocks), blk(win_rows), pl.BlockSpec(wmap.shape, lambda b: (0, 0))],
        out_specs=(ospec, ospec, ispec, ispec),
        compiler_params=_cparams(1, 32),
        name="attn_sample_a",
    )(q, cmp_blocks, win_rows, wmap)


def _attn_sample_b_kernel(idx_ref, val_ref, pt_ref, q_ref, new_ref, gt_ref, ocmp_ref, owin_ref, *rest,
                          scale, q_pos, past_len):
    blk_refs = rest[:N_SEL]
    o_ref = rest[N_SEL]
    b = pl.program_id(0)
    g = pl.program_id(1)
    q = q_ref[...]
    n_h = q.shape[0]
    rq = n_h // N_KV_HEADS
    base = (b * N_KV_HEADS + g) * N_SEL
    n_keys = N_SEL * SEL_BLOCK
    lane = lax.broadcasted_iota(jnp.int32, (1, n_keys), 1)
    k_new = new_ref[pl.ds(2 * g, 1), :]
    v_new = new_ref[pl.ds(2 * g + 1, 1), :]
    ks, vs = [], []
    kpos = _mod_pow2(lane, SEL_BLOCK)
    kval = jnp.zeros((1, n_keys), jnp.int32)
    lane_blk = _div_pow2(lane, SEL_BLOCK)
    for k in range(N_SEL):
        blk = idx_ref[base + k]
        is_new = blk * SEL_BLOCK >= past_len
        ks.append(jnp.where(is_new, k_new, blk_refs[k][pl.ds(2 * g, SEL_BLOCK, stride=KV_SLABS), :]).astype(BF))
        vs.append(jnp.where(is_new, v_new, blk_refs[k][pl.ds(2 * g + 1, SEL_BLOCK, stride=KV_SLABS), :]).astype(BF))
        in_k = lane_blk == k
        kpos = kpos + jnp.where(in_k, blk * SEL_BLOCK, 0)
        kval = jnp.where(in_k, val_ref[base + k], kval)
    valid = (kval > 0) & (kpos <= q_pos)
    p = _masked_softmax(_dot_t(q, jnp.concatenate(ks, axis=0)) * scale, valid).astype(BF)
    o_sel = _dot(p, jnp.concatenate(vs, axis=0))
    gt = gt_ref[...]
    in_g = _div_pow2(lax.broadcasted_iota(jnp.int32, (n_h, 1), 0), rq) == g

    @pl.when(g == 0)
    def _():
        o_ref[...] = gt[:, 0:1] * ocmp_ref[...] + gt[:, 2:3] * owin_ref[...]

    o_ref[...] += jnp.where(in_g, gt[:, 1:2] * o_sel, 0.0)


def _attn_sample_b(sel_idx, sel_val, page_table, q, kv_sel_new, gates, o_cmp, o_win, pool_sel, *,
                   scale, q_pos, past_len):
    nb, n_h, d = q.shape
    halves = PAGE_SIZE // SEL_BLOCK
    blk_rows = SEL_BLOCK * KV_SLABS
    pool = pool_sel.reshape(pool_sel.shape[0] // blk_rows, blk_rows, d)

    def blk_spec(k):
        def index_map(b, g, idx, val, pt):
            row0 = jnp.minimum(idx[(b * N_KV_HEADS + g) * N_SEL + k] * SEL_BLOCK, past_len - 1)
            return (pt[b, row0 // PAGE_SIZE] * halves + (row0 % PAGE_SIZE) // SEL_BLOCK, 0, 0)
        return pl.BlockSpec((None, blk_rows, d), index_map)

    per_b = lambda a: pl.BlockSpec((None,) + a.shape[1:], lambda b, g, idx, val, pt: (b, 0, 0))
    new_rows = kv_sel_new.reshape(nb, KV_SLABS, d)
    grid_spec = pltpu.PrefetchScalarGridSpec(
        num_scalar_prefetch=3,
        grid=(nb, N_KV_HEADS),
        in_specs=[per_b(q), per_b(new_rows), per_b(gates), per_b(o_cmp), per_b(o_win)]
        + [blk_spec(k) for k in range(N_SEL)],
        out_specs=pl.BlockSpec((None, n_h, d), lambda b, g, idx, val, pt: (b, 0, 0)),
    )
    return pl.pallas_call(
        functools.partial(_attn_sample_b_kernel, scale=scale, q_pos=q_pos, past_len=past_len),
        out_shape=jax.ShapeDtypeStruct((nb, n_h, d), F32),
        grid_spec=grid_spec,
        compiler_params=_cparams(2, 32),
        name="attn_sample_b",
    )(sel_idx, sel_val, page_table, q, new_rows, gates, o_cmp, o_win, *([pool] * N_SEL))


def kernel(x_prompt, x_sample, cache_cmp_kv, cache_sel_kv, cache_win_kv, page_table, c_prompt, c_sample, w_ada, b_ada, ln_g, ln_b, a_w_in, a_v_g, a_v_b, a_w_s, a_b_s, a_w_out, w_kv, cmp_pe, cmp_w1, cmp_b1, cmp_w2, cmp_b2, b_w_qg, b_w_o, ffn_w_in, ffn_w_out):
    nb, t, d = x_prompt.shape
    ns, tq, _ = x_sample.shape
    assert tq == 1, "the sample path handles one new token per sequence"
    depth = w_ada.shape[0]
    n_a = a_w_in.shape[0]
    alpha = float((2 * depth) ** 0.25)
    scale = float(HEAD_DIM ** -0.5)
    n_pool = cache_cmp_kv.shape[0]
    past_len = page_table.shape[1] * PAGE_SIZE
    kvw = N_KV_HEADS * 2 * HEAD_DIM
    q_width = b_w_o.shape[1]
    n_heads = q_width // HEAD_DIM
    rq = n_heads // N_KV_HEADS

    mod = _ada_modulation(jnp.concatenate([c_prompt, c_sample], axis=0), w_ada, b_ada)

    def mods(layer, sub):
        m = mod[layer * 2 + sub]
        out = []
        for part in range(3):
            sl = m[:, part * d:(part + 1) * d]
            out.append((sl[:nb].reshape(nb, 1, d), sl[nb:].reshape(1, ns, d)))
        return out

    wb = lambda w: w.astype(BF)
    hidden = cmp_w1.shape[-1]
    halves = CMP_LEN // CMP_STRIDE
    assert halves == 2, "a compression block spans two stride-chunks"
    w1all = wb(cmp_w1.reshape(2, halves, CMP_STRIDE, HEAD_DIM, hidden)
               .transpose(2, 3, 0, 1, 4).reshape(CMP_STRIDE * HEAD_DIM, 2 * halves * hidden))
    pe16 = jnp.pad(cmp_pe.reshape(2 * halves, CMP_STRIDE * HEAD_DIM), ((0, 16 - 2 * halves), (0, 0)))
    w2all = wb(cmp_w2.transpose(1, 0, 2).reshape(hidden, 2 * HEAD_DIM))

    a_w_out_b, b_w_o_b, ffn_w_in_b, ffn_w_out_b = wb(a_w_out), wb(b_w_o), wb(ffn_w_in), wb(ffn_w_out)

    tm_p = 512
    tpg_p = t // tm_p
    prompt = dict(tm=tm_p, tiles_per_group=tpg_p)
    sample = dict(tm=ns, tiles_per_group=1)

    xp = x_prompt.reshape(nb * t, d)
    xs = x_sample.reshape(ns, d)
    hp = hs = None
    chunk_v = []
    outs = {}
    for layer in range(depth):
        (sh_p, sh_s), (sc_p, sc_s), (gt_p, gt_s) = mods(layer, 0)
        (fsh_p, fsh_s), (fsc_p, fsc_s), (fgt_p, fgt_s) = mods(layer, 1)
        if layer < n_a:
            w_in = wb(a_w_in[layer])
            ap = _gmlp_in(xp, sc_p, sh_p, w_in, a_v_g[layer], a_v_b[layer], a_w_s[layer], a_b_s[layer],
                          chunked=True, **prompt)
            as_, v_rows = _gmlp_in(xs, sc_s, sh_s, w_in, a_v_g[layer], a_v_b[layer], a_w_s[layer], a_b_s[layer],
                                   chunked=False, **sample)
            chunk_v.append(v_rows.reshape(ns, tq, -1))
            xp, hp = _proj_res_ln(ap, a_w_out_b, xp, gt_p, ln_g[layer, 0], ln_b[layer, 0], (fsc_p, fsh_p),
                                  alpha=alpha, layer=layer, **prompt)
            xs, hs = _proj_res_ln(as_, a_w_out_b, xs, gt_s, ln_g[layer, 0], ln_b[layer, 0], (fsc_s, fsh_s),
                                  alpha=alpha, layer=layer, **sample)
        else:
            if layer == n_a:
                w_kvb = wb(w_kv)
                cmp_p, sel_p, win_p, sel_pb, win_pb = _kv_proj(xp, w_kvb, tm=tm_p)
                cmp_s, sel_s, win_s, _, _ = _kv_proj(xs, w_kvb, tm=ns)
                pages_per_seq_p = t // PAGE_SIZE
                ident = jnp.arange(nb * pages_per_seq_p, dtype=jnp.int32).reshape(nb, pages_per_seq_p)
                cmp_blocks_p = _compress(cmp_p, ident, w1all, pe16, cmp_b1, w2all, cmp_b2)
                assert (past_len + tq) // CMP_STRIDE == past_len // CMP_STRIDE
                cmp_blocks_s = _compress(cache_cmp_kv.reshape(n_pool * PAGE_FLAT_ROWS, HEAD_DIM), page_table,
                                         w1all, pe16, cmp_b1, w2all, cmp_b2)
                n_win = cache_win_kv.shape[1]
                win_keys_s = jnp.concatenate([cache_win_kv.reshape(ns, n_win * KV_SLABS, HEAD_DIM),
                                              win_s.reshape(ns, tq * KV_SLABS, HEAD_DIM)],
                                             axis=1)[:, -n_win * KV_SLABS:]
                outs["kv_p"] = (cmp_p, sel_p, win_p)
                outs["kv_pb"] = (sel_pb.reshape(nb, t, kvw), win_pb.reshape(nb, t, kvw))
                outs["kv_s"] = (cmp_s, sel_s, win_keys_s)
            j = layer - n_a
            w_qg = b_w_qg[j]
            w_q = wb(w_qg[:, :q_width])
            n_gates = w_qg.shape[1] - q_width
            w_gate = wb(jnp.pad(w_qg[:, q_width:], ((0, 0), (0, LANES - n_gates))))
            assert layer > 0, "an attention layer must follow another layer"
            q_p, g_p = _qg_proj(hp, w_q, w_gate, tm=tm_p, gates_per_group=rq * 3)
            q_s, g_s = _qg_proj(hs, w_q, w_gate, tm=ns, gates_per_group=rq * 3)
            o_p = _attn_prompt(q_p.reshape(nb, t, q_width), g_p.reshape(nb, t, N_KV_HEADS * LANES),
                               cmp_blocks_p, outs["kv_pb"][0], outs["kv_pb"][1], scale=scale)
            n_cmp_s = (past_len + tq) // CMP_STRIDE - CMP_LEN // CMP_STRIDE + 1
            n_slc_s = -(-(past_len + tq) // SEL_BLOCK)
            q_s3 = q_s.reshape(ns, n_heads, HEAD_DIM)
            o_cmp_s, o_win_s, idx_s, val_s = _attn_sample_a(
                q_s3, cmp_blocks_s, outs["kv_s"][2], scale=scale, q_pos=past_len, n_cmp=n_cmp_s, n_slc=n_slc_s,
                win_pos0=past_len + tq - outs["kv_s"][2].shape[1] // KV_SLABS)
            sel_idx = idx_s[:, ::rq, :N_SEL].reshape(-1)
            sel_val = val_s[:, ::rq, :N_SEL].reshape(-1)
            gates_s = g_s.reshape(ns, N_KV_HEADS, LANES)[:, :, :rq * 3].reshape(ns, n_heads, 3)
            o_s = _attn_sample_b(sel_idx, sel_val, page_table, q_s3, outs["kv_s"][1], gates_s, o_cmp_s, o_win_s,
                                 cache_sel_kv.reshape(n_pool * PAGE_FLAT_ROWS, HEAD_DIM), scale=scale,
                                 q_pos=past_len, past_len=past_len)
            xp, hp = _proj_res_ln(o_p.reshape(nb * t, q_width), b_w_o_b, xp, gt_p, ln_g[layer, 0], ln_b[layer, 0],
                                  (fsc_p, fsh_p), alpha=alpha, layer=j, **prompt)
            xs, hs = _proj_res_ln(o_s.reshape(ns, q_width), b_w_o_b, xs, gt_s, ln_g[layer, 0], ln_b[layer, 0],
                                  (fsc_s, fsh_s), alpha=alpha, layer=j, **sample)
        fp = _ffn_in(hp, ffn_w_in_b, layer, tm=2 * tm_p)
        fs = _ffn_in(hs, ffn_w_in_b, layer, tm=ns)
        nxt_p = nxt_s = None
        if n_a <= layer + 1 < depth:
            (nsh_p, nsh_s), (nsc_p, nsc_s), _ = mods(layer + 1, 0)
            nxt_p, nxt_s = (nsc_p, nsh_p), (nsc_s, nsh_s)
        tm_f = tm_p // 2
        xp, hp = _proj_res_ln(fp, ffn_w_out_b, xp, fgt_p, ln_g[layer, 1], ln_b[layer, 1], nxt_p, alpha=alpha,
                              layer=layer, tm=tm_f, tiles_per_group=t // tm_f)
        xs, hs = _proj_res_ln(fs, ffn_w_out_b, xs, fgt_s, ln_g[layer, 1], ln_b[layer, 1], nxt_s, alpha=alpha,
                              layer=layer, **sample)

    cmp_p, sel_p, win_p = outs["kv_p"]
    cmp_s, sel_s, win_keys_s = outs["kv_s"]
    kv_shape = (N_KV_HEADS, 2, HEAD_DIM)
    n_win_p = min(WINDOW, t)
    return (xp.reshape(nb, t, d), xs.reshape(ns, tq, d),
            cmp_p.reshape((nb, t) + kv_shape), sel_p.reshape((nb, t) + kv_shape),
            win_p.reshape((nb, t) + kv_shape)[:, -n_win_p:],
            cmp_s.reshape((ns, tq) + kv_shape), sel_s.reshape((ns, tq) + kv_shape),
            win_keys_s.reshape((ns, -1) + kv_shape),
            jnp.stack(chunk_v))
```

```python
import functools

import numpy as np
import jax
import jax.numpy as jnp
from jax import lax
from jax.experimental import pallas as pl
from jax.experimental.pallas import tpu as pltpu

BF = jnp.bfloat16
F32 = jnp.float32

LN_EPS = 1e-5
CHUNK = 128
HEAD_DIM = 128
N_KV_HEADS = 4
CMP_LEN = 32
CMP_STRIDE = 16
SEL_BLOCK = 64
N_SEL = 16
N_LOCAL_SEL = 2
FORCE_BONUS = 1e4
LOG2_E = 1.4426950408889634
WINDOW = 512
Q_BLOCK = 256
PAGE_SIZE = 128

LANES = 128
MIB = 1024 * 1024


def _cparams(n_axes, vmem_mib):
    return pltpu.CompilerParams(dimension_semantics=("arbitrary",) * n_axes,
                                vmem_limit_bytes=vmem_mib * MIB)


def _dot(a, b):
    return jnp.dot(a, b, preferred_element_type=F32)


def _dot_t(a, b):
    return lax.dot_general(a, b, (((1,), (1,)), ((), ())), preferred_element_type=F32)


def _layer_norm(x, g, b):
    mu = jnp.mean(x, axis=-1, keepdims=True)
    xc = x - mu
    var = jnp.mean(xc * xc, axis=-1, keepdims=True)
    return xc * lax.rsqrt(var + LN_EPS) * g + b


def _masked_softmax(s, valid):
    s = jnp.where(valid, s, -jnp.inf)
    m = jnp.max(s, axis=-1, keepdims=True)
    m = jnp.where(m == -jnp.inf, 0.0, m)
    e = jnp.exp(s - m)
    d = jnp.maximum(jnp.sum(e, axis=-1, keepdims=True), 1e-30)
    return e * (1.0 / d)


def _log2(n):
    assert n & (n - 1) == 0, "power of two expected"
    return n.bit_length() - 1


def _div_pow2(x, n):
    return lax.shift_right_logical(x, jnp.int32(_log2(n)))


def _mod_pow2(x, n):
    assert n & (n - 1) == 0, "power of two expected"
    return lax.bitwise_and(x, jnp.int32(n - 1))


def _rank_desc(imp, n):
    j_io = lax.broadcasted_iota(jnp.int32, (1, imp.shape[1]), 1)
    rank = jnp.zeros(imp.shape, F32)
    for jp in range(n):
        col = imp[:, jp:jp + 1]
        before = jnp.where(col == imp, jnp.where(jp < j_io, 1.0, 0.0), jnp.where(col > imp, 1.0, 0.0))
        rank = rank + before
    return rank


def _ada_kernel(c_ref, w_ref, b_ref, o_ref):
    a = jax.nn.silu(c_ref[...]).astype(BF)
    o_ref[...] = _dot(a, w_ref[...].astype(BF)) + b_ref[...]


def _ada_modulation(c_all, w_ada, b_ada):
    n_sub = w_ada.shape[0] * w_ada.shape[1]
    d, n3 = w_ada.shape[2], w_ada.shape[3]
    m = c_all.shape[0]
    tn = 512
    w = w_ada.reshape(n_sub, d, n3)
    b = b_ada.reshape(n_sub, 1, n3)
    return pl.pallas_call(
        _ada_kernel,
        out_shape=jax.ShapeDtypeStruct((n_sub, m, n3), F32),
        grid=(n_sub, n3 // tn),
        in_specs=[pl.BlockSpec((m, d), lambda s, j: (0, 0)),
                  pl.BlockSpec((None, d, tn), lambda s, j: (s, 0, j)),
                  pl.BlockSpec((None, 1, tn), lambda s, j: (s, 0, j))],
        out_specs=pl.BlockSpec((None, m, tn), lambda s, j: (s, 0, j)),
        compiler_params=_cparams(2, 32),
        name="ada_modulation",
    )(c_all, w, b)


def _mod_spec(mod, tiles_per_group):
    rb, d = mod.shape[1], mod.shape[2]
    return pl.BlockSpec((None, rb, d), lambda i: (i // tiles_per_group, 0, 0))


def _gmlp_in_kernel(x_ref, sc_ref, sh_ref, w_ref, vg_ref, vb_ref, ws_ref, bs_ref, *rest,
                    n_j, tn, chunked):
    if chunked:
        o_ref, z_scr = rest
    else:
        o_ref, v_ref, z_scr = rest
    h = (x_ref[...] * (1.0 + sc_ref[...]) + sh_ref[...]).astype(BF)
    for jj in range(n_j):
        z_scr[jj] = jax.nn.gelu(_dot(h, w_ref[:, jj * tn:(jj + 1) * tn]))

    tm = z_scr.shape[1]
    half = n_j // 2
    e_a = half * tn
    s1 = jnp.zeros((tm, 1), F32)
    for jj in range(half):
        s1 = s1 + jnp.sum(z_scr[half + jj], axis=-1, keepdims=True)
    mu = s1 / e_a
    s2 = jnp.zeros((tm, 1), F32)
    for jj in range(half):
        vc = z_scr[half + jj] - mu
        s2 = s2 + jnp.sum(vc * vc, axis=-1, keepdims=True)
    rstd = lax.rsqrt(s2 / e_a + LN_EPS)
    gpt = tn // CHUNK
    if chunked:
        row = lax.broadcasted_iota(jnp.int32, (CHUNK, CHUNK), 0)
        col = lax.broadcasted_iota(jnp.int32, (CHUNK, CHUNK), 1)
        for jj in range(half):
            for gi in range(gpt):
                g = jj * gpt + gi
                cs = slice(gi * CHUNK, (gi + 1) * CHUNK)
                wm = jnp.where(row >= col, ws_ref[g], 0.0).astype(BF)
                bcol = bs_ref[:, g:g + 1]
                vg = vg_ref[:, g * CHUNK:(g + 1) * CHUNK]
                vb = vb_ref[:, g * CHUNK:(g + 1) * CHUNK]
                for c in range(tm // CHUNK):
                    rs = slice(c * CHUNK, (c + 1) * CHUNK)
                    vn = (z_scr[half + jj, rs, cs] - mu[rs]) * rstd[rs] * vg + vb
                    mixed = _dot(wm, vn.astype(BF)) + bcol
                    o_ref[rs, g * CHUNK:(g + 1) * CHUNK] = (z_scr[jj, rs, cs] * mixed).astype(BF)
    else:
        for jj in range(half):
            cs = slice(jj * tn, (jj + 1) * tn)
            vn = (z_scr[half + jj] - mu) * rstd * vg_ref[:, cs] + vb_ref[:, cs]
            v_ref[:, cs] = vn
            mixed = ws_ref[:, cs] * vn + bs_ref[:, cs]
            o_ref[:, cs] = (z_scr[jj] * mixed).astype(BF)


def _gmlp_in(x, sc, sh, w_in, v_g, v_b, w_s, b_s, *, tm, tiles_per_group, chunked):
    r, d = x.shape
    n2 = w_in.shape[1]
    e_a = n2 // 2
    tn = 512
    n_j = n2 // tn
    n_g = w_s.shape[0]
    full = lambda a: pl.BlockSpec(a.shape, lambda i: (0,) * a.ndim)
    row_tile = pl.BlockSpec((tm, e_a), lambda i: (i, 0))
    if chunked:
        ws_arr = w_s
        bs_arr = b_s.T
        out_shape = jax.ShapeDtypeStruct((r, e_a), BF)
        out_specs = row_tile
    else:
        ws_arr = jnp.repeat(w_s[:, 0, 0], e_a // n_g).reshape(1, e_a)
        bs_arr = jnp.repeat(b_s[:, 0], e_a // n_g).reshape(1, e_a)
        out_shape = (jax.ShapeDtypeStruct((r, e_a), BF), jax.ShapeDtypeStruct((r, e_a), F32))
        out_specs = (row_tile, row_tile)
    vg2, vb2 = v_g.reshape(1, e_a), v_b.reshape(1, e_a)
    return pl.pallas_call(
        functools.partial(_gmlp_in_kernel, n_j=n_j, tn=tn, chunked=chunked),
        out_shape=out_shape,
        grid=(r // tm,),
        in_specs=[pl.BlockSpec((tm, d), lambda i: (i, 0)),
                  _mod_spec(sc, tiles_per_group), _mod_spec(sh, tiles_per_group),
                  pl.BlockSpec(w_in.shape, lambda i: (0, 0), pipeline_mode=pl.Buffered(1)),
                  full(vg2), full(vb2), full(ws_arr), full(bs_arr)],
        out_specs=out_specs,
        scratch_shapes=[pltpu.VMEM((n_j, tm, tn), F32)],
        compiler_params=_cparams(1, 48),
        name="gmlp_in_chunked" if chunked else "gmlp_in_single",
    )(x, sc, sh, w_in, vg2, vb2, ws_arr, bs_arr)


LN_ROWS = 16


def _proj_res_ln_kernel(a_ref, w_ref, x_ref, gate_ref, lg_ref, lb_ref, *rest, alpha, emit_next):
    if emit_next:
        sc_ref, sh_ref, y_ref, h_ref, acc = rest
    else:
        y_ref, acc = rest
    acc[...] = _dot(a_ref[...].astype(BF), w_ref[...])
    tm = acc.shape[0]
    per_row_mod = gate_ref.shape[0] > 1
    step = min(LN_ROWS, tm)
    assert tm % step == 0
    for c in range(tm // step):
        rs = slice(c * step, (c + 1) * step)
        ms = rs if per_row_mod else slice(None)
        t = alpha * x_ref[rs, :] + (1.0 + gate_ref[ms, :]) * acc[rs, :]
        y = _layer_norm(t, lg_ref[...], lb_ref[...])
        y_ref[rs, :] = y
        if emit_next:
            h_ref[rs, :] = (y * (1.0 + sc_ref[ms, :]) + sh_ref[ms, :]).astype(BF)


def _proj_res_ln(a, w, x, gate, ln_g, ln_b, nxt, *, alpha, tm, tiles_per_group, layer=0):
    r, kdim = a.shape
    d = w.shape[2]
    emit_next = nxt is not None
    mod_spec = lambda m: _mod_spec(m, tiles_per_group)
    in_specs = [pl.BlockSpec((tm, kdim), lambda i: (i, 0)),
                pl.BlockSpec((None, kdim, d), lambda i: (layer, 0, 0), pipeline_mode=pl.Buffered(1)),
                pl.BlockSpec((tm, d), lambda i: (i, 0)),
                mod_spec(gate),
                pl.BlockSpec((1, d), lambda i: (0, 0)),
                pl.BlockSpec((1, d), lambda i: (0, 0))]
    args = [a, w, x, gate, ln_g.reshape(1, d), ln_b.reshape(1, d)]
    out_shape = [jax.ShapeDtypeStruct((r, d), F32)]
    out_specs = [pl.BlockSpec((tm, d), lambda i: (i, 0))]
    if emit_next:
        in_specs += [mod_spec(nxt[0]), mod_spec(nxt[1])]
        args += [nxt[0], nxt[1]]
        out_shape.append(jax.ShapeDtypeStruct((r, d), BF))
        out_specs.append(pl.BlockSpec((tm, d), lambda i: (i, 0)))
    res = pl.pallas_call(
        functools.partial(_proj_res_ln_kernel, alpha=alpha, emit_next=emit_next),
        out_shape=tuple(out_shape),
        grid=(r // tm,),
        in_specs=in_specs,
        out_specs=tuple(out_specs),
        scratch_shapes=[pltpu.VMEM((tm, d), F32)],
        compiler_params=_cparams(1, 56),
        name="proj_res_ln",
    )(*args)
    return res if emit_next else (res[0], None)


def _ffn_in_kernel(h_ref, wg_ref, wu_ref, o_ref):
    h = h_ref[...]
    o_ref[...] = (jax.nn.silu(_dot(h, wg_ref[...])) * _dot(h, wu_ref[...])).astype(BF)


def _ffn_in(h, w_in, layer, *, tm):
    r, d = h.shape
    d_ff = w_in.shape[2] // 2
    tn = 512
    n_j = d_ff // tn
    return pl.pallas_call(
        _ffn_in_kernel,
        out_shape=jax.ShapeDtypeStruct((r, d_ff), BF),
        grid=(r // tm, n_j),
        in_specs=[pl.BlockSpec((tm, d), lambda i, j: (i, 0)),
                  pl.BlockSpec((None, d, tn), lambda i, j: (layer, 0, j)),
                  pl.BlockSpec((None, d, tn), lambda i, j: (layer, 0, j + n_j))],
        out_specs=pl.BlockSpec((tm, tn), lambda i, j: (i, j)),
        compiler_params=_cparams(2, 40),
        name="ffn_in",
    )(h, w_in, w_in)


KV_SLABS = N_KV_HEADS * 2


def _kv_proj_kernel(y_ref, w_ref, f0_ref, f1_ref, f2_ref, b1_ref, b2_ref):
    xb = y_ref[...].astype(BF)
    tm = xb.shape[0]
    n = w_ref.shape[1] // 3
    for idx, (f_ref, b_ref) in enumerate(((f0_ref, None), (f1_ref, b1_ref), (f2_ref, b2_ref))):
        r = _dot(xb, w_ref[:, idx * n:(idx + 1) * n])
        for hk in range(KV_SLABS):
            f_ref[pl.ds(hk, tm, stride=KV_SLABS), :] = r[:, hk * HEAD_DIM:(hk + 1) * HEAD_DIM]
        if b_ref is not None:
            b_ref[...] = r.astype(BF)


def _kv_proj(y, w_kv, *, tm):
    r, d = y.shape
    n = w_kv.shape[1] // 3
    flat = jax.ShapeDtypeStruct((r * KV_SLABS, HEAD_DIM), F32)
    fspec = pl.BlockSpec((tm * KV_SLABS, HEAD_DIM), lambda i: (i, 0))
    wide = jax.ShapeDtypeStruct((r, n), BF)
    wspec = pl.BlockSpec((tm, n), lambda i: (i, 0))
    return pl.pallas_call(
        _kv_proj_kernel,
        out_shape=(flat, flat, flat, wide, wide),
        grid=(r // tm,),
        in_specs=[pl.BlockSpec((tm, d), lambda i: (i, 0)),
                  pl.BlockSpec(w_kv.shape, lambda i: (0, 0), pipeline_mode=pl.Buffered(1))],
        out_specs=(fspec, fspec, fspec, wspec, wspec),
        compiler_params=_cparams(1, 48),
        name="kv_proj",
    )(y, w_kv)


def _qg_proj_kernel(h_ref, wq_ref, wg_ref, q_ref, g_ref, *, gates_per_group):
    h = h_ref[...]
    q_ref[...] = _dot(h, wq_ref[...]).astype(BF)
    gates = jax.nn.sigmoid(_dot(h, wg_ref[...]))
    for grp in range(g_ref.shape[1] // LANES):
        shift = (LANES - grp * gates_per_group) % LANES
        g_ref[:, grp * LANES:(grp + 1) * LANES] = gates if shift == 0 else pltpu.roll(gates, shift, 1)


def _qg_proj(h, w_q, w_g, *, tm, gates_per_group):
    r, d = h.shape
    qw = w_q.shape[1]
    gw = N_KV_HEADS * LANES
    resident = lambda w: pl.BlockSpec(w.shape, lambda i: (0, 0), pipeline_mode=pl.Buffered(1))
    return pl.pallas_call(
        functools.partial(_qg_proj_kernel, gates_per_group=gates_per_group),
        out_shape=(jax.ShapeDtypeStruct((r, qw), BF), jax.ShapeDtypeStruct((r, gw), F32)),
        grid=(r // tm,),
        in_specs=[pl.BlockSpec((tm, d), lambda i: (i, 0)), resident(w_q), resident(w_g)],
        out_specs=(pl.BlockSpec((tm, qw), lambda i: (i, 0)),
                   pl.BlockSpec((tm, gw), lambda i: (i, 0))),
        compiler_params=_cparams(1, 40),
        name="qg_proj",
    )(h, w_q, w_g)


PAGES_PER_STEP = 16
CHUNKS_PER_PAGE = PAGE_SIZE // CMP_STRIDE


PAGE_FLAT_ROWS = PAGE_SIZE * KV_SLABS
CHUNK_FLAT_ROWS = CMP_STRIDE * KV_SLABS


def _compress_kernel(pt_ref, *refs, n_steps):
    page_refs = refs[:PAGES_PER_STEP]
    w1_ref, pe_ref, b1_ref, w2_ref, b2_ref, o_ref, p0_scr, p1_scr = refs[PAGES_PER_STEP:]
    t = pl.program_id(1)
    rows_step = PAGES_PER_STEP * CHUNKS_PER_PAGE * KV_SLABS
    n_rows = n_steps * rows_step
    hidden = w2_ref.shape[0]

    @pl.when(t == 0)
    def _():
        p1_scr[pl.ds(n_rows, KV_SLABS), :] = jnp.zeros((KV_SLABS, hidden), F32)

    per_s = []
    for s in range(CMP_STRIDE):
        rows = [page_refs[p][pl.ds(c * CHUNK_FLAT_ROWS + s * KV_SLABS, KV_SLABS), :]
                for p in range(PAGES_PER_STEP) for c in range(CHUNKS_PER_PAGE)]
        per_s.append(jnp.concatenate(rows, axis=0))
    x = jnp.concatenate(per_s, axis=1).astype(BF)
    pr = _dot(x, w1_ref[...])
    is_v = _mod_pow2(lax.broadcasted_iota(jnp.int32, (rows_step, 1), 0), 2) == 1
    pk = jnp.where(is_v, pr[:, 2 * hidden:], pr[:, :2 * hidden])
    row0 = pl.multiple_of(t * rows_step, rows_step)
    p0_scr[pl.ds(row0, rows_step), :] = pk[:, :hidden]
    p1_scr[pl.ds(row0, rows_step), :] = pk[:, hidden:]

    @pl.when(t == n_steps - 1)
    def _():
        cst = _dot(pe_ref[...].astype(BF), w1_ref[...])
        const_k = cst[0:1, :hidden] + cst[1:2, hidden:2 * hidden] + b1_ref[0:1, :]
        const_v = cst[2:3, 2 * hidden:3 * hidden] + cst[3:4, 3 * hidden:] + b1_ref[1:2, :]
        v_rows = _mod_pow2(lax.broadcasted_iota(jnp.int32, (n_rows, 1), 0), 2) == 1
        pre = p0_scr[0:n_rows, :] + p1_scr[KV_SLABS:n_rows + KV_SLABS, :] + jnp.where(v_rows, const_v, const_k)
        o2 = _dot(jax.nn.gelu(pre).astype(BF), w2_ref[...])
        o_ref[...] = jnp.where(v_rows, o2[:, HEAD_DIM:] + b2_ref[1:2, :], o2[:, :HEAD_DIM] + b2_ref[0:1, :])


def _compress(pages_flat, page_table, w1all, pe16, b1, w2all, b2):
    nb, n_pages = page_table.shape
    n_steps = n_pages // PAGES_PER_STEP
    n_rows = n_pages * CHUNKS_PER_PAGE * KV_SLABS
    hidden = w2all.shape[0]

    def page_spec(p):
        return pl.BlockSpec((PAGE_FLAT_ROWS, HEAD_DIM), lambda b, t, pt: (pt[b, t * PAGES_PER_STEP + p], 0))

    full = lambda a: pl.BlockSpec(a.shape, lambda b, t, pt: (0,) * a.ndim)
    grid_spec = pltpu.PrefetchScalarGridSpec(
        num_scalar_prefetch=1,
        grid=(nb, n_steps),
        in_specs=[page_spec(p) for p in range(PAGES_PER_STEP)] + [
            full(w1all), full(pe16), full(b1), full(w2all), full(b2)],
        out_specs=pl.BlockSpec((None, n_rows, HEAD_DIM), lambda b, t, pt: (b, 0, 0)),
        scratch_shapes=[pltpu.VMEM((n_rows, hidden), F32), pltpu.VMEM((n_rows + KV_SLABS, hidden), F32)],
    )
    return pl.pallas_call(
        functools.partial(_compress_kernel, n_steps=n_steps),
        out_shape=jax.ShapeDtypeStruct((nb, n_rows, HEAD_DIM), F32),
        grid_spec=grid_spec,
        compiler_params=_cparams(2, 48),
        name="compress_blocks",
    )(page_table, *([pages_flat] * PAGES_PER_STEP), w1all, pe16, b1, w2all, b2)


def _overlap_map(n_cmp_rows, n_slc, n_cols):
    cmp_lo = np.arange(n_cmp_rows)[:, None] * CMP_STRIDE
    slc_lo = np.arange(n_cols)[None, :] * SEL_BLOCK
    ov = np.minimum(cmp_lo + CMP_LEN, slc_lo + SEL_BLOCK) - np.maximum(cmp_lo, slc_lo)
    w = np.clip(ov, 0, None).astype(np.float32) / CMP_LEN
    w[:, n_slc:] = 0.0
    return w


SEL_CHUNK = 512


def _attn_prompt_kernel(q_ref, gt_ref, cmp_ref, sel_ref, win_ref, wmap_t_ref, exp_ref, o_ref,
                        s_sel, m_run, l_run, e_scr, s_win, m_win, l_win, e_win, *, scale, n_slc, n_qb, n_widths):
    g = pl.program_id(1)
    qb = pl.program_id(2)
    q4 = q_ref[...]
    nq = q4.shape[0]
    rq = q4.shape[1] // HEAD_DIM
    qs = jnp.concatenate([q4[:, r * HEAD_DIM:(r + 1) * HEAD_DIM] for r in range(rq)], axis=0)
    qpos = qb * nq + lax.broadcasted_iota(jnp.int32, (nq, 1), 0)
    rows = [slice(r * nq, (r + 1) * nq) for r in range(rq)]

    head_groups = [rows[:rq // 2], rows[rq // 2:]] if rq % 2 == 0 else [rows]

    def attend(n_chunks, chunk, s_scr, m_run, l_run, e_scr, keys, mask, values):
        width = n_chunks * chunk
        m_run[...] = jnp.full(m_run.shape, -jnp.inf, F32)
        l_run[...] = jnp.zeros(l_run.shape, F32)
        for grp in head_groups:
            g_rows = slice(grp[0].start, grp[-1].stop)
            for c in range(n_chunks):
                s = _dot_t(qs[g_rows], keys(c))
                ok = mask(c)
                for i, rs in enumerate(grp):
                    sm = jnp.where(ok, s[i * nq:(i + 1) * nq], -jnp.inf) * (scale * LOG2_E)
                    s_scr[c, rs, :] = sm
                    mx = m_run[rs, :]
                    for j in range(chunk // LANES):
                        mx = jnp.maximum(mx, sm[:, j * LANES:(j + 1) * LANES])
                    m_run[rs, :] = mx
        outs = []
        for grp in head_groups:
            g_rows = slice(grp[0].start, grp[-1].stop)
            m = jnp.max(m_run[g_rows, :], axis=-1, keepdims=True)
            m = jnp.where(m == -jnp.inf, 0.0, m)
            for c in range(n_chunks):
                for i, rs in enumerate(grp):
                    e = jnp.exp2(s_scr[c, rs, :] - m[i * nq:(i + 1) * nq])
                    ls = l_run[rs, :]
                    for j in range(chunk // LANES):
                        ls = ls + e[:, j * LANES:(j + 1) * LANES]
                    l_run[rs, :] = ls
                    e_scr[rs, c * chunk:(c + 1) * chunk] = e.astype(BF)
            l = jnp.sum(l_run[g_rows, :], axis=-1, keepdims=True)
            outs.append(_dot(e_scr[g_rows, :width], values) * (1.0 / jnp.maximum(l, 1e-30)))
        return jnp.concatenate(outs, axis=0)

    def step(n_sel_chunks, first_q):
        span = WINDOW + nq
        k0 = pl.multiple_of(jnp.maximum(qb * nq - WINDOW, 0), nq)
        kp = k0 + lax.broadcasted_iota(jnp.int32, (1, span), 1)
        valid_w = lax.bitcast_convert_type(qpos - kp, jnp.uint32) < jnp.uint32(WINDOW)
        o_win = attend(1, span, s_win, m_win, l_win, e_win, lambda c: win_ref[pl.ds(k0, span), :HEAD_DIM],
                       lambda c: valid_w, win_ref[pl.ds(k0, span), HEAD_DIM:])

        n_cmp_rows = cmp_ref.shape[0] // KV_SLABS
        kc = cmp_ref[pl.ds(g * 2, n_cmp_rows, stride=KV_SLABS), :].astype(BF)
        vc = cmp_ref[pl.ds(g * 2 + 1, n_cmp_rows, stride=KV_SLABS), :].astype(BF)
        n_io = lax.broadcasted_iota(jnp.int32, (1, n_cmp_rows), 1)
        valid_c = (n_io * CMP_STRIDE + (CMP_LEN - 1)) <= qpos
        s_c = _dot_t(qs, kc) * scale
        ps = [_masked_softmax(s_c[rs], valid_c).astype(BF) for rs in rows]
        o_cmp = _dot(jnp.concatenate(ps, axis=0), vc)

        imp = _dot_t(wmap_t_ref[...], jnp.concatenate(ps, axis=1))
        n_rows = imp.shape[0]
        j_io = lax.broadcasted_iota(jnp.int32, (n_rows, 1), 0)
        q_blk = _div_pow2(qb * nq + lax.broadcasted_iota(jnp.int32, (1, nq), 1), SEL_BLOCK)
        causal = j_io <= q_blk
        forced = (j_io == 0) | (causal & (j_io > q_blk - N_LOCAL_SEL))
        imp = jnp.where(forced, imp + FORCE_BONUS, imp)
        imp = jnp.where(causal, imp, -jnp.inf)
        rank = jnp.zeros(imp.shape, F32)
        for jp in range(n_slc):
            other = imp[jp:jp + 1, :]
            rank = rank + jnp.where(other == imp, jnp.where(jp < j_io, 1.0, 0.0),
                                    jnp.where(other > imp, 1.0, 0.0))
        chosen_t = jnp.where(rank < min(N_SEL, n_slc), jnp.where(imp > -jnp.inf, 1.0, 0.0), 0.0)
        chosen_t = jnp.concatenate([chosen_t, jnp.zeros((LANES - n_rows, nq), F32)], axis=0)
        chosen = chosen_t.T.astype(BF)

        def sel_mask(c):
            in_sel = _dot(chosen, exp_ref[c])
            if (c + 1) * SEL_CHUNK <= first_q:
                return in_sel > 0.5
            t_io = c * SEL_CHUNK + lax.broadcasted_iota(jnp.int32, (1, SEL_CHUNK), 1)
            return jnp.where(t_io <= qpos, in_sel, 0.0) > 0.5

        o_sel = attend(n_sel_chunks, SEL_CHUNK, s_sel, m_run, l_run, e_scr,
                       lambda c: sel_ref[c * SEL_CHUNK:(c + 1) * SEL_CHUNK, :HEAD_DIM], sel_mask,
                       sel_ref[:n_sel_chunks * SEL_CHUNK, HEAD_DIM:])

        gt = gt_ref[...]
        outs = []
        for r, rs in enumerate(rows):
            outs.append(gt[:, 3 * r:3 * r + 1] * o_cmp[rs] + gt[:, 3 * r + 1:3 * r + 2] * o_sel[rs]
                        + gt[:, 3 * r + 2:3 * r + 3] * o_win[rs])
        o_ref[...] = jnp.concatenate(outs, axis=1).astype(BF)

    qb_per_width = n_qb // n_widths
    for wi in range(n_widths):
        n_sel_chunks = -(-(wi + 1) * qb_per_width * nq // SEL_CHUNK)
        pl.when((qb >= wi * qb_per_width) & (qb < (wi + 1) * qb_per_width))(
            functools.partial(step, n_sel_chunks, wi * qb_per_width * nq))


def _attn_prompt(q, gates, cmp_flat, kv_sel, kv_win, *, scale):
    nb, t, qw = q.shape
    n_g = N_KV_HEADS
    gw = qw // n_g
    kvw = 2 * HEAD_DIM
    cmp_rows = cmp_flat.shape[1]
    n_slc = t // SEL_BLOCK
    n_slc_rows = -(-n_slc // 8) * 8
    rq = gw // HEAD_DIM
    n_qb = t // Q_BLOCK
    n_chunks = t // SEL_CHUNK
    wmap_t = jnp.asarray(np.tile(_overlap_map(cmp_rows // KV_SLABS, n_slc, n_slc_rows), (rq, 1)).T, BF)
    key_blk = (np.arange(t) // SEL_BLOCK).reshape(n_chunks, 1, SEL_CHUNK)
    expand = jnp.asarray((key_blk == np.arange(LANES)[None, :, None]).astype(np.float32), BF)
    rows = rq * Q_BLOCK
    n_widths = 4 if n_qb % 4 == 0 else 1
    return pl.pallas_call(
        functools.partial(_attn_prompt_kernel, scale=scale, n_slc=n_slc, n_qb=n_qb, n_widths=n_widths),
        out_shape=jax.ShapeDtypeStruct((nb, t, qw), BF),
        grid=(nb, n_g, n_qb),
        in_specs=[pl.BlockSpec((None, Q_BLOCK, gw), lambda b, g, i: (b, i, g)),
                  pl.BlockSpec((None, Q_BLOCK, LANES), lambda b, g, i: (b, i, g)),
                  pl.BlockSpec((None, cmp_rows, HEAD_DIM), lambda b, g, i: (b, 0, 0)),
                  pl.BlockSpec((None, t, kvw), lambda b, g, i: (b, 0, g)),
                  pl.BlockSpec((None, t, kvw), lambda b, g, i: (b, 0, g)),
                  pl.BlockSpec(wmap_t.shape, lambda b, g, i: (0, 0)),
                  pl.BlockSpec(expand.shape, lambda b, g, i: (0, 0, 0))],
        out_specs=pl.BlockSpec((None, Q_BLOCK, gw), lambda b, g, i: (b, i, g)),
        scratch_shapes=[pltpu.VMEM((n_chunks, rows, SEL_CHUNK), F32),
                        pltpu.VMEM((rows, LANES), F32), pltpu.VMEM((rows, LANES), F32),
                        pltpu.VMEM((rows, t), BF),
                        pltpu.VMEM((1, rows, WINDOW + Q_BLOCK), F32),
                        pltpu.VMEM((rows, LANES), F32), pltpu.VMEM((rows, LANES), F32),
                        pltpu.VMEM((rows, WINDOW + Q_BLOCK), BF)],
        compiler_params=_cparams(3, 48),
        name="attn_prompt",
    )(q, gates, cmp_flat, kv_sel, kv_win, wmap_t, expand)


def _attn_sample_a_kernel(q_ref, cmp_ref, win_ref, wmap_ref, ocmp_ref, owin_ref, idx_ref, val_ref, *,
                          scale, q_pos, n_cmp, n_slc, win_pos0):
    q = q_ref[...]
    n_h = q.shape[0]
    rq = n_h // N_KV_HEADS
    row_g = _div_pow2(lax.broadcasted_iota(jnp.int32, (n_h, 1), 0), rq)
    cmp_rows = cmp_ref.shape[0] // KV_SLABS
    win_rows = win_ref.shape[0] // KV_SLABS
    n_io = lax.broadcasted_iota(jnp.int32, (1, cmp_rows), 1)
    valid_c = (n_io < n_cmp) & ((n_io * CMP_STRIDE + (CMP_LEN - 1)) <= q_pos)
    kp = win_pos0 + lax.broadcasted_iota(jnp.int32, (1, win_rows), 1)
    valid_w = (kp <= q_pos) & (kp > q_pos - WINDOW)
    o_cmp = jnp.zeros((n_h, HEAD_DIM), F32)
    o_win = jnp.zeros((n_h, HEAD_DIM), F32)
    imp = jnp.zeros((n_h, wmap_ref.shape[1]), F32)
    for g in range(N_KV_HEADS):
        in_g = row_g == g
        kc = cmp_ref[pl.ds(2 * g, cmp_rows, stride=KV_SLABS), :].astype(BF)
        vc = cmp_ref[pl.ds(2 * g + 1, cmp_rows, stride=KV_SLABS), :].astype(BF)
        p = _masked_softmax(_dot_t(q, kc) * scale, valid_c).astype(BF)
        o_cmp = jnp.where(in_g, _dot(p, vc), o_cmp)
        imp_rows = _dot(p, wmap_ref[...])
        imp_g = jnp.sum(jnp.where(in_g, imp_rows, 0.0), axis=0, keepdims=True)
        imp = jnp.where(in_g, imp_g, imp)
        kw = win_ref[pl.ds(2 * g, win_rows, stride=KV_SLABS), :].astype(BF)
        vw = win_ref[pl.ds(2 * g + 1, win_rows, stride=KV_SLABS), :].astype(BF)
        pw = _masked_softmax(_dot_t(q, kw) * scale, valid_w).astype(BF)
        o_win = jnp.where(in_g, _dot(pw, vw), o_win)
    ocmp_ref[...] = o_cmp
    owin_ref[...] = o_win

    j_io = lax.broadcasted_iota(jnp.int32, (1, imp.shape[1]), 1)
    q_blk = q_pos // SEL_BLOCK
    causal = j_io <= q_blk
    forced = (j_io == 0) | (causal & (j_io > q_blk - N_LOCAL_SEL))
    imp = jnp.where(forced, imp + FORCE_BONUS, imp)
    imp = jnp.where(causal, imp, -jnp.inf)
    rank = _rank_desc(imp, n_slc)
    jf = j_io.astype(F32)
    k_io = lax.broadcasted_iota(jnp.int32, (1, idx_ref.shape[1]), 1)
    idx = jnp.zeros(idx_ref.shape, F32)
    val = jnp.zeros(idx_ref.shape, F32)
    for k in range(min(N_SEL, n_slc)):
        hit = rank == k
        ik = jnp.sum(jnp.where(hit, jf, 0.0), axis=1, keepdims=True)
        vk = jnp.sum(jnp.where(hit, jnp.where(imp > -jnp.inf, 1.0, 0.0), 0.0), axis=1, keepdims=True)
        idx = jnp.where(k_io == k, ik, idx)
        val = jnp.where(k_io == k, vk, val)
    idx_ref[...] = idx.astype(jnp.int32)
    val_ref[...] = val.astype(jnp.int32)


def _attn_sample_a(q, cmp_blocks, win_rows, *, scale, q_pos, n_cmp, n_slc, win_pos0):
    nb, n_h, d = q.shape
    n_cols = -(-n_slc // LANES) * LANES
    wmap = jnp.asarray(_overlap_map(cmp_blocks.shape[1] // KV_SLABS, n_slc, n_cols), BF)
    o_sds = jax.ShapeDtypeStruct((nb, n_h, d), F32)
    i_sds = jax.ShapeDtypeStruct((nb, n_h, LANES), jnp.int32)
    blk = lambda a: pl.BlockSpec((None,) + a.shape[1:], lambda b: (b, 0, 0))
    ospec = pl.BlockSpec((None, n_h, d), lambda b: (b, 0, 0))
    ispec = pl.BlockSpec((None, n_h, LANES), lambda b: (b, 0, 0))
    return pl.pallas_call(
        functools.partial(_attn_sample_a_kernel, scale=scale, q_pos=q_pos, n_cmp=n_cmp, n_slc=n_slc,
                          win_pos0=win_pos0),
        out_shape=(o_sds, o_sds, i_sds, i_sds),
        grid=(nb,),
        in_specs=[blk(q), blk(cmp_blocks), blk(win_rows), pl.BlockSpec(wmap.shape, lambda b: (0, 0))],
        out_specs=(ospec, ospec, ispec, ispec),
        compiler_params=_cparams(1, 32),
        name="attn_sample_a",
    )(q, cmp_blocks, win_rows, wmap)


def _attn_sample_b_kernel(idx_ref, val_ref, pt_ref, q_ref, new_ref, gt_ref, ocmp_ref, owin_ref, *rest,
                          scale, q_pos, past_len):
    blk_refs = rest[:N_SEL]
    o_ref = rest[N_SEL]
    b = pl.program_id(0)
    g = pl.program_id(1)
    q = q_ref[...]
    n_h = q.shape[0]
    rq = n_h // N_KV_HEADS
    base = (b * N_KV_HEADS + g) * N_SEL
    n_keys = N_SEL * SEL_BLOCK
    lane = lax.broadcasted_iota(jnp.int32, (1, n_keys), 1)
    k_new = new_ref[pl.ds(2 * g, 1), :]
    v_new = new_ref[pl.ds(2 * g + 1, 1), :]
    ks, vs = [], []
    kpos = _mod_pow2(lane, SEL_BLOCK)
    kval = jnp.zeros((1, n_keys), jnp.int32)
    lane_blk = _div_pow2(lane, SEL_BLOCK)
    for k in range(N_SEL):
        blk = idx_ref[base + k]
        is_new = blk * SEL_BLOCK >= past_len
        ks.append(jnp.where(is_new, k_new, blk_refs[k][pl.ds(2 * g, SEL_BLOCK, stride=KV_SLABS), :]).astype(BF))
        vs.append(jnp.where(is_new, v_new, blk_refs[k][pl.ds(2 * g + 1, SEL_BLOCK, stride=KV_SLABS), :]).astype(BF))
        in_k = lane_blk == k
        kpos = kpos + jnp.where(in_k, blk * SEL_BLOCK, 0)
        kval = jnp.where(in_k, val_ref[base + k], kval)
    valid = (kval > 0) & (kpos <= q_pos)
    p = _masked_softmax(_dot_t(q, jnp.concatenate(ks, axis=0)) * scale, valid).astype(BF)
    o_sel = _dot(p, jnp.concatenate(vs, axis=0))
    gt = gt_ref[...]
    in_g = _div_pow2(lax.broadcasted_iota(jnp.int32, (n_h, 1), 0), rq) == g

    @pl.when(g == 0)
    def _():
        o_ref[...] = gt[:, 0:1] * ocmp_ref[...] + gt[:, 2:3] * owin_ref[...]

    o_ref[...] += jnp.where(in_g, gt[:, 1:2] * o_sel, 0.0)


def _attn_sample_b(sel_idx, sel_val, page_table, q, kv_sel_new, gates, o_cmp, o_win, pool_sel, *,
                   scale, q_pos, past_len):
    nb, n_h, d = q.shape
    halves = PAGE_SIZE // SEL_BLOCK
    blk_rows = SEL_BLOCK * KV_SLABS
    pool = pool_sel.reshape(pool_sel.shape[0] // blk_rows, blk_rows, d)

    def blk_spec(k):
        def index_map(b, g, idx, val, pt):
            row0 = jnp.minimum(idx[(b * N_KV_HEADS + g) * N_SEL + k] * SEL_BLOCK, past_len - 1)
            return (pt[b, row0 // PAGE_SIZE] * halves + (row0 % PAGE_SIZE) // SEL_BLOCK, 0, 0)
        return pl.BlockSpec((None, blk_rows, d), index_map)

    per_b = lambda a: pl.BlockSpec((None,) + a.shape[1:], lambda b, g, idx, val, pt: (b, 0, 0))
    new_rows = kv_sel_new.reshape(nb, KV_SLABS, d)
    grid_spec = pltpu.PrefetchScalarGridSpec(
        num_scalar_prefetch=3,
        grid=(nb, N_KV_HEADS),
        in_specs=[per_b(q), per_b(new_rows), per_b(gates), per_b(o_cmp), per_b(o_win)]
        + [blk_spec(k) for k in range(N_SEL)],
        out_specs=pl.BlockSpec((None, n_h, d), lambda b, g, idx, val, pt: (b, 0, 0)),
    )
    return pl.pallas_call(
        functools.partial(_attn_sample_b_kernel, scale=scale, q_pos=q_pos, past_len=past_len),
        out_shape=jax.ShapeDtypeStruct((nb, n_h, d), F32),
        grid_spec=grid_spec,
        compiler_params=_cparams(2, 32),
        name="attn_sample_b",
    )(sel_idx, sel_val, page_table, q, new_rows, gates, o_cmp, o_win, *([pool] * N_SEL))


def kernel(x_prompt, x_sample, cache_cmp_kv, cache_sel_kv, cache_win_kv, page_table, c_prompt, c_sample, w_ada, b_ada, ln_g, ln_b, a_w_in, a_v_g, a_v_b, a_w_s, a_b_s, a_w_out, w_kv, cmp_pe, cmp_w1, cmp_b1, cmp_w2, cmp_b2, b_w_qg, b_w_o, ffn_w_in, ffn_w_out):
    nb, t, d = x_prompt.shape
    ns, tq, _ = x_sample.shape
    assert tq == 1, "the sample path handles one new token per sequence"
    depth = w_ada.shape[0]
    n_a = a_w_in.shape[0]
    alpha = float((2 * depth) ** 0.25)
    scale = float(HEAD_DIM ** -0.5)
    n_pool = cache_cmp_kv.shape[0]
    past_len = page_table.shape[1] * PAGE_SIZE
    kvw = N_KV_HEADS * 2 * HEAD_DIM
    q_width = b_w_o.shape[1]
    n_heads = q_width // HEAD_DIM
    rq = n_heads // N_KV_HEADS

    mod = _ada_modulation(jnp.concatenate([c_prompt, c_sample], axis=0), w_ada, b_ada)

    def mods(layer, sub):
        m = mod[layer * 2 + sub]
        out = []
        for part in range(3):
            sl = m[:, part * d:(part + 1) * d]
            out.append((sl[:nb].reshape(nb, 1, d), sl[nb:].reshape(1, ns, d)))
        return out

    wb = lambda w: w.astype(BF)
    hidden = cmp_w1.shape[-1]
    halves = CMP_LEN // CMP_STRIDE
    assert halves == 2, "a compression block spans two stride-chunks"
    w1all = wb(cmp_w1.reshape(2, halves, CMP_STRIDE, HEAD_DIM, hidden)
               .transpose(2, 3, 0, 1, 4).reshape(CMP_STRIDE * HEAD_DIM, 2 * halves * hidden))
    pe16 = jnp.pad(cmp_pe.reshape(2 * halves, CMP_STRIDE * HEAD_DIM), ((0, 16 - 2 * halves), (0, 0)))
    w2all = wb(cmp_w2.transpose(1, 0, 2).reshape(hidden, 2 * HEAD_DIM))

    a_w_out_b, b_w_o_b, ffn_w_in_b, ffn_w_out_b = wb(a_w_out), wb(b_w_o), wb(ffn_w_in), wb(ffn_w_out)

    tm_p = 512
    tpg_p = t // tm_p
    prompt = dict(tm=tm_p, tiles_per_group=tpg_p)
    sample = dict(tm=ns, tiles_per_group=1)

    xp = x_prompt.reshape(nb * t, d)
    xs = x_sample.reshape(ns, d)
    hp = hs = None
    chunk_v = []
    outs = {}
    for layer in range(depth):
        (sh_p, sh_s), (sc_p, sc_s), (gt_p, gt_s) = mods(layer, 0)
        (fsh_p, fsh_s), (fsc_p, fsc_s), (fgt_p, fgt_s) = mods(layer, 1)
        if layer < n_a:
            w_in = wb(a_w_in[layer])
            ap = _gmlp_in(xp, sc_p, sh_p, w_in, a_v_g[layer], a_v_b[layer], a_w_s[layer], a_b_s[layer],
                          chunked=True, **prompt)
            as_, v_rows = _gmlp_in(xs, sc_s, sh_s, w_in, a_v_g[layer], a_v_b[layer], a_w_s[layer], a_b_s[layer],
                                   chunked=False, **sample)
            chunk_v.append(v_rows.reshape(ns, tq, -1))
            xp, hp = _proj_res_ln(ap, a_w_out_b, xp, gt_p, ln_g[layer, 0], ln_b[layer, 0], (fsc_p, fsh_p),
                                  alpha=alpha, layer=layer, **prompt)
            xs, hs = _proj_res_ln(as_, a_w_out_b, xs, gt_s, ln_g[layer, 0], ln_b[layer, 0], (fsc_s, fsh_s),
                                  alpha=alpha, layer=layer, **sample)
        else:
            if layer == n_a:
                w_kvb = wb(w_kv)
                cmp_p, sel_p, win_p, sel_pb, win_pb = _kv_proj(xp, w_kvb, tm=tm_p)
                cmp_s, sel_s, win_s, _, _ = _kv_proj(xs, w_kvb, tm=ns)
                pages_per_seq_p = t // PAGE_SIZE
                ident = jnp.arange(nb * pages_per_seq_p, dtype=jnp.int32).reshape(nb, pages_per_seq_p)
                cmp_blocks_p = _compress(cmp_p, ident, w1all, pe16, cmp_b1, w2all, cmp_b2)
                assert (past_len + tq) // CMP_STRIDE == past_len // CMP_STRIDE
                cmp_blocks_s = _compress(cache_cmp_kv.reshape(n_pool * PAGE_FLAT_ROWS, HEAD_DIM), page_table,
                                         w1all, pe16, cmp_b1, w2all, cmp_b2)
                n_win = cache_win_kv.shape[1]
                win_keys_s = jnp.concatenate([cache_win_kv.reshape(ns, n_win * KV_SLABS, HEAD_DIM),
                                              win_s.reshape(ns, tq * KV_SLABS, HEAD_DIM)],
                                             axis=1)[:, -n_win * KV_SLABS:]
                outs["kv_p"] = (cmp_p, sel_p, win_p)
                outs["kv_pb"] = (sel_pb.reshape(nb, t, kvw), win_pb.reshape(nb, t, kvw))
                outs["kv_s"] = (cmp_s, sel_s, win_keys_s)
            j = layer - n_a
            w_qg = b_w_qg[j]
            w_q = wb(w_qg[:, :q_width])
            n_gates = w_qg.shape[1] - q_width
            w_gate = wb(jnp.pad(w_qg[:, q_width:], ((0, 0), (0, LANES - n_gates))))
            assert layer > 0, "an attention layer must follow another layer"
            q_p, g_p = _qg_proj(hp, w_q, w_gate, tm=tm_p, gates_per_group=rq * 3)
            q_s, g_s = _qg_proj(hs, w_q, w_gate, tm=ns, gates_per_group=rq * 3)
            o_p = _attn_prompt(q_p.reshape(nb, t, q_width), g_p.reshape(nb, t, N_KV_HEADS * LANES),
                               cmp_blocks_p, outs["kv_pb"][0], outs["kv_pb"][1], scale=scale)
            n_cmp_s = (past_len + tq) // CMP_STRIDE - CMP_LEN // CMP_STRIDE + 1
            n_slc_s = -(-(past_len + tq) // SEL_BLOCK)
            q_s3 = q_s.reshape(ns, n_heads, HEAD_DIM)
            o_cmp_s, o_win_s, idx_s, val_s = _attn_sample_a(
                q_s3, cmp_blocks_s, outs["kv_s"][2], scale=scale, q_pos=past_len, n_cmp=n_cmp_s, n_slc=n_slc_s,
                win_pos0=past_len + tq - outs["kv_s"][2].shape[1] // KV_SLABS)
            sel_idx = idx_s[:, ::rq, :N_SEL].reshape(-1)
            sel_val = val_s[:, ::rq, :N_SEL].reshape(-1)
            gates_s = g_s.reshape(ns, N_KV_HEADS, LANES)[:, :, :rq * 3].reshape(ns, n_heads, 3)
            o_s = _attn_sample_b(sel_idx, sel_val, page_table, q_s3, outs["kv_s"][1], gates_s, o_cmp_s, o_win_s,
                                 cache_sel_kv.reshape(n_pool * PAGE_FLAT_ROWS, HEAD_DIM), scale=scale,
                                 q_pos=past_len, past_len=past_len)
            xp, hp = _proj_res_ln(o_p.reshape(nb * t, q_width), b_w_o_b, xp, gt_p, ln_g[layer, 0], ln_b[layer, 0],
                                  (fsc_p, fsh_p), alpha=alpha, layer=j, **prompt)
            xs, hs = _proj_res_ln(o_s.reshape(ns, q_width), b_w_o_b, xs, gt_s, ln_g[layer, 0], ln_b[layer, 0],
                                  (fsc_s, fsh_s), alpha=alpha, layer=j, **sample)
        fp = _ffn_in(hp, ffn_w_in_b, layer, tm=2 * tm_p)
        fs = _ffn_in(hs, ffn_w_in_b, layer, tm=ns)
        nxt_p = nxt_s = None
        if n_a <= layer + 1 < depth:
            (nsh_p, nsh_s), (nsc_p, nsc_s), _ = mods(layer + 1, 0)
            nxt_p, nxt_s = (nsc_p, nsh_p), (nsc_s, nsh_s)
        tm_f = tm_p // 2
        xp, hp = _proj_res_ln(fp, ffn_w_out_b, xp, fgt_p, ln_g[layer, 1], ln_b[layer, 1], nxt_p, alpha=alpha,
                              layer=layer, tm=tm_f, tiles_per_group=t // tm_f)
        xs, hs = _proj_res_ln(fs, ffn_w_out_b, xs, fgt_s, ln_g[layer, 1], ln_b[layer, 1], nxt_s, alpha=alpha,
                              layer=layer, **sample)

    cmp_p, sel_p, win_p = outs["kv_p"]
    cmp_s, sel_s, win_keys_s = outs["kv_s"]
    kv_shape = (N_KV_HEADS, 2, HEAD_DIM)
    n_win_p = min(WINDOW, t)
    return (xp.reshape(nb, t, d), xs.reshape(ns, tq, d),
            cmp_p.reshape((nb, t) + kv_shape), sel_p.reshape((nb, t) + kv_shape),
            win_p.reshape((nb, t) + kv_shape)[:, -n_win_p:],
            cmp_s.reshape((ns, tq) + kv_shape), sel_s.reshape((ns, tq) + kv_shape),
            win_keys_s.reshape((ns, -1) + kv_shape),
            jnp.stack(chunk_v))
```

```python
import functools

import numpy as np
import jax
import jax.numpy as jnp
from jax import lax
from jax.experimental import pallas as pl
from jax.experimental.pallas import tpu as pltpu

BF = jnp.bfloat16
F32 = jnp.float32

LN_EPS = 1e-5
CHUNK = 128
HEAD_DIM = 128
N_KV_HEADS = 4
CMP_LEN = 32
CMP_STRIDE = 16
SEL_BLOCK = 64
N_SEL = 16
N_LOCAL_SEL = 2
FORCE_BONUS = 1e4
LOG2_E = 1.4426950408889634
WINDOW = 512
Q_BLOCK = 256
PAGE_SIZE = 128

LANES = 128
MIB = 1024 * 1024


def _cparams(n_axes, vmem_mib):
    return pltpu.CompilerParams(dimension_semantics=("arbitrary",) * n_axes,
                                vmem_limit_bytes=vmem_mib * MIB)


def _dot(a, b):
    return jnp.dot(a, b, preferred_element_type=F32)


def _dot_t(a, b):
    return lax.dot_general(a, b, (((1,), (1,)), ((), ())), preferred_element_type=F32)


def _layer_norm(x, g, b):
    mu = jnp.mean(x, axis=-1, keepdims=True)
    xc = x - mu
    var = jnp.mean(xc * xc, axis=-1, keepdims=True)
    return xc * lax.rsqrt(var + LN_EPS) * g + b


def _masked_softmax(s, valid):
    s = jnp.where(valid, s, -jnp.inf)
    m = jnp.max(s, axis=-1, keepdims=True)
    m = jnp.where(m == -jnp.inf, 0.0, m)
    e = jnp.exp(s - m)
    d = jnp.maximum(jnp.sum(e, axis=-1, keepdims=True), 1e-30)
    return e * (1.0 / d)


def _log2(n):
    assert n & (n - 1) == 0, "power of two expected"
    return n.bit_length() - 1


def _div_pow2(x, n):
    return lax.shift_right_logical(x, jnp.int32(_log2(n)))


def _mod_pow2(x, n):
    assert n & (n - 1) == 0, "power of two expected"
    return lax.bitwise_and(x, jnp.int32(n - 1))


def _rank_desc(imp, n):
    j_io = lax.broadcasted_iota(jnp.int32, (1, imp.shape[1]), 1)
    rank = jnp.zeros(imp.shape, F32)
    for jp in range(n):
        col = imp[:, jp:jp + 1]
        before = jnp.where(col == imp, jnp.where(jp < j_io, 1.0, 0.0), jnp.where(col > imp, 1.0, 0.0))
        rank = rank + before
    return rank


def _ada_kernel(c_ref, w_ref, b_ref, o_ref):
    a = jax.nn.silu(c_ref[...]).astype(BF)
    o_ref[...] = _dot(a, w_ref[...].astype(BF)) + b_ref[...]


def _ada_modulation(c_all, w_ada, b_ada):
    n_sub = w_ada.shape[0] * w_ada.shape[1]
    d, n3 = w_ada.shape[2], w_ada.shape[3]
    m = c_all.shape[0]
    tn = 512
    w = w_ada.reshape(n_sub, d, n3)
    b = b_ada.reshape(n_sub, 1, n3)
    return pl.pallas_call(
        _ada_kernel,
        out_shape=jax.ShapeDtypeStruct((n_sub, m, n3), F32),
        grid=(n_sub, n3 // tn),
        in_specs=[pl.BlockSpec((m, d), lambda s, j: (0, 0)),
                  pl.BlockSpec((None, d, tn), lambda s, j: (s, 0, j)),
                  pl.BlockSpec((None, 1, tn), lambda s, j: (s, 0, j))],
        out_specs=pl.BlockSpec((None, m, tn), lambda s, j: (s, 0, j)),
        compiler_params=_cparams(2, 32),
        name="ada_modulation",
    )(c_all, w, b)


def _mod_spec(mod, tiles_per_group):
    rb, d = mod.shape[1], mod.shape[2]
    return pl.BlockSpec((None, rb, d), lambda i: (i // tiles_per_group, 0, 0))


def _gmlp_in_kernel(x_ref, sc_ref, sh_ref, w_ref, vg_ref, vb_ref, ws_ref, bs_ref, *rest,
                    n_j, tn, chunked):
    if chunked:
        o_ref, z_scr = rest
    else:
        o_ref, v_ref, z_scr = rest
    h = (x_ref[...] * (1.0 + sc_ref[...]) + sh_ref[...]).astype(BF)
    for jj in range(n_j):
        z_scr[jj] = jax.nn.gelu(_dot(h, w_ref[:, jj * tn:(jj + 1) * tn]))

    tm = z_scr.shape[1]
    half = n_j // 2
    e_a = half * tn
    s1 = jnp.zeros((tm, 1), F32)
    for jj in range(half):
        s1 = s1 + jnp.sum(z_scr[half + jj], axis=-1, keepdims=True)
    mu = s1 / e_a
    s2 = jnp.zeros((tm, 1), F32)
    for jj in range(half):
        vc = z_scr[half + jj] - mu
        s2 = s2 + jnp.sum(vc * vc, axis=-1, keepdims=True)
    rstd = lax.rsqrt(s2 / e_a + LN_EPS)
    gpt = tn // CHUNK
    if chunked:
        row = lax.broadcasted_iota(jnp.int32, (CHUNK, CHUNK), 0)
        col = lax.broadcasted_iota(jnp.int32, (CHUNK, CHUNK), 1)
        for jj in range(half):
            for gi in range(gpt):
                g = jj * gpt + gi
                cs = slice(gi * CHUNK, (gi + 1) * CHUNK)
                wm = jnp.where(row >= col, ws_ref[g], 0.0).astype(BF)
                bcol = bs_ref[:, g:g + 1]
                vg = vg_ref[:, g * CHUNK:(g + 1) * CHUNK]
                vb = vb_ref[:, g * CHUNK:(g + 1) * CHUNK]
                for c in range(tm // CHUNK):
                    rs = slice(c * CHUNK, (c + 1) * CHUNK)
                    vn = (z_scr[half + jj, rs, cs] - mu[rs]) * rstd[rs] * vg + vb
                    mixed = _dot(wm, vn.astype(BF)) + bcol
                    o_ref[rs, g * CHUNK:(g + 1) * CHUNK] = (z_scr[jj, rs, cs] * mixed).astype(BF)
    else:
        for jj in range(half):
            cs = slice(jj * tn, (jj + 1) * tn)
            vn = (z_scr[half + jj] - mu) * rstd * vg_ref[:, cs] + vb_ref[:, cs]
            v_ref[:, cs] = vn
            mixed = ws_ref[:, cs] * vn + bs_ref[:, cs]
            o_ref[:, cs] = (z_scr[jj] * mixed).astype(BF)


def _gmlp_in(x, sc, sh, w_in, v_g, v_b, w_s, b_s, *, tm, tiles_per_group, chunked):
    r, d = x.shape
    n2 = w_in.shape[1]
    e_a = n2 // 2
    tn = 512
    n_j = n2 // tn
    n_g = w_s.shape[0]
    full = lambda a: pl.BlockSpec(a.shape, lambda i: (0,) * a.ndim)
    row_tile = pl.BlockSpec((tm, e_a), lambda i: (i, 0))
    if chunked:
        ws_arr = w_s
        bs_arr = b_s.T
        out_shape = jax.ShapeDtypeStruct((r, e_a), BF)
        out_specs = row_tile
    else:
        ws_arr = jnp.repeat(w_s[:, 0, 0], e_a // n_g).reshape(1, e_a)
        bs_arr = jnp.repeat(b_s[:, 0], e_a // n_g).reshape(1, e_a)
        out_shape = (jax.ShapeDtypeStruct((r, e_a), BF), jax.ShapeDtypeStruct((r, e_a), F32))
        out_specs = (row_tile, row_tile)
    vg2, vb2 = v_g.reshape(1, e_a), v_b.reshape(1, e_a)
    return pl.pallas_call(
        functools.partial(_gmlp_in_kernel, n_j=n_j, tn=tn, chunked=chunked),
        out_shape=out_shape,
        grid=(r // tm,),
        in_specs=[pl.BlockSpec((tm, d), lambda i: (i, 0)),
                  _mod_spec(sc, tiles_per_group), _mod_spec(sh, tiles_per_group),
                  pl.BlockSpec(w_in.shape, lambda i: (0, 0), pipeline_mode=pl.Buffered(1)),
                  full(vg2), full(vb2), full(ws_arr), full(bs_arr)],
        out_specs=out_specs,
        scratch_shapes=[pltpu.VMEM((n_j, tm, tn), F32)],
        compiler_params=_cparams(1, 48),
        name="gmlp_in_chunked" if chunked else "gmlp_in_single",
    )(x, sc, sh, w_in, vg2, vb2, ws_arr, bs_arr)


LN_ROWS = 16


def _proj_res_ln_kernel(a_ref, w_ref, x_ref, gate_ref, lg_ref, lb_ref, *rest, alpha, emit_next):
    if emit_next:
        sc_ref, sh_ref, y_ref, h_ref, acc = rest
    else:
        y_ref, acc = rest
    acc[...] = _dot(a_ref[...].astype(BF), w_ref[...])
    tm = acc.shape[0]
    per_row_mod = gate_ref.shape[0] > 1
    step = min(LN_ROWS, tm)
    assert tm % step == 0
    for c in range(tm // step):
        rs = slice(c * step, (c + 1) * step)
        ms = rs if per_row_mod else slice(None)
        t = alpha * x_ref[rs, :] + (1.0 + gate_ref[ms, :]) * acc[rs, :]
        y = _layer_norm(t, lg_ref[...], lb_ref[...])
        y_ref[rs, :] = y
        if emit_next:
            h_ref[rs, :] = (y * (1.0 + sc_ref[ms, :]) + sh_ref[ms, :]).astype(BF)


def _proj_res_ln(a, w, x, gate, ln_g, ln_b, nxt, *, alpha, tm, tiles_per_group, layer=0):
    r, kdim = a.shape
    d = w.shape[2]
    emit_next = nxt is not None
    mod_spec = lambda m: _mod_spec(m, tiles_per_group)
    in_specs = [pl.BlockSpec((tm, kdim), lambda i: (i, 0)),
                pl.BlockSpec((None, kdim, d), lambda i: (layer, 0, 0), pipeline_mode=pl.Buffered(1)),
                pl.BlockSpec((tm, d), lambda i: (i, 0)),
                mod_spec(gate),
                pl.BlockSpec((1, d), lambda i: (0, 0)),
                pl.BlockSpec((1, d), lambda i: (0, 0))]
    args = [a, w, x, gate, ln_g.reshape(1, d), ln_b.reshape(1, d)]
    out_shape = [jax.ShapeDtypeStruct((r, d), F32)]
    out_specs = [pl.BlockSpec((tm, d), lambda i: (i, 0))]
    if emit_next:
        in_specs += [mod_spec(nxt[0]), mod_spec(nxt[1])]
        args += [nxt[0], nxt[1]]
        out_shape.append(jax.ShapeDtypeStruct((r, d), BF))
        out_specs.append(pl.BlockSpec((tm, d), lambda i: (i, 0)))
    res = pl.pallas_call(
        functools.partial(_proj_res_ln_kernel, alpha=alpha, emit_next=emit_next),
        out_shape=tuple(out_shape),
        grid=(r // tm,),
        in_specs=in_specs,
        out_specs=tuple(out_specs),
        scratch_shapes=[pltpu.VMEM((tm, d), F32)],
        compiler_params=_cparams(1, 56),
        name="proj_res_ln",
    )(*args)
    return res if emit_next else (res[0], None)


def _ffn_in_kernel(h_ref, wg_ref, wu_ref, o_ref):
    h = h_ref[...]
    o_ref[...] = (jax.nn.silu(_dot(h, wg_ref[...])) * _dot(h, wu_ref[...])).astype(BF)


def _ffn_in(h, w_in, layer, *, tm):
    r, d = h.shape
    d_ff = w_in.shape[2] // 2
    tn = 512
    n_j = d_ff // tn
    return pl.pallas_call(
        _ffn_in_kernel,
        out_shape=jax.ShapeDtypeStruct((r, d_ff), BF),
        grid=(r // tm, n_j),
        in_specs=[pl.BlockSpec((tm, d), lambda i, j: (i, 0)),
                  pl.BlockSpec((None, d, tn), lambda i, j: (layer, 0, j)),
                  pl.BlockSpec((None, d, tn), lambda i, j: (layer, 0, j + n_j))],
        out_specs=pl.BlockSpec((tm, tn), lambda i, j: (i, j)),
        compiler_params=_cparams(2, 40),
        name="ffn_in",
    )(h, w_in, w_in)


KV_SLABS = N_KV_HEADS * 2


def _kv_proj_kernel(y_ref, w_ref, f0_ref, f1_ref, f2_ref, b1_ref, b2_ref):
    xb = y_ref[...].astype(BF)
    tm = xb.shape[0]
    n = w_ref.shape[1] // 3
    for idx, (f_ref, b_ref) in enumerate(((f0_ref, None), (f1_ref, b1_ref), (f2_ref, b2_ref))):
        r = _dot(xb, w_ref[:, idx * n:(idx + 1) * n])
        for hk in range(KV_SLABS):
            f_ref[pl.ds(hk, tm, stride=KV_SLABS), :] = r[:, hk * HEAD_DIM:(hk + 1) * HEAD_DIM]
        if b_ref is not None:
            b_ref[...] = r.astype(BF)


def _kv_proj(y, w_kv, *, tm):
    r, d = y.shape
    n = w_kv.shape[1] // 3
    flat = jax.ShapeDtypeStruct((r * KV_SLABS, HEAD_DIM), F32)
    fspec = pl.BlockSpec((tm * KV_SLABS, HEAD_DIM), lambda i: (i, 0))
    wide = jax.ShapeDtypeStruct((r, n), BF)
    wspec = pl.BlockSpec((tm, n), lambda i: (i, 0))
    return pl.pallas_call(
        _kv_proj_kernel,
        out_shape=(flat, flat, flat, wide, wide),
        grid=(r // tm,),
        in_specs=[pl.BlockSpec((tm, d), lambda i: (i, 0)),
                  pl.BlockSpec(w_kv.shape, lambda i: (0, 0), pipeline_mode=pl.Buffered(1))],
        out_specs=(fspec, fspec, fspec, wspec, wspec),
        compiler_params=_cparams(1, 48),
        name="kv_proj",
    )(y, w_kv)


def _qg_proj_kernel(h_ref, wq_ref, wg_ref, q_ref, g_ref, *, gates_per_group):
    h = h_ref[...]
    q_ref[...] = _dot(h, wq_ref[...]).astype(BF)
    gates = jax.nn.sigmoid(_dot(h, wg_ref[...]))
    for grp in range(g_ref.shape[1] // LANES):
        shift = (LANES - grp * gates_per_group) % LANES
        g_ref[:, grp * LANES:(grp + 1) * LANES] = gates if shift == 0 else pltpu.roll(gates, shift, 1)


def _qg_proj(h, w_q, w_g, *, tm, gates_per_group):
    r, d = h.shape
    qw = w_q.shape[1]
    gw = N_KV_HEADS * LANES
    resident = lambda w: pl.BlockSpec(w.shape, lambda i: (0, 0), pipeline_mode=pl.Buffered(1))
    return pl.pallas_call(
        functools.partial(_qg_proj_kernel, gates_per_group=gates_per_group),
        out_shape=(jax.ShapeDtypeStruct((r, qw), BF), jax.ShapeDtypeStruct((r, gw), F32)),
        grid=(r // tm,),
        in_specs=[pl.BlockSpec((tm, d), lambda i: (i, 0)), resident(w_q), resident(w_g)],
        out_specs=(pl.BlockSpec((tm, qw), lambda i: (i, 0)),
                   pl.BlockSpec((tm, gw), lambda i: (i, 0))),
        compiler_params=_cparams(1, 40),
        name="qg_proj",
    )(h, w_q, w_g)


PAGES_PER_STEP = 16
CHUNKS_PER_PAGE = PAGE_SIZE // CMP_STRIDE


PAGE_FLAT_ROWS = PAGE_SIZE * KV_SLABS
CHUNK_FLAT_ROWS = CMP_STRIDE * KV_SLABS


def _compress_kernel(pt_ref, *refs, n_steps):
    page_refs = refs[:PAGES_PER_STEP]
    w1_ref, pe_ref, b1_ref, w2_ref, b2_ref, o_ref, p0_scr, p1_scr = refs[PAGES_PER_STEP:]
    t = pl.program_id(1)
    rows_step = PAGES_PER_STEP * CHUNKS_PER_PAGE * N_KV_HEADS
    n_rows = n_steps * rows_step
    hidden = w2_ref.shape[0]

    @pl.when(t == 0)
    def _():
        p1_scr[:, pl.ds(n_rows, 8), :] = jnp.zeros((2, 8, hidden), F32)

    row0 = pl.multiple_of(t * rows_step, rows_step)
    for k in range(2):
        per_s = []
        for s in range(CMP_STRIDE):
            rows = [page_refs[p][pl.ds(c * CHUNK_FLAT_ROWS + s * KV_SLABS + k, N_KV_HEADS, stride=2), :]
                    for p in range(PAGES_PER_STEP) for c in range(CHUNKS_PER_PAGE)]
            per_s.append(jnp.concatenate(rows, axis=0))
        x = jnp.concatenate(per_s, axis=1).astype(BF)
        pr = _dot(x, w1_ref[:, k * 2 * hidden:(k + 1) * 2 * hidden])
        p0_scr[k, pl.ds(row0, rows_step), :] = pr[:, :hidden]
        p1_scr[k, pl.ds(row0, rows_step), :] = pr[:, hidden:]

    @pl.when(t == n_steps - 1)
    def _():
        cst = _dot(pe_ref[...].astype(BF), w1_ref[...])
        for k in range(2):
            c0 = 2 * k * hidden
            const = (cst[2 * k:2 * k + 1, c0:c0 + hidden] + cst[2 * k + 1:2 * k + 2, c0 + hidden:c0 + 2 * hidden]
                     + b1_ref[k:k + 1, :])
            pre = p0_scr[k, 0:n_rows, :] + p1_scr[k, N_KV_HEADS:n_rows + N_KV_HEADS, :] + const
            res = _dot(jax.nn.gelu(pre).astype(BF), w2_ref[:, k * HEAD_DIM:(k + 1) * HEAD_DIM]) + b2_ref[k:k + 1, :]
            o_ref[pl.ds(k, n_rows, stride=2), :] = res


def _compress(pages_flat, page_table, w1all, pe16, b1, w2all, b2):
    nb, n_pages = page_table.shape
    n_steps = n_pages // PAGES_PER_STEP
    n_rows = n_pages * CHUNKS_PER_PAGE * KV_SLABS
    hidden = w2all.shape[0]

    def page_spec(p):
        return pl.BlockSpec((PAGE_FLAT_ROWS, HEAD_DIM), lambda b, t, pt: (pt[b, t * PAGES_PER_STEP + p], 0))

    full = lambda a: pl.BlockSpec(a.shape, lambda b, t, pt: (0,) * a.ndim)
    grid_spec = pltpu.PrefetchScalarGridSpec(
        num_scalar_prefetch=1,
        grid=(nb, n_steps),
        in_specs=[page_spec(p) for p in range(PAGES_PER_STEP)] + [
            full(w1all), full(pe16), full(b1), full(w2all), full(b2)],
        out_specs=pl.BlockSpec((None, n_rows, HEAD_DIM), lambda b, t, pt: (b, 0, 0)),
        scratch_shapes=[pltpu.VMEM((2, n_rows // 2, hidden), F32), pltpu.VMEM((2, n_rows // 2 + 8, hidden), F32)],
    )
    return pl.pallas_call(
        functools.partial(_compress_kernel, n_steps=n_steps),
        out_shape=jax.ShapeDtypeStruct((nb, n_rows, HEAD_DIM), F32),
        grid_spec=grid_spec,
        compiler_params=_cparams(2, 48),
        name="compress_blocks",
    )(page_table, *([pages_flat] * PAGES_PER_STEP), w1all, pe16, b1, w2all, b2)


def _overlap_map(n_cmp_rows, n_slc, n_cols):
    cmp_lo = np.arange(n_cmp_rows)[:, None] * CMP_STRIDE
    slc_lo = np.arange(n_cols)[None, :] * SEL_BLOCK
    ov = np.minimum(cmp_lo + CMP_LEN, slc_lo + SEL_BLOCK) - np.maximum(cmp_lo, slc_lo)
    w = np.clip(ov, 0, None).astype(np.float32) / CMP_LEN
    w[:, n_slc:] = 0.0
    return w


SEL_CHUNK = 512


def _attn_prompt_kernel(q_ref, gt_ref, cmp_ref, sel_ref, win_ref, wmap_t_ref, exp_ref, o_ref,
                        s_sel, m_run, l_run, e_scr, s_win, m_win, l_win, e_win, *, scale, n_slc, n_qb, n_widths):
    g = pl.program_id(1)
    qb = pl.program_id(2)
    q4 = q_ref[...]
    nq = q4.shape[0]
    rq = q4.shape[1] // HEAD_DIM
    qs = jnp.concatenate([q4[:, r * HEAD_DIM:(r + 1) * HEAD_DIM] for r in range(rq)], axis=0)
    qpos = qb * nq + lax.broadcasted_iota(jnp.int32, (nq, 1), 0)
    rows = [slice(r * nq, (r + 1) * nq) for r in range(rq)]

    head_groups = [rows[:rq // 2], rows[rq // 2:]] if rq % 2 == 0 else [rows]

    def attend(n_chunks, chunk, s_scr, m_run, l_run, e_scr, keys, mask, values):
        width = n_chunks * chunk
        m_run[...] = jnp.full(m_run.shape, -jnp.inf, F32)
        l_run[...] = jnp.zeros(l_run.shape, F32)
        for grp in head_groups:
            g_rows = slice(grp[0].start, grp[-1].stop)
            for c in range(n_chunks):
                s = _dot_t(qs[g_rows], keys(c))
                ok = mask(c)
                for i, rs in enumerate(grp):
                    sm = jnp.where(ok, s[i * nq:(i + 1) * nq], -jnp.inf) * (scale * LOG2_E)
                    s_scr[c, rs, :] = sm
                    mx = m_run[rs, :]
                    for j in range(chunk // LANES):
                        mx = jnp.maximum(mx, sm[:, j * LANES:(j + 1) * LANES])
                    m_run[rs, :] = mx
        outs = []
        for grp in head_groups:
            g_rows = slice(grp[0].start, grp[-1].stop)
            m = jnp.max(m_run[g_rows, :], axis=-1, keepdims=True)
            m = jnp.where(m == -jnp.inf, 0.0, m)
            for c in range(n_chunks):
                for i, rs in enumerate(grp):
                    e = jnp.exp2(s_scr[c, rs, :] - m[i * nq:(i + 1) * nq])
                    ls = l_run[rs, :]
                    for j in range(chunk // LANES):
                        ls = ls + e[:, j * LANES:(j + 1) * LANES]
                    l_run[rs, :] = ls
                    e_scr[rs, c * chunk:(c + 1) * chunk] = e.astype(BF)
            l = jnp.sum(l_run[g_rows, :], axis=-1, keepdims=True)
            outs.append(_dot(e_scr[g_rows, :width], values) * (1.0 / jnp.maximum(l, 1e-30)))
        return jnp.concatenate(outs, axis=0)

    def step(n_sel_chunks, first_q):
        span = WINDOW + nq
        k0 = pl.multiple_of(jnp.maximum(qb * nq - WINDOW, 0), nq)
        kp = k0 + lax.broadcasted_iota(jnp.int32, (1, span), 1)
        valid_w = lax.bitcast_convert_type(qpos - kp, jnp.uint32) < jnp.uint32(WINDOW)
        o_win = attend(1, span, s_win, m_win, l_win, e_win, lambda c: win_ref[pl.ds(k0, span), :HEAD_DIM],
                       lambda c: valid_w, win_ref[pl.ds(k0, span), HEAD_DIM:])

        n_cmp_rows = cmp_ref.shape[0] // KV_SLABS
        kc = cmp_ref[pl.ds(g * 2, n_cmp_rows, stride=KV_SLABS), :].astype(BF)
        vc = cmp_ref[pl.ds(g * 2 + 1, n_cmp_rows, stride=KV_SLABS), :].astype(BF)
        n_io = lax.broadcasted_iota(jnp.int32, (1, n_cmp_rows), 1)
        valid_c = (n_io * CMP_STRIDE + (CMP_LEN - 1)) <= qpos
        s_c = _dot_t(qs, kc) * scale
        ps = [_masked_softmax(s_c[rs], valid_c).astype(BF) for rs in rows]
        o_cmp = _dot(jnp.concatenate(ps, axis=0), vc)

        imp = _dot_t(wmap_t_ref[...], jnp.concatenate(ps, axis=1))
        n_rows = imp.shape[0]
        j_io = lax.broadcasted_iota(jnp.int32, (n_rows, 1), 0)
        q_blk = _div_pow2(qb * nq + lax.broadcasted_iota(jnp.int32, (1, nq), 1), SEL_BLOCK)
        causal = j_io <= q_blk
        forced = (j_io == 0) | (causal & (j_io > q_blk - N_LOCAL_SEL))
        imp = jnp.where(forced, imp + FORCE_BONUS, imp)
        imp = jnp.where(causal, imp, -jnp.inf)
        rank = jnp.zeros(imp.shape, F32)
        for jp in range(n_slc):
            other = imp[jp:jp + 1, :]
            rank = rank + jnp.where(other == imp, jnp.where(jp < j_io, 1.0, 0.0),
                                    jnp.where(other > imp, 1.0, 0.0))
        chosen_t = jnp.where(rank < min(N_SEL, n_slc), jnp.where(imp > -jnp.inf, 1.0, 0.0), 0.0)
        chosen_t = jnp.concatenate([chosen_t, jnp.zeros((LANES - n_rows, nq), F32)], axis=0)
        chosen = chosen_t.T.astype(BF)

        def sel_mask(c):
            in_sel = _dot(chosen, exp_ref[c])
            if (c + 1) * SEL_CHUNK <= first_q:
                return in_sel > 0.5
            t_io = c * SEL_CHUNK + lax.broadcasted_iota(jnp.int32, (1, SEL_CHUNK), 1)
            return jnp.where(t_io <= qpos, in_sel, 0.0) > 0.5

        o_sel = attend(n_sel_chunks, SEL_CHUNK, s_sel, m_run, l_run, e_scr,
                       lambda c: sel_ref[c * SEL_CHUNK:(c + 1) * SEL_CHUNK, :HEAD_DIM], sel_mask,
                       sel_ref[:n_sel_chunks * SEL_CHUNK, HEAD_DIM:])

        gt = gt_ref[...]
        outs = []
        for r, rs in enumerate(rows):
            outs.append(gt[:, 3 * r:3 * r + 1] * o_cmp[rs] + gt[:, 3 * r + 1:3 * r + 2] * o_sel[rs]
                        + gt[:, 3 * r + 2:3 * r + 3] * o_win[rs])
        o_ref[...] = jnp.concatenate(outs, axis=1).astype(BF)

    qb_per_width = n_qb // n_widths
    for wi in range(n_widths):
        n_sel_chunks = -(-(wi + 1) * qb_per_width * nq // SEL_CHUNK)
        pl.when((qb >= wi * qb_per_width) & (qb < (wi + 1) * qb_per_width))(
            functools.partial(step, n_sel_chunks, wi * qb_per_width * nq))


def _attn_prompt(q, gates, cmp_flat, kv_sel, kv_win, *, scale):
    nb, t, qw = q.shape
    n_g = N_KV_HEADS
    gw = qw // n_g
    kvw = 2 * HEAD_DIM
    cmp_rows = cmp_flat.shape[1]
    n_slc = t // SEL_BLOCK
    n_slc_rows = -(-n_slc // 8) * 8
    rq = gw // HEAD_DIM
    n_qb = t // Q_BLOCK
    n_chunks = t // SEL_CHUNK
    wmap_t = jnp.asarray(np.tile(_overlap_map(cmp_rows // KV_SLABS, n_slc, n_slc_rows), (rq, 1)).T, BF)
    key_blk = (np.arange(t) // SEL_BLOCK).reshape(n_chunks, 1, SEL_CHUNK)
    expand = jnp.asarray((key_blk == np.arange(LANES)[None, :, None]).astype(np.float32), BF)
    rows = rq * Q_BLOCK
    n_widths = 4 if n_qb % 4 == 0 else 1
    return pl.pallas_call(
        functools.partial(_attn_prompt_kernel, scale=scale, n_slc=n_slc, n_qb=n_qb, n_widths=n_widths),
        out_shape=jax.ShapeDtypeStruct((nb, t, qw), BF),
        grid=(nb, n_g, n_qb),
        in_specs=[pl.BlockSpec((None, Q_BLOCK, gw), lambda b, g, i: (b, i, g)),
                  pl.BlockSpec((None, Q_BLOCK, LANES), lambda b, g, i: (b, i, g)),
                  pl.BlockSpec((None, cmp_rows, HEAD_DIM), lambda b, g, i: (b, 0, 0)),
                  pl.BlockSpec((None, t, kvw), lambda b, g, i: (b, 0, g)),
                  pl.BlockSpec((None, t, kvw), lambda b, g, i: (b, 0, g)),
                  pl.BlockSpec(wmap_t.shape, lambda b, g, i: (0, 0)),
                  pl.BlockSpec(expand.shape, lambda b, g, i: (0, 0, 0))],
        out_specs=pl.BlockSpec((None, Q_BLOCK, gw), lambda b, g, i: (b, i, g)),
        scratch_shapes=[pltpu.VMEM((n_chunks, rows, SEL_CHUNK), F32),
                        pltpu.VMEM((rows, LANES), F32), pltpu.VMEM((rows, LANES), F32),
                        pltpu.VMEM((rows, t), BF),
                        pltpu.VMEM((1, rows, WINDOW + Q_BLOCK), F32),
                        pltpu.VMEM((rows, LANES), F32), pltpu.VMEM((rows, LANES), F32),
                        pltpu.VMEM((rows, WINDOW + Q_BLOCK), BF)],
        compiler_params=_cparams(3, 48),
        name="attn_prompt",
    )(q, gates, cmp_flat, kv_sel, kv_win, wmap_t, expand)


def _attn_sample_a_kernel(q_ref, cmp_ref, win_ref, wmap_ref, ocmp_ref, owin_ref, idx_ref, val_ref, *,
                          scale, q_pos, n_cmp, n_slc, win_pos0):
    q = q_ref[...]
    n_h = q.shape[0]
    rq = n_h // N_KV_HEADS
    row_g = _div_pow2(lax.broadcasted_iota(jnp.int32, (n_h, 1), 0), rq)
    cmp_rows = cmp_ref.shape[0] // KV_SLABS
    win_rows = win_ref.shape[0] // KV_SLABS
    n_io = lax.broadcasted_iota(jnp.int32, (1, cmp_rows), 1)
    valid_c = (n_io < n_cmp) & ((n_io * CMP_STRIDE + (CMP_LEN - 1)) <= q_pos)
    kp = win_pos0 + lax.broadcasted_iota(jnp.int32, (1, win_rows), 1)
    valid_w = (kp <= q_pos) & (kp > q_pos - WINDOW)
    o_cmp = jnp.zeros((n_h, HEAD_DIM), F32)
    o_win = jnp.zeros((n_h, HEAD_DIM), F32)
    imp = jnp.zeros((n_h, wmap_ref.shape[1]), F32)
    for g in range(N_KV_HEADS):
        in_g = row_g == g
        kc = cmp_ref[pl.ds(2 * g, cmp_rows, stride=KV_SLABS), :].astype(BF)
        vc = cmp_ref[pl.ds(2 * g + 1, cmp_rows, stride=KV_SLABS), :].astype(BF)
        p = _masked_softmax(_dot_t(q, kc) * scale, valid_c).astype(BF)
        o_cmp = jnp.where(in_g, _dot(p, vc), o_cmp)
        imp_rows = _dot(p, wmap_ref[...])
        imp_g = jnp.sum(jnp.where(in_g, imp_rows, 0.0), axis=0, keepdims=True)
        imp = jnp.where(in_g, imp_g, imp)
        kw = win_ref[pl.ds(2 * g, win_rows, stride=KV_SLABS), :].astype(BF)
        vw = win_ref[pl.ds(2 * g + 1, win_rows, stride=KV_SLABS), :].astype(BF)
        pw = _masked_softmax(_dot_t(q, kw) * scale, valid_w).astype(BF)
        o_win = jnp.where(in_g, _dot(pw, vw), o_win)
    ocmp_ref[...] = o_cmp
    owin_ref[...] = o_win

    j_io = lax.broadcasted_iota(jnp.int32, (1, imp.shape[1]), 1)
    q_blk = q_pos // SEL_BLOCK
    causal = j_io <= q_blk
    forced = (j_io == 0) | (causal & (j_io > q_blk - N_LOCAL_SEL))
    imp = jnp.where(forced, imp + FORCE_BONUS, imp)
    imp = jnp.where(causal, imp, -jnp.inf)
    rank = _rank_desc(imp, n_slc)
    jf = j_io.astype(F32)
    k_io = lax.broadcasted_iota(jnp.int32, (1, idx_ref.shape[1]), 1)
    idx = jnp.zeros(idx_ref.shape, F32)
    val = jnp.zeros(idx_ref.shape, F32)
    for k in range(min(N_SEL, n_slc)):
        hit = rank == k
        ik = jnp.sum(jnp.where(hit, jf, 0.0), axis=1, keepdims=True)
        vk = jnp.sum(jnp.where(hit, jnp.where(imp > -jnp.inf, 1.0, 0.0), 0.0), axis=1, keepdims=True)
        idx = jnp.where(k_io == k, ik, idx)
        val = jnp.where(k_io == k, vk, val)
    idx_ref[...] = idx.astype(jnp.int32)
    val_ref[...] = val.astype(jnp.int32)


def _attn_sample_a(q, cmp_blocks, win_rows, *, scale, q_pos, n_cmp, n_slc, win_pos0):
    nb, n_h, d = q.shape
    n_cols = -(-n_slc // LANES) * LANES
    wmap = jnp.asarray(_overlap_map(cmp_blocks.shape[1] // KV_SLABS, n_slc, n_cols), BF)
    o_sds = jax.ShapeDtypeStruct((nb, n_h, d), F32)
    i_sds = jax.ShapeDtypeStruct((nb, n_h, LANES), jnp.int32)
    blk = lambda a: pl.BlockSpec((None,) + a.shape[1:], lambda b: (b, 0, 0))
    ospec = pl.BlockSpec((None, n_h, d), lambda b: (b, 0, 0))
    ispec = pl.BlockSpec((None, n_h, LANES), lambda b: (b, 0, 0))
    return pl.pallas_call(
        functools.partial(_attn_sample_a_kernel, scale=scale, q_pos=q_pos, n_cmp=n_cmp, n_slc=n_slc,
                          win_pos0=win_pos0),
        out_shape=(o_sds, o_sds, i_sds, i_sds),
        grid=(nb,),
        in_specs=[blk(q), blk(cmp_blocks), blk(win_rows), pl.BlockSpec(wmap.shape, lambda b: (0, 0))],
        out_specs=(ospec, ospec, ispec, ispec),
        compiler_params=_cparams(1, 32),
        name="attn_sample_a",
    )(q, cmp_blocks, win_rows, wmap)


def _attn_sample_b_kernel(idx_ref, val_ref, pt_ref, q_ref, new_ref, gt_ref, ocmp_ref, owin_ref, *rest,
                          scale, q_pos, past_len):
    blk_refs = rest[:N_SEL]
    o_ref = rest[N_SEL]
    b = pl.program_id(0)
    g = pl.program_id(1)
    q = q_ref[...]
    n_h = q.shape[0]
    rq = n_h // N_KV_HEADS
    base = (b * N_KV_HEADS + g) * N_SEL
    n_keys = N_SEL * SEL_BLOCK
    lane = lax.broadcasted_iota(jnp.int32, (1, n_keys), 1)
    k_new = new_ref[pl.ds(2 * g, 1), :]
    v_new = new_ref[pl.ds(2 * g + 1, 1), :]
    ks, vs = [], []
    kpos = _mod_pow2(lane, SEL_BLOCK)
    kval = jnp.zeros((1, n_keys), jnp.int32)
    lane_blk = _div_pow2(lane, SEL_BLOCK)
    for k in range(N_SEL):
        blk = idx_ref[base + k]
        is_new = blk * SEL_BLOCK >= past_len
        ks.append(jnp.where(is_new, k_new, blk_refs[k][pl.ds(2 * g, SEL_BLOCK, stride=KV_SLABS), :]).astype(BF))
        vs.append(jnp.where(is_new, v_new, blk_refs[k][pl.ds(2 * g + 1, SEL_BLOCK, stride=KV_SLABS), :]).astype(BF))
        in_k = lane_blk == k
        kpos = kpos + jnp.where(in_k, blk * SEL_BLOCK, 0)
        kval = jnp.where(in_k, val_ref[base + k], kval)
    valid = (kval > 0) & (kpos <= q_pos)
    p = _masked_softmax(_dot_t(q, jnp.concatenate(ks, axis=0)) * scale, valid).astype(BF)
    o_sel = _dot(p, jnp.concatenate(vs, axis=0))
    gt = gt_ref[...]
    in_g = _div_pow2(lax.broadcasted_iota(jnp.int32, (n_h, 1), 0), rq) == g

    @pl.when(g == 0)
    def _():
        o_ref[...] = gt[:, 0:1] * ocmp_ref[...] + gt[:, 2:3] * owin_ref[...]

    o_ref[...] += jnp.where(in_g, gt[:, 1:2] * o_sel, 0.0)


def _attn_sample_b(sel_idx, sel_val, page_table, q, kv_sel_new, gates, o_cmp, o_win, pool_sel, *,
                   scale, q_pos, past_len):
    nb, n_h, d = q.shape
    halves = PAGE_SIZE // SEL_BLOCK
    blk_rows = SEL_BLOCK * KV_SLABS
    pool = pool_sel.reshape(pool_sel.shape[0] // blk_rows, blk_rows, d)

    def blk_spec(k):
        def index_map(b, g, idx, val, pt):
            row0 = jnp.minimum(idx[(b * N_KV_HEADS + g) * N_SEL + k] * SEL_BLOCK, past_len - 1)
            return (pt[b, row0 // PAGE_SIZE] * halves + (row0 % PAGE_SIZE) // SEL_BLOCK, 0, 0)
        return pl.BlockSpec((None, blk_rows, d), index_map)

    per_b = lambda a: pl.BlockSpec((None,) + a.shape[1:], lambda b, g, idx, val, pt: (b, 0, 0))
    new_rows = kv_sel_new.reshape(nb, KV_SLABS, d)
    grid_spec = pltpu.PrefetchScalarGridSpec(
        num_scalar_prefetch=3,
        grid=(nb, N_KV_HEADS),
        in_specs=[per_b(q), per_b(new_rows), per_b(gates), per_b(o_cmp), per_b(o_win)]
        + [blk_spec(k) for k in range(N_SEL)],
        out_specs=pl.BlockSpec((None, n_h, d), lambda b, g, idx, val, pt: (b, 0, 0)),
    )
    return pl.pallas_call(
        functools.partial(_attn_sample_b_kernel, scale=scale, q_pos=q_pos, past_len=past_len),
        out_shape=jax.ShapeDtypeStruct((nb, n_h, d), F32),
        grid_spec=grid_spec,
        compiler_params=_cparams(2, 32),
        name="attn_sample_b",
    )(sel_idx, sel_val, page_table, q, new_rows, gates, o_cmp, o_win, *([pool] * N_SEL))


def kernel(x_prompt, x_sample, cache_cmp_kv, cache_sel_kv, cache_win_kv, page_table, c_prompt, c_sample, w_ada, b_ada, ln_g, ln_b, a_w_in, a_v_g, a_v_b, a_w_s, a_b_s, a_w_out, w_kv, cmp_pe, cmp_w1, cmp_b1, cmp_w2, cmp_b2, b_w_qg, b_w_o, ffn_w_in, ffn_w_out):
    nb, t, d = x_prompt.shape
    ns, tq, _ = x_sample.shape
    assert tq == 1, "the sample path handles one new token per sequence"
    depth = w_ada.shape[0]
    n_a = a_w_in.shape[0]
    alpha = float((2 * depth) ** 0.25)
    scale = float(HEAD_DIM ** -0.5)
    n_pool = cache_cmp_kv.shape[0]
    past_len = page_table.shape[1] * PAGE_SIZE
    kvw = N_KV_HEADS * 2 * HEAD_DIM
    q_width = b_w_o.shape[1]
    n_heads = q_width // HEAD_DIM
    rq = n_heads // N_KV_HEADS

    mod = _ada_modulation(jnp.concatenate([c_prompt, c_sample], axis=0), w_ada, b_ada)

    def mods(layer, sub):
        m = mod[layer * 2 + sub]
        out = []
        for part in range(3):
            sl = m[:, part * d:(part + 1) * d]
            out.append((sl[:nb].reshape(nb, 1, d), sl[nb:].reshape(1, ns, d)))
        return out

    wb = lambda w: w.astype(BF)
    hidden = cmp_w1.shape[-1]
    halves = CMP_LEN // CMP_STRIDE
    assert halves == 2, "a compression block spans two stride-chunks"
    w1all = wb(cmp_w1.reshape(2, halves, CMP_STRIDE, HEAD_DIM, hidden)
               .transpose(2, 3, 0, 1, 4).reshape(CMP_STRIDE * HEAD_DIM, 2 * halves * hidden))
    pe16 = jnp.pad(cmp_pe.reshape(2 * halves, CMP_STRIDE * HEAD_DIM), ((0, 16 - 2 * halves), (0, 0)))
    w2all = wb(cmp_w2.transpose(1, 0, 2).reshape(hidden, 2 * HEAD_DIM))

    a_w_out_b, b_w_o_b, ffn_w_in_b, ffn_w_out_b = wb(a_w_out), wb(b_w_o), wb(ffn_w_in), wb(ffn_w_out)

    tm_p = 512
    tpg_p = t // tm_p
    prompt = dict(tm=tm_p, tiles_per_group=tpg_p)
    sample = dict(tm=ns, tiles_per_group=1)

    xp = x_prompt.reshape(nb * t, d)
    xs = x_sample.reshape(ns, d)
    hp = hs = None
    chunk_v = []
    outs = {}
    for layer in range(depth):
        (sh_p, sh_s), (sc_p, sc_s), (gt_p, gt_s) = mods(layer, 0)
        (fsh_p, fsh_s), (fsc_p, fsc_s), (fgt_p, fgt_s) = mods(layer, 1)
        if layer < n_a:
            w_in = wb(a_w_in[layer])
            ap = _gmlp_in(xp, sc_p, sh_p, w_in, a_v_g[layer], a_v_b[layer], a_w_s[layer], a_b_s[layer],
                          chunked=True, **prompt)
            as_, v_rows = _gmlp_in(xs, sc_s, sh_s, w_in, a_v_g[layer], a_v_b[layer], a_w_s[layer], a_b_s[layer],
                                   chunked=False, **sample)
            chunk_v.append(v_rows.reshape(ns, tq, -1))
            xp, hp = _proj_res_ln(ap, a_w_out_b, xp, gt_p, ln_g[layer, 0], ln_b[layer, 0], (fsc_p, fsh_p),
                                  alpha=alpha, layer=layer, **prompt)
            xs, hs = _proj_res_ln(as_, a_w_out_b, xs, gt_s, ln_g[layer, 0], ln_b[layer, 0], (fsc_s, fsh_s),
                                  alpha=alpha, layer=layer, **sample)
        else:
            if layer == n_a:
                w_kvb = wb(w_kv)
                cmp_p, sel_p, win_p, sel_pb, win_pb = _kv_proj(xp, w_kvb, tm=tm_p)
                cmp_s, sel_s, win_s, _, _ = _kv_proj(xs, w_kvb, tm=ns)
                pages_per_seq_p = t // PAGE_SIZE
                ident = jnp.arange(nb * pages_per_seq_p, dtype=jnp.int32).reshape(nb, pages_per_seq_p)
                cmp_blocks_p = _compress(cmp_p, ident, w1all, pe16, cmp_b1, w2all, cmp_b2)
                assert (past_len + tq) // CMP_STRIDE == past_len // CMP_STRIDE
                cmp_blocks_s = _compress(cache_cmp_kv.reshape(n_pool * PAGE_FLAT_ROWS, HEAD_DIM), page_table,
                                         w1all, pe16, cmp_b1, w2all, cmp_b2)
                n_win = cache_win_kv.shape[1]
                win_keys_s = jnp.concatenate([cache_win_kv.reshape(ns, n_win * KV_SLABS, HEAD_DIM),
                                              win_s.reshape(ns, tq * KV_SLABS, HEAD_DIM)],
                                             axis=1)[:, -n_win * KV_SLABS:]
                outs["kv_p"] = (cmp_p, sel_p, win_p)
                outs["kv_pb"] = (sel_pb.reshape(nb, t, kvw), win_pb.reshape(nb, t, kvw))
                outs["kv_s"] = (cmp_s, sel_s, win_keys_s)
            j = layer - n_a
            w_qg = b_w_qg[j]
            w_q = wb(w_qg[:, :q_width])
            n_gates = w_qg.shape[1] - q_width
            w_gate = wb(jnp.pad(w_qg[:, q_width:], ((0, 0), (0, LANES - n_gates))))
            assert layer > 0, "an attention layer must follow another layer"
            q_p, g_p = _qg_proj(hp, w_q, w_gate, tm=tm_p, gates_per_group=rq * 3)
            q_s, g_s = _qg_proj(hs, w_q, w_gate, tm=ns, gates_per_group=rq * 3)
            o_p = _attn_prompt(q_p.reshape(nb, t, q_width), g_p.reshape(nb, t, N_KV_HEADS * LANES),
                               cmp_blocks_p, outs["kv_pb"][0], outs["kv_pb"][1], scale=scale)
            n_cmp_s = (past_len + tq) // CMP_STRIDE - CMP_LEN // CMP_STRIDE + 1
            n_slc_s = -(-(past_len + tq) // SEL_BLOCK)
            q_s3 = q_s.reshape(ns, n_heads, HEAD_DIM)
            o_cmp_s, o_win_s, idx_s, val_s = _attn_sample_a(
                q_s3, cmp_blocks_s, outs["kv_s"][2], scale=scale, q_pos=past_len, n_cmp=n_cmp_s, n_slc=n_slc_s,
                win_pos0=past_len + tq - outs["kv_s"][2].shape[1] // KV_SLABS)
            sel_idx = idx_s[:, ::rq, :N_SEL].reshape(-1)
            sel_val = val_s[:, ::rq, :N_SEL].reshape(-1)
            gates_s = g_s.reshape(ns, N_KV_HEADS, LANES)[:, :, :rq * 3].reshape(ns, n_heads, 3)
            o_s = _attn_sample_b(sel_idx, sel_val, page_table, q_s3, outs["kv_s"][1], gates_s, o_cmp_s, o_win_s,
                                 cache_sel_kv.reshape(n_pool * PAGE_FLAT_ROWS, HEAD_DIM), scale=scale,
                                 q_pos=past_len, past_len=past_len)
            xp, hp = _proj_res_ln(o_p.reshape(nb * t, q_width), b_w_o_b, xp, gt_p, ln_g[layer, 0], ln_b[layer, 0],
                                  (fsc_p, fsh_p), alpha=alpha, layer=j, **prompt)
            xs, hs = _proj_res_ln(o_s.reshape(ns, q_width), b_w_o_b, xs, gt_s, ln_g[layer, 0], ln_b[layer, 0],
                                  (fsc_s, fsh_s), alpha=alpha, layer=j, **sample)
        fp = _ffn_in(hp, ffn_w_in_b, layer, tm=2 * tm_p)
        fs = _ffn_in(hs, ffn_w_in_b, layer, tm=ns)
        nxt_p = nxt_s = None
        if n_a <= layer + 1 < depth:
            (nsh_p, nsh_s), (nsc_p, nsc_s), _ = mods(layer + 1, 0)
            nxt_p, nxt_s = (nsc_p, nsh_p), (nsc_s, nsh_s)
        tm_f = tm_p // 2
        xp, hp = _proj_res_ln(fp, ffn_w_out_b, xp, fgt_p, ln_g[layer, 1], ln_b[layer, 1], nxt_p, alpha=alpha,
                              layer=layer, tm=tm_f, tiles_per_group=t // tm_f)
        xs, hs = _proj_res_ln(fs, ffn_w_out_b, xs, fgt_s, ln_g[layer, 1], ln_b[layer, 1], nxt_s, alpha=alpha,
                              layer=layer, **sample)

    cmp_p, sel_p, win_p = outs["kv_p"]
    cmp_s, sel_s, win_keys_s = outs["kv_s"]
    kv_shape = (N_KV_HEADS, 2, HEAD_DIM)
    n_win_p = min(WINDOW, t)
    return (xp.reshape(nb, t, d), xs.reshape(ns, tq, d),
            cmp_p.reshape((nb, t) + kv_shape), sel_p.reshape((nb, t) + kv_shape),
            win_p.reshape((nb, t) + kv_shape)[:, -n_win_p:],
            cmp_s.reshape((ns, tq) + kv_shape), sel_s.reshape((ns, tq) + kv_shape),
            win_keys_s.reshape((ns, -1) + kv_shape),
            jnp.stack(chunk_v))
```

```python
import functools

import numpy as np
import jax
import jax.numpy as jnp
from jax import lax
from jax.experimental import pallas as pl
from jax.experimental.pallas import tpu as pltpu

BF = jnp.bfloat16
F32 = jnp.float32

LN_EPS = 1e-5
CHUNK = 128
HEAD_DIM = 128
N_KV_HEADS = 4
CMP_LEN = 32
CMP_STRIDE = 16
SEL_BLOCK = 64
N_SEL = 16
N_LOCAL_SEL = 2
FORCE_BONUS = 1e4
LOG2_E = 1.4426950408889634
WINDOW = 512
Q_BLOCK = 256
PAGE_SIZE = 128

LANES = 128
MIB = 1024 * 1024


def _cparams(n_axes, vmem_mib):
    return pltpu.CompilerParams(dimension_semantics=("arbitrary",) * n_axes,
                                vmem_limit_bytes=vmem_mib * MIB)


def _dot(a, b):
    return jnp.dot(a, b, preferred_element_type=F32)


def _dot_t(a, b):
    return lax.dot_general(a, b, (((1,), (1,)), ((), ())), preferred_element_type=F32)


def _layer_norm(x, g, b):
    mu = jnp.mean(x, axis=-1, keepdims=True)
    xc = x - mu
    var = jnp.mean(xc * xc, axis=-1, keepdims=True)
    return xc * lax.rsqrt(var + LN_EPS) * g + b


def _masked_softmax(s, valid):
    s = jnp.where(valid, s, -jnp.inf)
    m = jnp.max(s, axis=-1, keepdims=True)
    m = jnp.where(m == -jnp.inf, 0.0, m)
    e = jnp.exp(s - m)
    d = jnp.maximum(jnp.sum(e, axis=-1, keepdims=True), 1e-30)
    return e * (1.0 / d)


def _log2(n):
    assert n & (n - 1) == 0, "power of two expected"
    return n.bit_length() - 1


def _div_pow2(x, n):
    return lax.shift_right_logical(x, jnp.int32(_log2(n)))


def _mod_pow2(x, n):
    assert n & (n - 1) == 0, "power of two expected"
    return lax.bitwise_and(x, jnp.int32(n - 1))


def _rank_desc(imp, n):
    j_io = lax.broadcasted_iota(jnp.int32, (1, imp.shape[1]), 1)
    rank = jnp.zeros(imp.shape, F32)
    for jp in range(n):
        col = imp[:, jp:jp + 1]
        before = jnp.where(col == imp, jnp.where(jp < j_io, 1.0, 0.0), jnp.where(col > imp, 1.0, 0.0))
        rank = rank + before
    return rank


def _ada_kernel(c_ref, w_ref, b_ref, o_ref):
    a = jax.nn.silu(c_ref[...]).astype(BF)
    o_ref[...] = _dot(a, w_ref[...].astype(BF)) + b_ref[...]


def _ada_modulation(c_all, w_ada, b_ada):
    n_sub = w_ada.shape[0] * w_ada.shape[1]
    d, n3 = w_ada.shape[2], w_ada.shape[3]
    m = c_all.shape[0]
    tn = 512
    w = w_ada.reshape(n_sub, d, n3)
    b = b_ada.reshape(n_sub, 1, n3)
    return pl.pallas_call(
        _ada_kernel,
        out_shape=jax.ShapeDtypeStruct((n_sub, m, n3), F32),
        grid=(n_sub, n3 // tn),
        in_specs=[pl.BlockSpec((m, d), lambda s, j: (0, 0)),
                  pl.BlockSpec((None, d, tn), lambda s, j: (s, 0, j)),
                  pl.BlockSpec((None, 1, tn), lambda s, j: (s, 0, j))],
        out_specs=pl.BlockSpec((None, m, tn), lambda s, j: (s, 0, j)),
        compiler_params=_cparams(2, 32),
        name="ada_modulation",
    )(c_all, w, b)


def _mod_spec(mod, tiles_per_group):
    rb, d = mod.shape[1], mod.shape[2]
    return pl.BlockSpec((None, rb, d), lambda i: (i // tiles_per_group, 0, 0))


def _gmlp_in_kernel(x_ref, sc_ref, sh_ref, w_ref, vg_ref, vb_ref, ws_ref, bs_ref, *rest,
                    n_j, tn, chunked):
    if chunked:
        o_ref, z_scr = rest
    else:
        o_ref, v_ref, z_scr = rest
    h = (x_ref[...] * (1.0 + sc_ref[...]) + sh_ref[...]).astype(BF)
    for jj in range(n_j):
        z_scr[jj] = jax.nn.gelu(_dot(h, w_ref[:, jj * tn:(jj + 1) * tn]))

    tm = z_scr.shape[1]
    half = n_j // 2
    e_a = half * tn
    s1 = jnp.zeros((tm, 1), F32)
    for jj in range(half):
        s1 = s1 + jnp.sum(z_scr[half + jj], axis=-1, keepdims=True)
    mu = s1 / e_a
    s2 = jnp.zeros((tm, 1), F32)
    for jj in range(half):
        vc = z_scr[half + jj] - mu
        s2 = s2 + jnp.sum(vc * vc, axis=-1, keepdims=True)
    rstd = lax.rsqrt(s2 / e_a + LN_EPS)
    gpt = tn // CHUNK
    if chunked:
        row = lax.broadcasted_iota(jnp.int32, (CHUNK, CHUNK), 0)
        col = lax.broadcasted_iota(jnp.int32, (CHUNK, CHUNK), 1)
        for jj in range(half):
            for gi in range(gpt):
                g = jj * gpt + gi
                cs = slice(gi * CHUNK, (gi + 1) * CHUNK)
                wm = jnp.where(row >= col, ws_ref[g], 0.0).astype(BF)
                bcol = bs_ref[:, g:g + 1]
                vg = vg_ref[:, g * CHUNK:(g + 1) * CHUNK]
                vb = vb_ref[:, g * CHUNK:(g + 1) * CHUNK]
                for c in range(tm // CHUNK):
                    rs = slice(c * CHUNK, (c + 1) * CHUNK)
                    vn = (z_scr[half + jj, rs, cs] - mu[rs]) * rstd[rs] * vg + vb
                    mixed = _dot(wm, vn.astype(BF)) + bcol
                    o_ref[rs, g * CHUNK:(g + 1) * CHUNK] = (z_scr[jj, rs, cs] * mixed).astype(BF)
    else:
        for jj in range(half):
            cs = slice(jj * tn, (jj + 1) * tn)
            vn = (z_scr[half + jj] - mu) * rstd * vg_ref[:, cs] + vb_ref[:, cs]
            v_ref[:, cs] = vn
            mixed = ws_ref[:, cs] * vn + bs_ref[:, cs]
            o_ref[:, cs] = (z_scr[jj] * mixed).astype(BF)


def _gmlp_in(x, sc, sh, w_in, v_g, v_b, w_s, b_s, *, tm, tiles_per_group, chunked):
    r, d = x.shape
    n2 = w_in.shape[1]
    e_a = n2 // 2
    tn = 512
    n_j = n2 // tn
    n_g = w_s.shape[0]
    full = lambda a: pl.BlockSpec(a.shape, lambda i: (0,) * a.ndim)
    row_tile = pl.BlockSpec((tm, e_a), lambda i: (i, 0))
    if chunked:
        ws_arr = w_s
        bs_arr = b_s.T
        out_shape = jax.ShapeDtypeStruct((r, e_a), BF)
        out_specs = row_tile
    else:
        ws_arr = jnp.repeat(w_s[:, 0, 0], e_a // n_g).reshape(1, e_a)
        bs_arr = jnp.repeat(b_s[:, 0], e_a // n_g).reshape(1, e_a)
        out_shape = (jax.ShapeDtypeStruct((r, e_a), BF), jax.ShapeDtypeStruct((r, e_a), F32))
        out_specs = (row_tile, row_tile)
    vg2, vb2 = v_g.reshape(1, e_a), v_b.reshape(1, e_a)
    return pl.pallas_call(
        functools.partial(_gmlp_in_kernel, n_j=n_j, tn=tn, chunked=chunked),
        out_shape=out_shape,
        grid=(r // tm,),
        in_specs=[pl.BlockSpec((tm, d), lambda i: (i, 0)),
                  _mod_spec(sc, tiles_per_group), _mod_spec(sh, tiles_per_group),
                  pl.BlockSpec(w_in.shape, lambda i: (0, 0), pipeline_mode=pl.Buffered(1)),
                  full(vg2), full(vb2), full(ws_arr), full(bs_arr)],
        out_specs=out_specs,
        scratch_shapes=[pltpu.VMEM((n_j, tm, tn), F32)],
        compiler_params=_cparams(1, 48),
        name="gmlp_in_chunked" if chunked else "gmlp_in_single",
    )(x, sc, sh, w_in, vg2, vb2, ws_arr, bs_arr)


LN_ROWS = 16


def _proj_res_ln_kernel(a_ref, w_ref, x_ref, gate_ref, lg_ref, lb_ref, *rest, alpha, emit_next):
    if emit_next:
        sc_ref, sh_ref, y_ref, h_ref, acc = rest
    else:
        y_ref, acc = rest
    acc[...] = _dot(a_ref[...].astype(BF), w_ref[...])
    tm = acc.shape[0]
    per_row_mod = gate_ref.shape[0] > 1
    step = min(LN_ROWS, tm)
    assert tm % step == 0
    for c in range(tm // step):
        rs = slice(c * step, (c + 1) * step)
        ms = rs if per_row_mod else slice(None)
        t = alpha * x_ref[rs, :] + (1.0 + gate_ref[ms, :]) * acc[rs, :]
        y = _layer_norm(t, lg_ref[...], lb_ref[...])
        y_ref[rs, :] = y
        if emit_next:
            h_ref[rs, :] = (y * (1.0 + sc_ref[ms, :]) + sh_ref[ms, :]).astype(BF)


def _proj_res_ln(a, w, x, gate, ln_g, ln_b, nxt, *, alpha, tm, tiles_per_group, layer=0):
    r, kdim = a.shape
    d = w.shape[2]
    emit_next = nxt is not None
    mod_spec = lambda m: _mod_spec(m, tiles_per_group)
    in_specs = [pl.BlockSpec((tm, kdim), lambda i: (i, 0)),
                pl.BlockSpec((None, kdim, d), lambda i: (layer, 0, 0), pipeline_mode=pl.Buffered(1)),
                pl.BlockSpec((tm, d), lambda i: (i, 0)),
                mod_spec(gate),
                pl.BlockSpec((1, d), lambda i: (0, 0)),
                pl.BlockSpec((1, d), lambda i: (0, 0))]
    args = [a, w, x, gate, ln_g.reshape(1, d), ln_b.reshape(1, d)]
    out_shape = [jax.ShapeDtypeStruct((r, d), F32)]
    out_specs = [pl.BlockSpec((tm, d), lambda i: (i, 0))]
    if emit_next:
        in_specs += [mod_spec(nxt[0]), mod_spec(nxt[1])]
        args += [nxt[0], nxt[1]]
        out_shape.append(jax.ShapeDtypeStruct((r, d), BF))
        out_specs.append(pl.BlockSpec((tm, d), lambda i: (i, 0)))
    res = pl.pallas_call(
        functools.partial(_proj_res_ln_kernel, alpha=alpha, emit_next=emit_next),
        out_shape=tuple(out_shape),
        grid=(r // tm,),
        in_specs=in_specs,
        out_specs=tuple(out_specs),
        scratch_shapes=[pltpu.VMEM((tm, d), F32)],
        compiler_params=_cparams(1, 56),
        name="proj_res_ln",
    )(*args)
    return res if emit_next else (res[0], None)


def _ffn_in_kernel(h_ref, wg_ref, wu_ref, o_ref):
    h = h_ref[...]
    o_ref[...] = (jax.nn.silu(_dot(h, wg_ref[...])) * _dot(h, wu_ref[...])).astype(BF)


def _ffn_in(h, w_in, layer, *, tm):
    r, d = h.shape
    d_ff = w_in.shape[2] // 2
    tn = 512
    n_j = d_ff // tn
    return pl.pallas_call(
        _ffn_in_kernel,
        out_shape=jax.ShapeDtypeStruct((r, d_ff), BF),
        grid=(r // tm, n_j),
        in_specs=[pl.BlockSpec((tm, d), lambda i, j: (i, 0)),
                  pl.BlockSpec((None, d, tn), lambda i, j: (layer, 0, j)),
                  pl.BlockSpec((None, d, tn), lambda i, j: (layer, 0, j + n_j))],
        out_specs=pl.BlockSpec((tm, tn), lambda i, j: (i, j)),
        compiler_params=_cparams(2, 40),
        name="ffn_in",
    )(h, w_in, w_in)


KV_SLABS = N_KV_HEADS * 2


def _kv_proj_kernel(y_ref, w_ref, f0_ref, f1_ref, f2_ref, b1_ref, b2_ref):
    xb = y_ref[...].astype(BF)
    tm = xb.shape[0]
    n = w_ref.shape[1] // 3
    for idx, (f_ref, b_ref) in enumerate(((f0_ref, None), (f1_ref, b1_ref), (f2_ref, b2_ref))):
        r = _dot(xb, w_ref[:, idx * n:(idx + 1) * n])
        for hk in range(KV_SLABS):
            f_ref[pl.ds(hk, tm, stride=KV_SLABS), :] = r[:, hk * HEAD_DIM:(hk + 1) * HEAD_DIM]
        if b_ref is not None:
            b_ref[...] = r.astype(BF)


def _kv_proj(y, w_kv, *, tm):
    r, d = y.shape
    n = w_kv.shape[1] // 3
    flat = jax.ShapeDtypeStruct((r * KV_SLABS, HEAD_DIM), F32)
    fspec = pl.BlockSpec((tm * KV_SLABS, HEAD_DIM), lambda i: (i, 0))
    wide = jax.ShapeDtypeStruct((r, n), BF)
    wspec = pl.BlockSpec((tm, n), lambda i: (i, 0))
    return pl.pallas_call(
        _kv_proj_kernel,
        out_shape=(flat, flat, flat, wide, wide),
        grid=(r // tm,),
        in_specs=[pl.BlockSpec((tm, d), lambda i: (i, 0)),
                  pl.BlockSpec(w_kv.shape, lambda i: (0, 0), pipeline_mode=pl.Buffered(1))],
        out_specs=(fspec, fspec, fspec, wspec, wspec),
        compiler_params=_cparams(1, 48),
        name="kv_proj",
    )(y, w_kv)


def _qg_proj_kernel(h_ref, wq_ref, wg_ref, q_ref, g_ref, *, gates_per_group):
    h = h_ref[...]
    q_ref[...] = _dot(h, wq_ref[...]).astype(BF)
    gates = jax.nn.sigmoid(_dot(h, wg_ref[...]))
    for grp in range(g_ref.shape[1] // LANES):
        shift = (LANES - grp * gates_per_group) % LANES
        g_ref[:, grp * LANES:(grp + 1) * LANES] = gates if shift == 0 else pltpu.roll(gates, shift, 1)


def _qg_proj(h, w_q, w_g, *, tm, gates_per_group):
    r, d = h.shape
    qw = w_q.shape[1]
    gw = N_KV_HEADS * LANES
    resident = lambda w: pl.BlockSpec(w.shape, lambda i: (0, 0), pipeline_mode=pl.Buffered(1))
    return pl.pallas_call(
        functools.partial(_qg_proj_kernel, gates_per_group=gates_per_group),
        out_shape=(jax.ShapeDtypeStruct((r, qw), BF), jax.ShapeDtypeStruct((r, gw), F32)),
        grid=(r // tm,),
        in_specs=[pl.BlockSpec((tm, d), lambda i: (i, 0)), resident(w_q), resident(w_g)],
        out_specs=(pl.BlockSpec((tm, qw), lambda i: (i, 0)),
                   pl.BlockSpec((tm, gw), lambda i: (i, 0))),
        compiler_params=_cparams(1, 40),
        name="qg_proj",
    )(h, w_q, w_g)


PAGES_PER_STEP = 16
CHUNKS_PER_PAGE = PAGE_SIZE // CMP_STRIDE


PAGE_FLAT_ROWS = PAGE_SIZE * KV_SLABS
CHUNK_FLAT_ROWS = CMP_STRIDE * KV_SLABS


def _compress_kernel(pt_ref, *refs, n_steps):
    page_refs = refs[:PAGES_PER_STEP]
    w1_ref, pe_ref, b1_ref, w2_ref, b2_ref, o_ref, p0_scr, p1_scr = refs[PAGES_PER_STEP:]
    t = pl.program_id(1)
    rows_step = PAGES_PER_STEP * CHUNKS_PER_PAGE * N_KV_HEADS
    n_rows = n_steps * rows_step
    hidden = w2_ref.shape[0]

    @pl.when(t == 0)
    def _():
        p1_scr[:, pl.ds(n_rows, 8), :] = jnp.zeros((2, 8, hidden), F32)

    row0 = pl.multiple_of(t * rows_step, rows_step)
    for k in range(2):
        per_s = []
        for s in range(CMP_STRIDE):
            rows = [page_refs[p][pl.ds(c * CHUNK_FLAT_ROWS + s * KV_SLABS + k, N_KV_HEADS, stride=2), :]
                    for p in range(PAGES_PER_STEP) for c in range(CHUNKS_PER_PAGE)]
            per_s.append(jnp.concatenate(rows, axis=0))
        x = jnp.concatenate(per_s, axis=1).astype(BF)
        pr = _dot(x, w1_ref[:, k * 2 * hidden:(k + 1) * 2 * hidden])
        p0_scr[k, pl.ds(row0, rows_step), :] = pr[:, :hidden]
        p1_scr[k, pl.ds(row0, rows_step), :] = pr[:, hidden:]

    @pl.when(t == n_steps - 1)
    def _():
        cst = _dot(pe_ref[...].astype(BF), w1_ref[...])
        for k in range(2):
            c0 = 2 * k * hidden
            const = (cst[2 * k:2 * k + 1, c0:c0 + hidden] + cst[2 * k + 1:2 * k + 2, c0 + hidden:c0 + 2 * hidden]
                     + b1_ref[k:k + 1, :])
            pre = p0_scr[k, 0:n_rows, :] + p1_scr[k, N_KV_HEADS:n_rows + N_KV_HEADS, :] + const
            res = _dot(jax.nn.gelu(pre).astype(BF), w2_ref[:, k * HEAD_DIM:(k + 1) * HEAD_DIM]) + b2_ref[k:k + 1, :]
            o_ref[pl.ds(k, n_rows, stride=2), :] = res


def _compress(pages_flat, page_table, w1all, pe16, b1, w2all, b2):
    nb, n_pages = page_table.shape
    n_steps = n_pages // PAGES_PER_STEP
    n_rows = n_pages * CHUNKS_PER_PAGE * KV_SLABS
    hidden = w2all.shape[0]

    def page_spec(p):
        return pl.BlockSpec((PAGE_FLAT_ROWS, HEAD_DIM), lambda b, t, pt: (pt[b, t * PAGES_PER_STEP + p], 0))

    full = lambda a: pl.BlockSpec(a.shape, lambda b, t, pt: (0,) * a.ndim)
    grid_spec = pltpu.PrefetchScalarGridSpec(
        num_scalar_prefetch=1,
        grid=(nb, n_steps),
        in_specs=[page_spec(p) for p in range(PAGES_PER_STEP)] + [
            full(w1all), full(pe16), full(b1), full(w2all), full(b2)],
        out_specs=pl.BlockSpec((None, n_rows, HEAD_DIM), lambda b, t, pt: (b, 0, 0)),
        scratch_shapes=[pltpu.VMEM((2, n_rows // 2, hidden), F32), pltpu.VMEM((2, n_rows // 2 + 8, hidden), F32)],
    )
    return pl.pallas_call(
        functools.partial(_compress_kernel, n_steps=n_steps),
        out_shape=jax.ShapeDtypeStruct((nb, n_rows, HEAD_DIM), F32),
        grid_spec=grid_spec,
        compiler_params=_cparams(2, 48),
        name="compress_blocks",
    )(page_table, *([pages_flat] * PAGES_PER_STEP), w1all, pe16, b1, w2all, b2)


def _overlap_map(n_cmp_rows, n_slc, n_cols):
    cmp_lo = np.arange(n_cmp_rows)[:, None] * CMP_STRIDE
    slc_lo = np.arange(n_cols)[None, :] * SEL_BLOCK
    ov = np.minimum(cmp_lo + CMP_LEN, slc_lo + SEL_BLOCK) - np.maximum(cmp_lo, slc_lo)
    w = np.clip(ov, 0, None).astype(np.float32) / CMP_LEN
    w[:, n_slc:] = 0.0
    return w


SEL_CHUNK = 512


def _attn_prompt_kernel(q_ref, gt_ref, cmp_ref, sel_ref, win_ref, wmap_t_ref, exp_ref, o_ref,
                        s_sel, m_run, l_run, e_scr, s_win, m_win, l_win, e_win, *, scale, n_slc, n_qb, n_widths):
    g = pl.program_id(1)
    qb = pl.program_id(2)
    q4 = q_ref[...]
    nq = q4.shape[0]
    rq = q4.shape[1] // HEAD_DIM
    qs = jnp.concatenate([q4[:, r * HEAD_DIM:(r + 1) * HEAD_DIM] for r in range(rq)], axis=0)
    qpos = qb * nq + lax.broadcasted_iota(jnp.int32, (nq, 1), 0)
    rows = [slice(r * nq, (r + 1) * nq) for r in range(rq)]

    head_groups = [rows[:rq // 2], rows[rq // 2:]] if rq % 2 == 0 else [rows]

    def attend(n_chunks, chunk, s_scr, m_run, l_run, e_scr, keys, mask, values):
        width = n_chunks * chunk
        m_run[...] = jnp.full(m_run.shape, -jnp.inf, F32)
        l_run[...] = jnp.zeros(l_run.shape, F32)
        for grp in head_groups:
            g_rows = slice(grp[0].start, grp[-1].stop)
            for c in range(n_chunks):
                s = _dot_t(qs[g_rows], keys(c))
                ok = mask(c)
                for i, rs in enumerate(grp):
                    sm = jnp.where(ok, s[i * nq:(i + 1) * nq], -jnp.inf) * (scale * LOG2_E)
                    s_scr[c, rs, :] = sm
                    mx = m_run[rs, :]
                    for j in range(chunk // LANES):
                        mx = jnp.maximum(mx, sm[:, j * LANES:(j + 1) * LANES])
                    m_run[rs, :] = mx
        outs = []
        for grp in head_groups:
            g_rows = slice(grp[0].start, grp[-1].stop)
            m = jnp.max(m_run[g_rows, :], axis=-1, keepdims=True)
            m = jnp.where(m == -jnp.inf, 0.0, m)
            for c in range(n_chunks):
                for i, rs in enumerate(grp):
                    e = jnp.exp2(s_scr[c, rs, :] - m[i * nq:(i + 1) * nq])
                    ls = l_run[rs, :]
                    for j in range(chunk // LANES):
                        ls = ls + e[:, j * LANES:(j + 1) * LANES]
                    l_run[rs, :] = ls
                    e_scr[rs, c * chunk:(c + 1) * chunk] = e.astype(BF)
            l = jnp.sum(l_run[g_rows, :], axis=-1, keepdims=True)
            outs.append(_dot(e_scr[g_rows, :width], values) * (1.0 / jnp.maximum(l, 1e-30)))
        return jnp.concatenate(outs, axis=0)

    def step(n_sel_chunks, first_q):
        span = WINDOW + nq
        k0 = pl.multiple_of(jnp.maximum(qb * nq - WINDOW, 0), nq)
        kp = k0 + lax.broadcasted_iota(jnp.int32, (1, span), 1)
        valid_w = lax.bitcast_convert_type(qpos - kp, jnp.uint32) < jnp.uint32(WINDOW)
        o_win = attend(1, span, s_win, m_win, l_win, e_win, lambda c: win_ref[pl.ds(k0, span), :HEAD_DIM],
                       lambda c: valid_w, win_ref[pl.ds(k0, span), HEAD_DIM:])

        n_cmp_rows = cmp_ref.shape[0] // KV_SLABS
        kc = cmp_ref[pl.ds(g * 2, n_cmp_rows, stride=KV_SLABS), :].astype(BF)
        vc = cmp_ref[pl.ds(g * 2 + 1, n_cmp_rows, stride=KV_SLABS), :].astype(BF)
        n_io = lax.broadcasted_iota(jnp.int32, (1, n_cmp_rows), 1)
        valid_c = (n_io * CMP_STRIDE + (CMP_LEN - 1)) <= qpos
        s_c = _dot_t(qs, kc) * scale
        ps = [_masked_softmax(s_c[rs], valid_c).astype(BF) for rs in rows]
        o_cmp = _dot(jnp.concatenate(ps, axis=0), vc)

        imp = _dot_t(wmap_t_ref[...], jnp.concatenate(ps, axis=1))
        n_rows = imp.shape[0]
        j_io = lax.broadcasted_iota(jnp.int32, (n_rows, 1), 0)
        q_blk = _div_pow2(qb * nq + lax.broadcasted_iota(jnp.int32, (1, nq), 1), SEL_BLOCK)
        causal = j_io <= q_blk
        forced = (j_io == 0) | (causal & (j_io > q_blk - N_LOCAL_SEL))
        imp = jnp.where(forced, imp + FORCE_BONUS, imp)
        imp = jnp.where(causal, imp, -jnp.inf)
        rank = jnp.zeros(imp.shape, F32)
        for jp in range(n_slc):
            other = imp[jp:jp + 1, :]
            rank = rank + jnp.where(other == imp, jnp.where(jp < j_io, 1.0, 0.0),
                                    jnp.where(other > imp, 1.0, 0.0))
        chosen_t = jnp.where(rank < min(N_SEL, n_slc), jnp.where(imp > -jnp.inf, 1.0, 0.0), 0.0)
        chosen_t = jnp.concatenate([chosen_t, jnp.zeros((LANES - n_rows, nq), F32)], axis=0)
        chosen = chosen_t.T.astype(BF)

        def sel_mask(c):
            in_sel = _dot(chosen, exp_ref[c])
            if (c + 1) * SEL_CHUNK <= first_q:
                return in_sel > 0.5
            t_io = c * SEL_CHUNK + lax.broadcasted_iota(jnp.int32, (1, SEL_CHUNK), 1)
            return jnp.where(t_io <= qpos, in_sel, 0.0) > 0.5

        o_sel = attend(n_sel_chunks, SEL_CHUNK, s_sel, m_run, l_run, e_scr,
                       lambda c: sel_ref[c * SEL_CHUNK:(c + 1) * SEL_CHUNK, :HEAD_DIM], sel_mask,
                       sel_ref[:n_sel_chunks * SEL_CHUNK, HEAD_DIM:])

        gt = gt_ref[...]
        outs = []
        for r, rs in enumerate(rows):
            outs.append(gt[:, 3 * r:3 * r + 1] * o_cmp[rs] + gt[:, 3 * r + 1:3 * r + 2] * o_sel[rs]
                        + gt[:, 3 * r + 2:3 * r + 3] * o_win[rs])
        o_ref[...] = jnp.concatenate(outs, axis=1).astype(BF)

    qb_per_width = n_qb // n_widths
    for wi in range(n_widths):
        n_sel_chunks = -(-(wi + 1) * qb_per_width * nq // SEL_CHUNK)
        pl.when((qb >= wi * qb_per_width) & (qb < (wi + 1) * qb_per_width))(
            functools.partial(step, n_sel_chunks, wi * qb_per_width * nq))


def _attn_prompt(q, gates, cmp_flat, kv_sel, kv_win, *, scale):
    nb, t, qw = q.shape
    n_g = N_KV_HEADS
    gw = qw // n_g
    kvw = 2 * HEAD_DIM
    cmp_rows = cmp_flat.shape[1]
    n_slc = t // SEL_BLOCK
    n_slc_rows = -(-n_slc // 8) * 8
    rq = gw // HEAD_DIM
    n_qb = t // Q_BLOCK
    n_chunks = t // SEL_CHUNK
    wmap_t = jnp.asarray(np.tile(_overlap_map(cmp_rows // KV_SLABS, n_slc, n_slc_rows), (rq, 1)).T, BF)
    key_blk = (np.arange(t) // SEL_BLOCK).reshape(n_chunks, 1, SEL_CHUNK)
    expand = jnp.asarray((key_blk == np.arange(LANES)[None, :, None]).astype(np.float32), BF)
    rows = rq * Q_BLOCK
    n_widths = 4 if n_qb % 4 == 0 else 1
    return pl.pallas_call(
        functools.partial(_attn_prompt_kernel, scale=scale, n_slc=n_slc, n_qb=n_qb, n_widths=n_widths),
        out_shape=jax.ShapeDtypeStruct((nb, t, qw), BF),
        grid=(nb, n_g, n_qb),
        in_specs=[pl.BlockSpec((None, Q_BLOCK, gw), lambda b, g, i: (b, i, g)),
                  pl.BlockSpec((None, Q_BLOCK, LANES), lambda b, g, i: (b, i, g)),
                  pl.BlockSpec((None, cmp_rows, HEAD_DIM), lambda b, g, i: (b, 0, 0)),
                  pl.BlockSpec((None, t, kvw), lambda b, g, i: (b, 0, g)),
                  pl.BlockSpec((None, t, kvw), lambda b, g, i: (b, 0, g)),
                  pl.BlockSpec(wmap_t.shape, lambda b, g, i: (0, 0)),
                  pl.BlockSpec(expand.shape, lambda b, g, i: (0, 0, 0))],
        out_specs=pl.BlockSpec((None, Q_BLOCK, gw), lambda b, g, i: (b, i, g)),
        scratch_shapes=[pltpu.VMEM((n_chunks, rows, SEL_CHUNK), F32),
                        pltpu.VMEM((rows, LANES), F32), pltpu.VMEM((rows, LANES), F32),
                        pltpu.VMEM((rows, t), BF),
                        pltpu.VMEM((1, rows, WINDOW + Q_BLOCK), F32),
                        pltpu.VMEM((rows, LANES), F32), pltpu.VMEM((rows, LANES), F32),
                        pltpu.VMEM((rows, WINDOW + Q_BLOCK), BF)],
        compiler_params=_cparams(3, 48),
        name="attn_prompt",
    )(q, gates, cmp_flat, kv_sel, kv_win, wmap_t, expand)


def _attn_sample_a_kernel(q_ref, cmp_ref, win_ref, wmap_ref, ocmp_ref, owin_ref, idx_ref, val_ref, *,
                          scale, q_pos, n_cmp, n_slc, win_pos0):
    q = q_ref[...]
    n_h = q.shape[0]
    rq = n_h // N_KV_HEADS
    row_g = _div_pow2(lax.broadcasted_iota(jnp.int32, (n_h, 1), 0), rq)
    cmp_rows = cmp_ref.shape[0] // KV_SLABS
    win_rows = win_ref.shape[0] // KV_SLABS
    n_io = lax.broadcasted_iota(jnp.int32, (1, cmp_rows), 1)
    valid_c = (n_io < n_cmp) & ((n_io * CMP_STRIDE + (CMP_LEN - 1)) <= q_pos)
    kp = win_pos0 + lax.broadcasted_iota(jnp.int32, (1, win_rows), 1)
    valid_w = (kp <= q_pos) & (kp > q_pos - WINDOW)
    o_cmp = jnp.zeros((n_h, HEAD_DIM), F32)
    o_win = jnp.zeros((n_h, HEAD_DIM), F32)
    imp = jnp.zeros((n_h, wmap_ref.shape[1]), F32)
    for g in range(N_KV_HEADS):
        in_g = row_g == g
        kc = cmp_ref[pl.ds(2 * g, cmp_rows, stride=KV_SLABS), :].astype(BF)
        vc = cmp_ref[pl.ds(2 * g + 1, cmp_rows, stride=KV_SLABS), :].astype(BF)
        p = _masked_softmax(_dot_t(q, kc) * scale, valid_c).astype(BF)
        o_cmp = jnp.where(in_g, _dot(p, vc), o_cmp)
        imp_rows = _dot(p, wmap_ref[...])
        imp_g = jnp.sum(jnp.where(in_g, imp_rows, 0.0), axis=0, keepdims=True)
        imp = jnp.where(in_g, imp_g, imp)
        kw = win_ref[pl.ds(2 * g, win_rows, stride=KV_SLABS), :].astype(BF)
        vw = win_ref[pl.ds(2 * g + 1, win_rows, stride=KV_SLABS), :].astype(BF)
        pw = _masked_softmax(_dot_t(q, kw) * scale, valid_w).astype(BF)
        o_win = jnp.where(in_g, _dot(pw, vw), o_win)
    ocmp_ref[...] = o_cmp
    owin_ref[...] = o_win

    j_io = lax.broadcasted_iota(jnp.int32, (1, imp.shape[1]), 1)
    q_blk = q_pos // SEL_BLOCK
    causal = j_io <= q_blk
    forced = (j_io == 0) | (causal & (j_io > q_blk - N_LOCAL_SEL))
    imp = jnp.where(forced, imp + FORCE_BONUS, imp)
    imp = jnp.where(causal, imp, -jnp.inf)
    rank = _rank_desc(imp, n_slc)
    jf = j_io.astype(F32)
    k_io = lax.broadcasted_iota(jnp.int32, (1, idx_ref.shape[1]), 1)
    idx = jnp.zeros(idx_ref.shape, F32)
    val = jnp.zeros(idx_ref.shape, F32)
    for k in range(min(N_SEL, n_slc)):
        hit = rank == k
        ik = jnp.sum(jnp.where(hit, jf, 0.0), axis=1, keepdims=True)
        vk = jnp.sum(jnp.where(hit, jnp.where(imp > -jnp.inf, 1.0, 0.0), 0.0), axis=1, keepdims=True)
        idx = jnp.where(k_io == k, ik, idx)
        val = jnp.where(k_io == k, vk, val)
    idx_ref[...] = idx.astype(jnp.int32)
    val_ref[...] = val.astype(jnp.int32)


def _attn_sample_a(q, cmp_blocks, win_rows, *, scale, q_pos, n_cmp, n_slc, win_pos0):
    nb, n_h, d = q.shape
    n_cols = -(-n_slc // LANES) * LANES
    wmap = jnp.asarray(_overlap_map(cmp_blocks.shape[1] // KV_SLABS, n_slc, n_cols), BF)
    o_sds = jax.ShapeDtypeStruct((nb, n_h, d), F32)
    i_sds = jax.ShapeDtypeStruct((nb, n_h, LANES), jnp.int32)
    blk = lambda a: pl.BlockSpec((None,) + a.shape[1:], lambda b: (b, 0, 0))
    ospec = pl.BlockSpec((None, n_h, d), lambda b: (b, 0, 0))
    ispec = pl.BlockSpec((None, n_h, LANES), lambda b: (b, 0, 0))
    return pl.pallas_call(
        functools.partial(_attn_sample_a_kernel, scale=scale, q_pos=q_pos, n_cmp=n_cmp, n_slc=n_slc,
                          win_pos0=win_pos0),
        out_shape=(o_sds, o_sds, i_sds, i_sds),
        grid=(nb,),
        in_specs=[blk(q), blk(cmp_blocks), blk(win_rows), pl.BlockSpec(wmap.shape, lambda b: (0, 0))],
        out_specs=(ospec, ospec, ispec, ispec),
        compiler_params=_cparams(1, 32),
        name="attn_sample_a",
    )(q, cmp_blocks, win_rows, wmap)


def _attn_sample_b_kernel(idx_ref, val_ref, pt_ref, q_ref, new_ref, gt_ref, ocmp_ref, owin_ref, *rest,
                          scale, q_pos, past_len):
    blk_refs = rest[:N_SEL]
    o_ref = rest[N_SEL]
    b = pl.program_id(0)
    g = pl.program_id(1)
    q = q_ref[...]
    n_h = q.shape[0]
    rq = n_h // N_KV_HEADS
    base = (b * N_KV_HEADS + g) * N_SEL
    n_keys = N_SEL * SEL_BLOCK
    lane = lax.broadcasted_iota(jnp.int32, (1, n_keys), 1)
    k_new = new_ref[pl.ds(2 * g, 1), :]
    v_new = new_ref[pl.ds(2 * g + 1, 1), :]
    ks, vs = [], []
    kpos = _mod_pow2(lane, SEL_BLOCK)
    kval = jnp.zeros((1, n_keys), jnp.int32)
    lane_blk = _div_pow2(lane, SEL_BLOCK)
    for k in range(N_SEL):
        blk = idx_ref[base + k]
        is_new = blk * SEL_BLOCK >= past_len
        ks.append(jnp.where(is_new, k_new, blk_refs[k][pl.ds(2 * g, SEL_BLOCK, stride=KV_SLABS), :]).astype(BF))
        vs.append(jnp.where(is_new, v_new, blk_refs[k][pl.ds(2 * g + 1, SEL_BLOCK, stride=KV_SLABS), :]).astype(BF))
        in_k = lane_blk == k
        kpos = kpos + jnp.where(in_k, blk * SEL_BLOCK, 0)
        kval = jnp.where(in_k, val_ref[base + k], kval)
    valid = (kval > 0) & (kpos <= q_pos)
    p = _masked_softmax(_dot_t(q, jnp.concatenate(ks, axis=0)) * scale, valid).astype(BF)
    o_sel = _dot(p, jnp.concatenate(vs, axis=0))
    gt = gt_ref[...]
    in_g = _div_pow2(lax.broadcasted_iota(jnp.int32, (n_h, 1), 0), rq) == g

    @pl.when(g == 0)
    def _():
        o_ref[...] = gt[:, 0:1] * ocmp_ref[...] + gt[:, 2:3] * owin_ref[...]

    o_ref[...] += jnp.where(in_g, gt[:, 1:2] * o_sel, 0.0)


def _attn_sample_b(sel_idx, sel_val, page_table, q, kv_sel_new, gates, o_cmp, o_win, pool_sel, *,
                   scale, q_pos, past_len):
    nb, n_h, d = q.shape
    halves = PAGE_SIZE // SEL_BLOCK
    blk_rows = SEL_BLOCK * KV_SLABS
    pool = pool_sel.reshape(pool_sel.shape[0] // blk_rows, blk_rows, d)

    def blk_spec(k):
        def index_map(b, g, idx, val, pt):
            row0 = jnp.minimum(idx[(b * N_KV_HEADS + g) * N_SEL + k] * SEL_BLOCK, past_len - 1)
            page = lax.shift_right_logical(row0, _log2(PAGE_SIZE))
            half = lax.shift_right_logical(lax.bitwise_and(row0, PAGE_SIZE - 1), _log2(SEL_BLOCK))
            return (pt[b, page] * halves + half, 0, 0)
        return pl.BlockSpec((None, blk_rows, d), index_map)

    per_b = lambda a: pl.BlockSpec((None,) + a.shape[1:], lambda b, g, idx, val, pt: (b, 0, 0))
    new_rows = kv_sel_new.reshape(nb, KV_SLABS, d)
    grid_spec = pltpu.PrefetchScalarGridSpec(
        num_scalar_prefetch=3,
        grid=(nb, N_KV_HEADS),
        in_specs=[per_b(q), per_b(new_rows), per_b(gates), per_b(o_cmp), per_b(o_win)]
        + [blk_spec(k) for k in range(N_SEL)],
        out_specs=pl.BlockSpec((None, n_h, d), lambda b, g, idx, val, pt: (b, 0, 0)),
    )
    return pl.pallas_call(
        functools.partial(_attn_sample_b_kernel, scale=scale, q_pos=q_pos, past_len=past_len),
        out_shape=jax.ShapeDtypeStruct((nb, n_h, d), F32),
        grid_spec=grid_spec,
        compiler_params=_cparams(2, 32),
        name="attn_sample_b",
    )(sel_idx, sel_val, page_table, q, new_rows, gates, o_cmp, o_win, *([pool] * N_SEL))


def kernel(x_prompt, x_sample, cache_cmp_kv, cache_sel_kv, cache_win_kv, page_table, c_prompt, c_sample, w_ada, b_ada, ln_g, ln_b, a_w_in, a_v_g, a_v_b, a_w_s, a_b_s, a_w_out, w_kv, cmp_pe, cmp_w1, cmp_b1, cmp_w2, cmp_b2, b_w_qg, b_w_o, ffn_w_in, ffn_w_out):
    nb, t, d = x_prompt.shape
    ns, tq, _ = x_sample.shape
    assert tq == 1, "the sample path handles one new token per sequence"
    depth = w_ada.shape[0]
    n_a = a_w_in.shape[0]
    alpha = float((2 * depth) ** 0.25)
    scale = float(HEAD_DIM ** -0.5)
    n_pool = cache_cmp_kv.shape[0]
    past_len = page_table.shape[1] * PAGE_SIZE
    kvw = N_KV_HEADS * 2 * HEAD_DIM
    q_width = b_w_o.shape[1]
    n_heads = q_width // HEAD_DIM
    rq = n_heads // N_KV_HEADS

    mod = _ada_modulation(jnp.concatenate([c_prompt, c_sample], axis=0), w_ada, b_ada)

    def mods(layer, sub):
        m = mod[layer * 2 + sub]
        out = []
        for part in range(3):
            sl = m[:, part * d:(part + 1) * d]
            out.append((sl[:nb].reshape(nb, 1, d), sl[nb:].reshape(1, ns, d)))
        return out

    wb = lambda w: w.astype(BF)
    hidden = cmp_w1.shape[-1]
    halves = CMP_LEN // CMP_STRIDE
    assert halves == 2, "a compression block spans two stride-chunks"
    w1all = wb(cmp_w1.reshape(2, halves, CMP_STRIDE, HEAD_DIM, hidden)
               .transpose(2, 3, 0, 1, 4).reshape(CMP_STRIDE * HEAD_DIM, 2 * halves * hidden))
    pe16 = jnp.pad(cmp_pe.reshape(2 * halves, CMP_STRIDE * HEAD_DIM), ((0, 16 - 2 * halves), (0, 0)))
    w2all = wb(cmp_w2.transpose(1, 0, 2).reshape(hidden, 2 * HEAD_DIM))

    a_w_out_b, b_w_o_b, ffn_w_in_b, ffn_w_out_b = wb(a_w_out), wb(b_w_o), wb(ffn_w_in), wb(ffn_w_out)

    tm_p = 512
    tpg_p = t // tm_p
    prompt = dict(tm=tm_p, tiles_per_group=tpg_p)
    sample = dict(tm=ns, tiles_per_group=1)

    xp = x_prompt.reshape(nb * t, d)
    xs = x_sample.reshape(ns, d)
    hp = hs = None
    chunk_v = []
    outs = {}
    for layer in range(depth):
        (sh_p, sh_s), (sc_p, sc_s), (gt_p, gt_s) = mods(layer, 0)
        (fsh_p, fsh_s), (fsc_p, fsc_s), (fgt_p, fgt_s) = mods(layer, 1)
        if layer < n_a:
            w_in = wb(a_w_in[layer])
            ap = _gmlp_in(xp, sc_p, sh_p, w_in, a_v_g[layer], a_v_b[layer], a_w_s[layer], a_b_s[layer],
                          chunked=True, **prompt)
            as_, v_rows = _gmlp_in(xs, sc_s, sh_s, w_in, a_v_g[layer], a_v_b[layer], a_w_s[layer], a_b_s[layer],
                                   chunked=False, **sample)
            chunk_v.append(v_rows.reshape(ns, tq, -1))
            xp, hp = _proj_res_ln(ap, a_w_out_b, xp, gt_p, ln_g[layer, 0], ln_b[layer, 0], (fsc_p, fsh_p),
                                  alpha=alpha, layer=layer, **prompt)
            xs, hs = _proj_res_ln(as_, a_w_out_b, xs, gt_s, ln_g[layer, 0], ln_b[layer, 0], (fsc_s, fsh_s),
                                  alpha=alpha, layer=layer, **sample)
        else:
            if layer == n_a:
                w_kvb = wb(w_kv)
                cmp_p, sel_p, win_p, sel_pb, win_pb = _kv_proj(xp, w_kvb, tm=tm_p)
                cmp_s, sel_s, win_s, _, _ = _kv_proj(xs, w_kvb, tm=ns)
                pages_per_seq_p = t // PAGE_SIZE
                ident = jnp.arange(nb * pages_per_seq_p, dtype=jnp.int32).reshape(nb, pages_per_seq_p)
                cmp_blocks_p = _compress(cmp_p, ident, w1all, pe16, cmp_b1, w2all, cmp_b2)
                assert (past_len + tq) // CMP_STRIDE == past_len // CMP_STRIDE
                cmp_blocks_s = _compress(cache_cmp_kv.reshape(n_pool * PAGE_FLAT_ROWS, HEAD_DIM), page_table,
                                         w1all, pe16, cmp_b1, w2all, cmp_b2)
                n_win = cache_win_kv.shape[1]
                win_keys_s = jnp.concatenate([cache_win_kv.reshape(ns, n_win * KV_SLABS, HEAD_DIM),
                                              win_s.reshape(ns, tq * KV_SLABS, HEAD_DIM)],
                                             axis=1)[:, -n_win * KV_SLABS:]
                outs["kv_p"] = (cmp_p, sel_p, win_p)
                outs["kv_pb"] = (sel_pb.reshape(nb, t, kvw), win_pb.reshape(nb, t, kvw))
                outs["kv_s"] = (cmp_s, sel_s, win_keys_s)
            j = layer - n_a
            w_qg = b_w_qg[j]
            w_q = wb(w_qg[:, :q_width])
            n_gates = w_qg.shape[1] - q_width
            w_gate = wb(jnp.pad(w_qg[:, q_width:], ((0, 0), (0, LANES - n_gates))))
            assert layer > 0, "an attention layer must follow another layer"
            q_p, g_p = _qg_proj(hp, w_q, w_gate, tm=tm_p, gates_per_group=rq * 3)
            q_s, g_s = _qg_proj(hs, w_q, w_gate, tm=ns, gates_per_group=rq * 3)
            o_p = _attn_prompt(q_p.reshape(nb, t, q_width), g_p.reshape(nb, t, N_KV_HEADS * LANES),
                               cmp_blocks_p, outs["kv_pb"][0], outs["kv_pb"][1], scale=scale)
            n_cmp_s = (past_len + tq) // CMP_STRIDE - CMP_LEN // CMP_STRIDE + 1
            n_slc_s = -(-(past_len + tq) // SEL_BLOCK)
            q_s3 = q_s.reshape(ns, n_heads, HEAD_DIM)
            o_cmp_s, o_win_s, idx_s, val_s = _attn_sample_a(
                q_s3, cmp_blocks_s, outs["kv_s"][2], scale=scale, q_pos=past_len, n_cmp=n_cmp_s, n_slc=n_slc_s,
                win_pos0=past_len + tq - outs["kv_s"][2].shape[1] // KV_SLABS)
            sel_idx = idx_s[:, ::rq, :N_SEL].reshape(-1)
            sel_val = val_s[:, ::rq, :N_SEL].reshape(-1)
            gates_s = g_s.reshape(ns, N_KV_HEADS, LANES)[:, :, :rq * 3].reshape(ns, n_heads, 3)
            o_s = _attn_sample_b(sel_idx, sel_val, page_table, q_s3, outs["kv_s"][1], gates_s, o_cmp_s, o_win_s,
                                 cache_sel_kv.reshape(n_pool * PAGE_FLAT_ROWS, HEAD_DIM), scale=scale,
                                 q_pos=past_len, past_len=past_len)
            xp, hp = _proj_res_ln(o_p.reshape(nb * t, q_width), b_w_o_b, xp, gt_p, ln_g[layer, 0], ln_b[layer, 0],
                                  (fsc_p, fsh_p), alpha=alpha, layer=j, **prompt)
            xs, hs = _proj_res_ln(o_s.reshape(ns, q_width), b_w_o_b, xs, gt_s, ln_g[layer, 0], ln_b[layer, 0],
                                  (fsc_s, fsh_s), alpha=alpha, layer=j, **sample)
        fp = _ffn_in(hp, ffn_w_in_b, layer, tm=2 * tm_p)
        fs = _ffn_in(hs, ffn_w_in_b, layer, tm=ns)
        nxt_p = nxt_s = None
        if n_a <= layer + 1 < depth:
            (nsh_p, nsh_s), (nsc_p, nsc_s), _ = mods(layer + 1, 0)
            nxt_p, nxt_s = (nsc_p, nsh_p), (nsc_s, nsh_s)
        tm_f = tm_p // 2
        xp, hp = _proj_res_ln(fp, ffn_w_out_b, xp, fgt_p, ln_g[layer, 1], ln_b[layer, 1], nxt_p, alpha=alpha,
                              layer=layer, tm=tm_f, tiles_per_group=t // tm_f)
        xs, hs = _proj_res_ln(fs, ffn_w_out_b, xs, fgt_s, ln_g[layer, 1], ln_b[layer, 1], nxt_s, alpha=alpha,
                              layer=layer, **sample)

    cmp_p, sel_p, win_p = outs["kv_p"]
    cmp_s, sel_s, win_keys_s = outs["kv_s"]
    kv_shape = (N_KV_HEADS, 2, HEAD_DIM)
    n_win_p = min(WINDOW, t)
    return (xp.reshape(nb, t, d), xs.reshape(ns, tq, d),
            cmp_p.reshape((nb, t) + kv_shape), sel_p.reshape((nb, t) + kv_shape),
            win_p.reshape((nb, t) + kv_shape)[:, -n_win_p:],
            cmp_s.reshape((ns, tq) + kv_shape), sel_s.reshape((ns, tq) + kv_shape),
            win_keys_s.reshape((ns, -1) + kv_shape),
            jnp.stack(chunk_v))
```

```python
import functools

import numpy as np
import jax
import jax.numpy as jnp
from jax import lax
from jax.experimental import pallas as pl
from jax.experimental.pallas import tpu as pltpu

BF = jnp.bfloat16
F32 = jnp.float32

LN_EPS = 1e-5
CHUNK = 128
HEAD_DIM = 128
N_KV_HEADS = 4
CMP_LEN = 32
CMP_STRIDE = 16
SEL_BLOCK = 64
N_SEL = 16
N_LOCAL_SEL = 2
FORCE_BONUS = 1e4
LOG2_E = 1.4426950408889634
WINDOW = 512
Q_BLOCK = 256
PAGE_SIZE = 128

LANES = 128
MIB = 1024 * 1024


def _cparams(n_axes, vmem_mib):
    return pltpu.CompilerParams(dimension_semantics=("arbitrary",) * n_axes,
                                vmem_limit_bytes=vmem_mib * MIB)


def _dot(a, b):
    return jnp.dot(a, b, preferred_element_type=F32)


def _dot_t(a, b):
    return lax.dot_general(a, b, (((1,), (1,)), ((), ())), preferred_element_type=F32)


def _layer_norm(x, g, b):
    mu = jnp.mean(x, axis=-1, keepdims=True)
    xc = x - mu
    var = jnp.mean(xc * xc, axis=-1, keepdims=True)
    return xc * lax.rsqrt(var + LN_EPS) * g + b


def _masked_softmax(s, valid):
    s = jnp.where(valid, s, -jnp.inf)
    m = jnp.max(s, axis=-1, keepdims=True)
    m = jnp.where(m == -jnp.inf, 0.0, m)
    e = jnp.exp(s - m)
    d = jnp.maximum(jnp.sum(e, axis=-1, keepdims=True), 1e-30)
    return e * (1.0 / d)


def _log2(n):
    assert n & (n - 1) == 0, "power of two expected"
    return n.bit_length() - 1


def _div_pow2(x, n):
    return lax.shift_right_logical(x, jnp.int32(_log2(n)))


def _mod_pow2(x, n):
    assert n & (n - 1) == 0, "power of two expected"
    return lax.bitwise_and(x, jnp.int32(n - 1))


def _rank_desc(imp, n):
    j_io = lax.broadcasted_iota(jnp.int32, (1, imp.shape[1]), 1)
    rank = jnp.zeros(imp.shape, F32)
    for jp in range(n):
        col = imp[:, jp:jp + 1]
        before = jnp.where(col == imp, jnp.where(jp < j_io, 1.0, 0.0), jnp.where(col > imp, 1.0, 0.0))
        rank = rank + before
    return rank


def _ada_kernel(c_ref, w_ref, b_ref, o_ref):
    a = jax.nn.silu(c_ref[...]).astype(BF)
    o_ref[...] = _dot(a, w_ref[...].astype(BF)) + b_ref[...]


def _ada_modulation(c_all, w_ada, b_ada):
    n_sub = w_ada.shape[0] * w_ada.shape[1]
    d, n3 = w_ada.shape[2], w_ada.shape[3]
    m = c_all.shape[0]
    tn = 512
    w = w_ada.reshape(n_sub, d, n3)
    b = b_ada.reshape(n_sub, 1, n3)
    return pl.pallas_call(
        _ada_kernel,
        out_shape=jax.ShapeDtypeStruct((n_sub, m, n3), F32),
        grid=(n_sub, n3 // tn),
        in_specs=[pl.BlockSpec((m, d), lambda s, j: (0, 0)),
                  pl.BlockSpec((None, d, tn), lambda s, j: (s, 0, j)),
                  pl.BlockSpec((None, 1, tn), lambda s, j: (s, 0, j))],
        out_specs=pl.BlockSpec((None, m, tn), lambda s, j: (s, 0, j)),
        compiler_params=_cparams(2, 32),
        name="ada_modulation",
    )(c_all, w, b)


def _mod_spec(mod, tiles_per_group):
    rb, d = mod.shape[1], mod.shape[2]
    return pl.BlockSpec((None, rb, d), lambda i: (i // tiles_per_group, 0, 0))


def _gmlp_in_kernel(x_ref, sc_ref, sh_ref, w_ref, vg_ref, vb_ref, ws_ref, bs_ref, *rest,
                    n_j, tn, chunked):
    if chunked:
        o_ref, z_scr = rest
    else:
        o_ref, v_ref, z_scr = rest
    h = (x_ref[...] * (1.0 + sc_ref[...]) + sh_ref[...]).astype(BF)
    for jj in range(n_j):
        z_scr[jj] = jax.nn.gelu(_dot(h, w_ref[:, jj * tn:(jj + 1) * tn]))

    tm = z_scr.shape[1]
    half = n_j // 2
    e_a = half * tn
    s1 = jnp.zeros((tm, 1), F32)
    for jj in range(half):
        s1 = s1 + jnp.sum(z_scr[half + jj], axis=-1, keepdims=True)
    mu = s1 / e_a
    s2 = jnp.zeros((tm, 1), F32)
    for jj in range(half):
        vc = z_scr[half + jj] - mu
        s2 = s2 + jnp.sum(vc * vc, axis=-1, keepdims=True)
    rstd = lax.rsqrt(s2 / e_a + LN_EPS)
    gpt = tn // CHUNK
    if chunked:
        row = lax.broadcasted_iota(jnp.int32, (CHUNK, CHUNK), 0)
        col = lax.broadcasted_iota(jnp.int32, (CHUNK, CHUNK), 1)
        for jj in range(half):
            for gi in range(gpt):
                g = jj * gpt + gi
                cs = slice(gi * CHUNK, (gi + 1) * CHUNK)
                wm = jnp.where(row >= col, ws_ref[g], 0.0).astype(BF)
                bcol = bs_ref[:, g:g + 1]
                vg = vg_ref[:, g * CHUNK:(g + 1) * CHUNK]
                vb = vb_ref[:, g * CHUNK:(g + 1) * CHUNK]
                for c in range(tm // CHUNK):
                    rs = slice(c * CHUNK, (c + 1) * CHUNK)
                    vn = (z_scr[half + jj, rs, cs] - mu[rs]) * rstd[rs] * vg + vb
                    mixed = _dot(wm, vn.astype(BF)) + bcol
                    o_ref[rs, g * CHUNK:(g + 1) * CHUNK] = (z_scr[jj, rs, cs] * mixed).astype(BF)
    else:
        for jj in range(half):
            cs = slice(jj * tn, (jj + 1) * tn)
            vn = (z_scr[half + jj] - mu) * rstd * vg_ref[:, cs] + vb_ref[:, cs]
            v_ref[:, cs] = vn
            mixed = ws_ref[:, cs] * vn + bs_ref[:, cs]
            o_ref[:, cs] = (z_scr[jj] * mixed).astype(BF)


def _gmlp_in(x, sc, sh, w_in, v_g, v_b, w_s, b_s, *, tm, tiles_per_group, chunked):
    r, d = x.shape
    n2 = w_in.shape[1]
    e_a = n2 // 2
    tn = 512
    n_j = n2 // tn
    n_g = w_s.shape[0]
    full = lambda a: pl.BlockSpec(a.shape, lambda i: (0,) * a.ndim)
    row_tile = pl.BlockSpec((tm, e_a), lambda i: (i, 0))
    if chunked:
        ws_arr = w_s
        bs_arr = b_s.T
        out_shape = jax.ShapeDtypeStruct((r, e_a), BF)
        out_specs = row_tile
    else:
        ws_arr = jnp.repeat(w_s[:, 0, 0], e_a // n_g).reshape(1, e_a)
        bs_arr = jnp.repeat(b_s[:, 0], e_a // n_g).reshape(1, e_a)
        out_shape = (jax.ShapeDtypeStruct((r, e_a), BF), jax.ShapeDtypeStruct((r, e_a), F32))
        out_specs = (row_tile, row_tile)
    vg2, vb2 = v_g.reshape(1, e_a), v_b.reshape(1, e_a)
    return pl.pallas_call(
        functools.partial(_gmlp_in_kernel, n_j=n_j, tn=tn, chunked=chunked),
        out_shape=out_shape,
        grid=(r // tm,),
        in_specs=[pl.BlockSpec((tm, d), lambda i: (i, 0)),
                  _mod_spec(sc, tiles_per_group), _mod_spec(sh, tiles_per_group),
                  pl.BlockSpec(w_in.shape, lambda i: (0, 0), pipeline_mode=pl.Buffered(1)),
                  full(vg2), full(vb2), full(ws_arr), full(bs_arr)],
        out_specs=out_specs,
        scratch_shapes=[pltpu.VMEM((n_j, tm, tn), F32)],
        compiler_params=_cparams(1, 48),
        name="gmlp_in_chunked" if chunked else "gmlp_in_single",
    )(x, sc, sh, w_in, vg2, vb2, ws_arr, bs_arr)


LN_ROWS = 16


def _proj_res_ln_kernel(a_ref, w_ref, x_ref, gate_ref, lg_ref, lb_ref, *rest, alpha, emit_next):
    if emit_next:
        sc_ref, sh_ref, y_ref, h_ref, acc = rest
    else:
        y_ref, acc = rest
    acc[...] = _dot(a_ref[...].astype(BF), w_ref[...])
    tm = acc.shape[0]
    per_row_mod = gate_ref.shape[0] > 1
    step = min(LN_ROWS, tm)
    assert tm % step == 0
    for c in range(tm // step):
        rs = slice(c * step, (c + 1) * step)
        ms = rs if per_row_mod else slice(None)
        t = alpha * x_ref[rs, :] + (1.0 + gate_ref[ms, :]) * acc[rs, :]
        y = _layer_norm(t, lg_ref[...], lb_ref[...])
        y_ref[rs, :] = y
        if emit_next:
            h_ref[rs, :] = (y * (1.0 + sc_ref[ms, :]) + sh_ref[ms, :]).astype(BF)


def _proj_res_ln(a, w, x, gate, ln_g, ln_b, nxt, *, alpha, tm, tiles_per_group, layer=0):
    r, kdim = a.shape
    d = w.shape[2]
    emit_next = nxt is not None
    mod_spec = lambda m: _mod_spec(m, tiles_per_group)
    in_specs = [pl.BlockSpec((tm, kdim), lambda i: (i, 0)),
                pl.BlockSpec((None, kdim, d), lambda i: (layer, 0, 0), pipeline_mode=pl.Buffered(1)),
                pl.BlockSpec((tm, d), lambda i: (i, 0)),
                mod_spec(gate),
                pl.BlockSpec((1, d), lambda i: (0, 0)),
                pl.BlockSpec((1, d), lambda i: (0, 0))]
    args = [a, w, x, gate, ln_g.reshape(1, d), ln_b.reshape(1, d)]
    out_shape = [jax.ShapeDtypeStruct((r, d), F32)]
    out_specs = [pl.BlockSpec((tm, d), lambda i: (i, 0))]
    if emit_next:
        in_specs += [mod_spec(nxt[0]), mod_spec(nxt[1])]
        args += [nxt[0], nxt[1]]
        out_shape.append(jax.ShapeDtypeStruct((r, d), BF))
        out_specs.append(pl.BlockSpec((tm, d), lambda i: (i, 0)))
    res = pl.pallas_call(
        functools.partial(_proj_res_ln_kernel, alpha=alpha, emit_next=emit_next),
        out_shape=tuple(out_shape),
        grid=(r // tm,),
        in_specs=in_specs,
        out_specs=tuple(out_specs),
        scratch_shapes=[pltpu.VMEM((tm, d), F32)],
        compiler_params=_cparams(1, 56),
        name="proj_res_ln",
    )(*args)
    return res if emit_next else (res[0], None)


def _ffn_in_kernel(h_ref, wg_ref, wu_ref, o_ref):
    h = h_ref[...]
    o_ref[...] = (jax.nn.silu(_dot(h, wg_ref[...])) * _dot(h, wu_ref[...])).astype(BF)


def _ffn_in(h, w_in, layer, *, tm):
    r, d = h.shape
    d_ff = w_in.shape[2] // 2
    tn = 512
    n_j = d_ff // tn
    return pl.pallas_call(
        _ffn_in_kernel,
        out_shape=jax.ShapeDtypeStruct((r, d_ff), BF),
        grid=(r // tm, n_j),
        in_specs=[pl.BlockSpec((tm, d), lambda i, j: (i, 0)),
                  pl.BlockSpec((None, d, tn), lambda i, j: (layer, 0, j)),
                  pl.BlockSpec((None, d, tn), lambda i, j: (layer, 0, j + n_j))],
        out_specs=pl.BlockSpec((tm, tn), lambda i, j: (i, j)),
        compiler_params=_cparams(2, 40),
        name="ffn_in",
    )(h, w_in, w_in)


KV_SLABS = N_KV_HEADS * 2


def _kv_proj_kernel(y_ref, w_ref, f0_ref, f1_ref, f2_ref, b1_ref, b2_ref):
    xb = y_ref[...].astype(BF)
    tm = xb.shape[0]
    n = w_ref.shape[1] // 3
    for idx, (f_ref, b_ref) in enumerate(((f0_ref, None), (f1_ref, b1_ref), (f2_ref, b2_ref))):
        r = _dot(xb, w_ref[:, idx * n:(idx + 1) * n])
        for hk in range(KV_SLABS):
            f_ref[pl.ds(hk, tm, stride=KV_SLABS), :] = r[:, hk * HEAD_DIM:(hk + 1) * HEAD_DIM]
        if b_ref is not None:
            b_ref[...] = r.astype(BF)


def _kv_proj(y, w_kv, *, tm):
    r, d = y.shape
    n = w_kv.shape[1] // 3
    flat = jax.ShapeDtypeStruct((r * KV_SLABS, HEAD_DIM), F32)
    fspec = pl.BlockSpec((tm * KV_SLABS, HEAD_DIM), lambda i: (i, 0))
    wide = jax.ShapeDtypeStruct((r, n), BF)
    wspec = pl.BlockSpec((tm, n), lambda i: (i, 0))
    return pl.pallas_call(
        _kv_proj_kernel,
        out_shape=(flat, flat, flat, wide, wide),
        grid=(r // tm,),
        in_specs=[pl.BlockSpec((tm, d), lambda i: (i, 0)),
                  pl.BlockSpec(w_kv.shape, lambda i: (0, 0), pipeline_mode=pl.Buffered(1))],
        out_specs=(fspec, fspec, fspec, wspec, wspec),
        compiler_params=_cparams(1, 48),
        name="kv_proj",
    )(y, w_kv)


def _qg_proj_kernel(h_ref, wq_ref, wg_ref, q_ref, g_ref, *, gates_per_group):
    h = h_ref[...]
    q_ref[...] = _dot(h, wq_ref[...]).astype(BF)
    gates = jax.nn.sigmoid(_dot(h, wg_ref[...]))
    for grp in range(g_ref.shape[1] // LANES):
        shift = (LANES - grp * gates_per_group) % LANES
        g_ref[:, grp * LANES:(grp + 1) * LANES] = gates if shift == 0 else pltpu.roll(gates, shift, 1)


def _qg_proj(h, w_q, w_g, *, tm, gates_per_group):
    r, d = h.shape
    qw = w_q.shape[1]
    gw = N_KV_HEADS * LANES
    resident = lambda w: pl.BlockSpec(w.shape, lambda i: (0, 0), pipeline_mode=pl.Buffered(1))
    return pl.pallas_call(
        functools.partial(_qg_proj_kernel, gates_per_group=gates_per_group),
        out_shape=(jax.ShapeDtypeStruct((r, qw), BF), jax.ShapeDtypeStruct((r, gw), F32)),
        grid=(r // tm,),
        in_specs=[pl.BlockSpec((tm, d), lambda i: (i, 0)), resident(w_q), resident(w_g)],
        out_specs=(pl.BlockSpec((tm, qw), lambda i: (i, 0)),
                   pl.BlockSpec((tm, gw), lambda i: (i, 0))),
        compiler_params=_cparams(1, 40),
        name="qg_proj",
    )(h, w_q, w_g)


PAGES_PER_STEP = 16
CHUNKS_PER_PAGE = PAGE_SIZE // CMP_STRIDE


PAGE_FLAT_ROWS = PAGE_SIZE * KV_SLABS
CHUNK_FLAT_ROWS = CMP_STRIDE * KV_SLABS


def _compress_kernel(pt_ref, *refs, n_steps):
    page_refs = refs[:PAGES_PER_STEP]
    w1_ref, pe_ref, b1_ref, w2_ref, b2_ref, o_ref, p0_scr, p1_scr = refs[PAGES_PER_STEP:]
    t = pl.program_id(1)
    rows_step = PAGES_PER_STEP * CHUNKS_PER_PAGE * N_KV_HEADS
    n_rows = n_steps * rows_step
    hidden = w2_ref.shape[0]

    @pl.when(t == 0)
    def _():
        p1_scr[:, pl.ds(n_rows, 8), :] = jnp.zeros((2, 8, hidden), F32)

    row0 = pl.multiple_of(t * rows_step, rows_step)
    for k in range(2):
        per_s = []
        for s in range(CMP_STRIDE):
            rows = [page_refs[p][pl.ds(c * CHUNK_FLAT_ROWS + s * KV_SLABS + k, N_KV_HEADS, stride=2), :]
                    for p in range(PAGES_PER_STEP) for c in range(CHUNKS_PER_PAGE)]
            per_s.append(jnp.concatenate(rows, axis=0))
        x = jnp.concatenate(per_s, axis=1).astype(BF)
        pr = _dot(x, w1_ref[:, k * 2 * hidden:(k + 1) * 2 * hidden])
        p0_scr[k, pl.ds(row0, rows_step), :] = pr[:, :hidden]
        p1_scr[k, pl.ds(row0, rows_step), :] = pr[:, hidden:]

    @pl.when(t == n_steps - 1)
    def _():
        cst = _dot(pe_ref[...].astype(BF), w1_ref[...])
        for k in range(2):
            c0 = 2 * k * hidden
            const = (cst[2 * k:2 * k + 1, c0:c0 + hidden] + cst[2 * k + 1:2 * k + 2, c0 + hidden:c0 + 2 * hidden]
                     + b1_ref[k:k + 1, :])
            pre = p0_scr[k, 0:n_rows, :] + p1_scr[k, N_KV_HEADS:n_rows + N_KV_HEADS, :] + const
            res = _dot(jax.nn.gelu(pre).astype(BF), w2_ref[:, k * HEAD_DIM:(k + 1) * HEAD_DIM]) + b2_ref[k:k + 1, :]
            o_ref[pl.ds(k, n_rows, stride=2), :] = res


def _compress(pages_flat, page_table, w1all, pe16, b1, w2all, b2):
    nb, n_pages = page_table.shape
    n_steps = n_pages // PAGES_PER_STEP
    n_rows = n_pages * CHUNKS_PER_PAGE * KV_SLABS
    hidden = w2all.shape[0]

    def page_spec(p):
        return pl.BlockSpec((PAGE_FLAT_ROWS, HEAD_DIM), lambda b, t, pt: (pt[b, t * PAGES_PER_STEP + p], 0))

    full = lambda a: pl.BlockSpec(a.shape, lambda b, t, pt: (0,) * a.ndim)
    grid_spec = pltpu.PrefetchScalarGridSpec(
        num_scalar_prefetch=1,
        grid=(nb, n_steps),
        in_specs=[page_spec(p) for p in range(PAGES_PER_STEP)] + [
            full(w1all), full(pe16), full(b1), full(w2all), full(b2)],
        out_specs=pl.BlockSpec((None, n_rows, HEAD_DIM), lambda b, t, pt: (b, 0, 0)),
        scratch_shapes=[pltpu.VMEM((2, n_rows // 2, hidden), F32), pltpu.VMEM((2, n_rows // 2 + 8, hidden), F32)],
    )
    return pl.pallas_call(
        functools.partial(_compress_kernel, n_steps=n_steps),
        out_shape=jax.ShapeDtypeStruct((nb, n_rows, HEAD_DIM), F32),
        grid_spec=grid_spec,
        compiler_params=_cparams(2, 48),
        name="compress_blocks",
    )(page_table, *([pages_flat] * PAGES_PER_STEP), w1all, pe16, b1, w2all, b2)


def _overlap_map(n_cmp_rows, n_slc, n_cols):
    cmp_lo = np.arange(n_cmp_rows)[:, None] * CMP_STRIDE
    slc_lo = np.arange(n_cols)[None, :] * SEL_BLOCK
    ov = np.minimum(cmp_lo + CMP_LEN, slc_lo + SEL_BLOCK) - np.maximum(cmp_lo, slc_lo)
    w = np.clip(ov, 0, None).astype(np.float32) / CMP_LEN
    w[:, n_slc:] = 0.0
    return w


SEL_CHUNK = 512


def _attn_prompt_kernel(q_ref, gt_ref, cmp_ref, sel_ref, win_ref, wmap_t_ref, exp_ref, o_ref,
                        s_sel, m_run, e_scr, s_win, m_win, e_win, *, scale, n_slc, n_qb, n_widths):
    g = pl.program_id(1)
    qb = pl.program_id(2)
    q4 = q_ref[...]
    nq = q4.shape[0]
    rq = q4.shape[1] // HEAD_DIM
    qs = jnp.concatenate([q4[:, r * HEAD_DIM:(r + 1) * HEAD_DIM] for r in range(rq)], axis=0)
    qpos = qb * nq + lax.broadcasted_iota(jnp.int32, (nq, 1), 0)
    rows = [slice(r * nq, (r + 1) * nq) for r in range(rq)]

    head_groups = [rows[:rq // 2], rows[rq // 2:]] if rq % 2 == 0 else [rows]

    def attend(n_chunks, chunk, s_scr, m_run, e_scr, keys, mask, values):
        width = n_chunks * chunk
        values_and_ones = jnp.concatenate([values, jnp.ones(values.shape, BF)], axis=1)
        m_run[...] = jnp.full(m_run.shape, -jnp.inf, F32)
        for grp in head_groups:
            g_rows = slice(grp[0].start, grp[-1].stop)
            for c in range(n_chunks):
                s = _dot_t(qs[g_rows], keys(c))
                ok = mask(c)
                for i, rs in enumerate(grp):
                    sm = jnp.where(ok, s[i * nq:(i + 1) * nq], -jnp.inf) * (scale * LOG2_E)
                    s_scr[c, rs, :] = sm
                    mx = m_run[rs, :]
                    for j in range(chunk // LANES):
                        mx = jnp.maximum(mx, sm[:, j * LANES:(j + 1) * LANES])
                    m_run[rs, :] = mx
        outs = []
        for grp in head_groups:
            g_rows = slice(grp[0].start, grp[-1].stop)
            m = jnp.max(m_run[g_rows, :], axis=-1, keepdims=True)
            m = jnp.where(m == -jnp.inf, 0.0, m)
            for c in range(n_chunks):
                for i, rs in enumerate(grp):
                    e = jnp.exp2(s_scr[c, rs, :] - m[i * nq:(i + 1) * nq])
                    e_scr[rs, c * chunk:(c + 1) * chunk] = e.astype(BF)
            pv = _dot(e_scr[g_rows, :width], values_and_ones)
            outs.append(pv[:, :HEAD_DIM] * (1.0 / jnp.maximum(pv[:, HEAD_DIM:], 1e-30)))
        return jnp.concatenate(outs, axis=0)

    def step(n_sel_chunks, first_q):
        span = WINDOW + nq
        k0 = pl.multiple_of(jnp.maximum(qb * nq - WINDOW, 0), nq)
        kp = k0 + lax.broadcasted_iota(jnp.int32, (1, span), 1)
        valid_w = lax.bitcast_convert_type(qpos - kp, jnp.uint32) < jnp.uint32(WINDOW)
        o_win = attend(1, span, s_win, m_win, e_win, lambda c: win_ref[pl.ds(k0, span), :HEAD_DIM],
                       lambda c: valid_w, win_ref[pl.ds(k0, span), HEAD_DIM:])

        n_cmp_rows = cmp_ref.shape[0] // KV_SLABS
        kc = cmp_ref[pl.ds(g * 2, n_cmp_rows, stride=KV_SLABS), :].astype(BF)
        vc = cmp_ref[pl.ds(g * 2 + 1, n_cmp_rows, stride=KV_SLABS), :].astype(BF)
        n_io = lax.broadcasted_iota(jnp.int32, (1, n_cmp_rows), 1)
        valid_c = (n_io * CMP_STRIDE + (CMP_LEN - 1)) <= qpos
        s_c = _dot_t(qs, kc) * scale
        ps = [_masked_softmax(s_c[rs], valid_c).astype(BF) for rs in rows]
        o_cmp = _dot(jnp.concatenate(ps, axis=0), vc)

        imp = _dot_t(wmap_t_ref[...], jnp.concatenate(ps, axis=1))
        n_rows = imp.shape[0]
        j_io = lax.broadcasted_iota(jnp.int32, (n_rows, 1), 0)
        q_blk = _div_pow2(qb * nq + lax.broadcasted_iota(jnp.int32, (1, nq), 1), SEL_BLOCK)
        causal = j_io <= q_blk
        forced = (j_io == 0) | (causal & (j_io > q_blk - N_LOCAL_SEL))
        imp = jnp.where(forced, imp + FORCE_BONUS, imp)
        imp = jnp.where(causal, imp, -jnp.inf)
        rank = jnp.zeros(imp.shape, F32)
        for jp in range(n_slc):
            other = imp[jp:jp + 1, :]
            rank = rank + jnp.where(other == imp, jnp.where(jp < j_io, 1.0, 0.0),
                                    jnp.where(other > imp, 1.0, 0.0))
        chosen_t = jnp.where(rank < min(N_SEL, n_slc), jnp.where(imp > -jnp.inf, 1.0, 0.0), 0.0)
        chosen_t = jnp.concatenate([chosen_t, jnp.zeros((LANES - n_rows, nq), F32)], axis=0)
        chosen = chosen_t.T.astype(BF)

        def sel_mask(c):
            in_sel = _dot(chosen, exp_ref[c])
            if (c + 1) * SEL_CHUNK <= first_q:
                return in_sel > 0.5
            t_io = c * SEL_CHUNK + lax.broadcasted_iota(jnp.int32, (1, SEL_CHUNK), 1)
            return jnp.where(t_io <= qpos, in_sel, 0.0) > 0.5

        o_sel = attend(n_sel_chunks, SEL_CHUNK, s_sel, m_run, e_scr,
                       lambda c: sel_ref[c * SEL_CHUNK:(c + 1) * SEL_CHUNK, :HEAD_DIM], sel_mask,
                       sel_ref[:n_sel_chunks * SEL_CHUNK, HEAD_DIM:])

        gt = gt_ref[...]
        outs = []
        for r, rs in enumerate(rows):
            outs.append(gt[:, 3 * r:3 * r + 1] * o_cmp[rs] + gt[:, 3 * r + 1:3 * r + 2] * o_sel[rs]
                        + gt[:, 3 * r + 2:3 * r + 3] * o_win[rs])
        o_ref[...] = jnp.concatenate(outs, axis=1).astype(BF)

    qb_per_width = n_qb // n_widths
    for wi in range(n_widths):
        n_sel_chunks = -(-(wi + 1) * qb_per_width * nq // SEL_CHUNK)
        pl.when((qb >= wi * qb_per_width) & (qb < (wi + 1) * qb_per_width))(
            functools.partial(step, n_sel_chunks, wi * qb_per_width * nq))


def _attn_prompt(q, gates, cmp_flat, kv_sel, kv_win, *, scale):
    nb, t, qw = q.shape
    n_g = N_KV_HEADS
    gw = qw // n_g
    kvw = 2 * HEAD_DIM
    cmp_rows = cmp_flat.shape[1]
    n_slc = t // SEL_BLOCK
    n_slc_rows = -(-n_slc // 8) * 8
    rq = gw // HEAD_DIM
    n_qb = t // Q_BLOCK
    n_chunks = t // SEL_CHUNK
    wmap_t = jnp.asarray(np.tile(_overlap_map(cmp_rows // KV_SLABS, n_slc, n_slc_rows), (rq, 1)).T, BF)
    key_blk = (np.arange(t) // SEL_BLOCK).reshape(n_chunks, 1, SEL_CHUNK)
    expand = jnp.asarray((key_blk == np.arange(LANES)[None, :, None]).astype(np.float32), BF)
    rows = rq * Q_BLOCK
    n_widths = 4 if n_qb % 4 == 0 else 1
    return pl.pallas_call(
        functools.partial(_attn_prompt_kernel, scale=scale, n_slc=n_slc, n_qb=n_qb, n_widths=n_widths),
        out_shape=jax.ShapeDtypeStruct((nb, t, qw), BF),
        grid=(nb, n_g, n_qb),
        in_specs=[pl.BlockSpec((None, Q_BLOCK, gw), lambda b, g, i: (b, i, g)),
                  pl.BlockSpec((None, Q_BLOCK, LANES), lambda b, g, i: (b, i, g)),
                  pl.BlockSpec((None, cmp_rows, HEAD_DIM), lambda b, g, i: (b, 0, 0)),
                  pl.BlockSpec((None, t, kvw), lambda b, g, i: (b, 0, g)),
                  pl.BlockSpec((None, t, kvw), lambda b, g, i: (b, 0, g)),
                  pl.BlockSpec(wmap_t.shape, lambda b, g, i: (0, 0)),
                  pl.BlockSpec(expand.shape, lambda b, g, i: (0, 0, 0))],
        out_specs=pl.BlockSpec((None, Q_BLOCK, gw), lambda b, g, i: (b, i, g)),
        scratch_shapes=[pltpu.VMEM((n_chunks, rows, SEL_CHUNK), F32),
                        pltpu.VMEM((rows, LANES), F32),
                        pltpu.VMEM((rows, t), BF),
                        pltpu.VMEM((1, rows, WINDOW + Q_BLOCK), F32),
                        pltpu.VMEM((rows, LANES), F32),
                        pltpu.VMEM((rows, WINDOW + Q_BLOCK), BF)],
        compiler_params=_cparams(3, 48),
        name="attn_prompt",
    )(q, gates, cmp_flat, kv_sel, kv_win, wmap_t, expand)


def _attn_sample_a_kernel(q_ref, cmp_ref, win_ref, wmap_ref, ocmp_ref, owin_ref, idx_ref, val_ref, *,
                          scale, q_pos, n_cmp, n_slc, win_pos0):
    q = q_ref[...]
    n_h = q.shape[0]
    rq = n_h // N_KV_HEADS
    row_g = _div_pow2(lax.broadcasted_iota(jnp.int32, (n_h, 1), 0), rq)
    cmp_rows = cmp_ref.shape[0] // KV_SLABS
    win_rows = win_ref.shape[0] // KV_SLABS
    n_io = lax.broadcasted_iota(jnp.int32, (1, cmp_rows), 1)
    valid_c = (n_io < n_cmp) & ((n_io * CMP_STRIDE + (CMP_LEN - 1)) <= q_pos)
    kp = win_pos0 + lax.broadcasted_iota(jnp.int32, (1, win_rows), 1)
    valid_w = (kp <= q_pos) & (kp > q_pos - WINDOW)
    o_cmp = jnp.zeros((n_h, HEAD_DIM), F32)
    o_win = jnp.zeros((n_h, HEAD_DIM), F32)
    imp = jnp.zeros((n_h, wmap_ref.shape[1]), F32)
    for g in range(N_KV_HEADS):
        in_g = row_g == g
        kc = cmp_ref[pl.ds(2 * g, cmp_rows, stride=KV_SLABS), :].astype(BF)
        vc = cmp_ref[pl.ds(2 * g + 1, cmp_rows, stride=KV_SLABS), :].astype(BF)
        p = _masked_softmax(_dot_t(q, kc) * scale, valid_c).astype(BF)
        o_cmp = jnp.where(in_g, _dot(p, vc), o_cmp)
        imp_rows = _dot(p, wmap_ref[...])
        imp_g = jnp.sum(jnp.where(in_g, imp_rows, 0.0), axis=0, keepdims=True)
        imp = jnp.where(in_g, imp_g, imp)
        kw = win_ref[pl.ds(2 * g, win_rows, stride=KV_SLABS), :].astype(BF)
        vw = win_ref[pl.ds(2 * g + 1, win_rows, stride=KV_SLABS), :].astype(BF)
        pw = _masked_softmax(_dot_t(q, kw) * scale, valid_w).astype(BF)
        o_win = jnp.where(in_g, _dot(pw, vw), o_win)
    ocmp_ref[...] = o_cmp
    owin_ref[...] = o_win

    j_io = lax.broadcasted_iota(jnp.int32, (1, imp.shape[1]), 1)
    q_blk = q_pos // SEL_BLOCK
    causal = j_io <= q_blk
    forced = (j_io == 0) | (causal & (j_io > q_blk - N_LOCAL_SEL))
    imp = jnp.where(forced, imp + FORCE_BONUS, imp)
    imp = jnp.where(causal, imp, -jnp.inf)
    rank = _rank_desc(imp, n_slc)
    jf = j_io.astype(F32)
    k_io = lax.broadcasted_iota(jnp.int32, (1, idx_ref.shape[1]), 1)
    idx = jnp.zeros(idx_ref.shape, F32)
    val = jnp.zeros(idx_ref.shape, F32)
    for k in range(min(N_SEL, n_slc)):
        hit = rank == k
        ik = jnp.sum(jnp.where(hit, jf, 0.0), axis=1, keepdims=True)
        vk = jnp.sum(jnp.where(hit, jnp.where(imp > -jnp.inf, 1.0, 0.0), 0.0), axis=1, keepdims=True)
        idx = jnp.where(k_io == k, ik, idx)
        val = jnp.where(k_io == k, vk, val)
    idx_ref[...] = idx.astype(jnp.int32)
    val_ref[...] = val.astype(jnp.int32)


def _attn_sample_a(q, cmp_blocks, win_rows, *, scale, q_pos, n_cmp, n_slc, win_pos0):
    nb, n_h, d = q.shape
    n_cols = -(-n_slc // LANES) * LANES
    wmap = jnp.asarray(_overlap_map(cmp_blocks.shape[1] // KV_SLABS, n_slc, n_cols), BF)
    o_sds = jax.ShapeDtypeStruct((nb, n_h, d), F32)
    i_sds = jax.ShapeDtypeStruct((nb, n_h, LANES), jnp.int32)
    blk = lambda a: pl.BlockSpec((None,) + a.shape[1:], lambda b: (b, 0, 0))
    ospec = pl.BlockSpec((None, n_h, d), lambda b: (b, 0, 0))
    ispec = pl.BlockSpec((None, n_h, LANES), lambda b: (b, 0, 0))
    return pl.pallas_call(
        functools.partial(_attn_sample_a_kernel, scale=scale, q_pos=q_pos, n_cmp=n_cmp, n_slc=n_slc,
                          win_pos0=win_pos0),
        out_shape=(o_sds, o_sds, i_sds, i_sds),
        grid=(nb,),
        in_specs=[blk(q), blk(cmp_blocks), blk(win_rows), pl.BlockSpec(wmap.shape, lambda b: (0, 0))],
        out_specs=(ospec, ospec, ispec, ispec),
        compiler_params=_cparams(1, 32),
        name="attn_sample_a",
    )(q, cmp_blocks, win_rows, wmap)


def _attn_sample_b_kernel(idx_ref, val_ref, pt_ref, q_ref, new_ref, gt_ref, ocmp_ref, owin_ref, *rest,
                          scale, q_pos, past_len):
    blk_refs = rest[:N_SEL]
    o_ref = rest[N_SEL]
    b = pl.program_id(0)
    g = pl.program_id(1)
    q = q_ref[...]
    n_h = q.shape[0]
    rq = n_h // N_KV_HEADS
    base = (b * N_KV_HEADS + g) * N_SEL
    n_keys = N_SEL * SEL_BLOCK
    lane = lax.broadcasted_iota(jnp.int32, (1, n_keys), 1)
    k_new = new_ref[pl.ds(2 * g, 1), :]
    v_new = new_ref[pl.ds(2 * g + 1, 1), :]
    ks, vs = [], []
    kpos = _mod_pow2(lane, SEL_BLOCK)
    kval = jnp.zeros((1, n_keys), jnp.int32)
    lane_blk = _div_pow2(lane, SEL_BLOCK)
    for k in range(N_SEL):
        blk = idx_ref[base + k]
        is_new = blk * SEL_BLOCK >= past_len
        ks.append(jnp.where(is_new, k_new, blk_refs[k][pl.ds(2 * g, SEL_BLOCK, stride=KV_SLABS), :]).astype(BF))
        vs.append(jnp.where(is_new, v_new, blk_refs[k][pl.ds(2 * g + 1, SEL_BLOCK, stride=KV_SLABS), :]).astype(BF))
        in_k = lane_blk == k
        kpos = kpos + jnp.where(in_k, blk * SEL_BLOCK, 0)
        kval = jnp.where(in_k, val_ref[base + k], kval)
    valid = (kval > 0) & (kpos <= q_pos)
    p = _masked_softmax(_dot_t(q, jnp.concatenate(ks, axis=0)) * scale, valid).astype(BF)
    o_sel = _dot(p, jnp.concatenate(vs, axis=0))
    gt = gt_ref[...]
    in_g = _div_pow2(lax.broadcasted_iota(jnp.int32, (n_h, 1), 0), rq) == g

    @pl.when(g == 0)
    def _():
        o_ref[...] = gt[:, 0:1] * ocmp_ref[...] + gt[:, 2:3] * owin_ref[...]

    o_ref[...] += jnp.where(in_g, gt[:, 1:2] * o_sel, 0.0)


def _attn_sample_b(sel_idx, sel_val, page_table, q, kv_sel_new, gates, o_cmp, o_win, pool_sel, *,
                   scale, q_pos, past_len):
    nb, n_h, d = q.shape
    halves = PAGE_SIZE // SEL_BLOCK
    blk_rows = SEL_BLOCK * KV_SLABS
    pool = pool_sel.reshape(pool_sel.shape[0] // blk_rows, blk_rows, d)

    def blk_spec(k):
        def index_map(b, g, idx, val, pt):
            row0 = jnp.minimum(idx[(b * N_KV_HEADS + g) * N_SEL + k] * SEL_BLOCK, past_len - 1)
            page = lax.shift_right_logical(row0, _log2(PAGE_SIZE))
            half = lax.shift_right_logical(lax.bitwise_and(row0, PAGE_SIZE - 1), _log2(SEL_BLOCK))
            return (pt[b, page] * halves + half, 0, 0)
        return pl.BlockSpec((None, blk_rows, d), index_map)

    per_b = lambda a: pl.BlockSpec((None,) + a.shape[1:], lambda b, g, idx, val, pt: (b, 0, 0))
    new_rows = kv_sel_new.reshape(nb, KV_SLABS, d)
    grid_spec = pltpu.PrefetchScalarGridSpec(
        num_scalar_prefetch=3,
        grid=(nb, N_KV_HEADS),
        in_specs=[per_b(q), per_b(new_rows), per_b(gates), per_b(o_cmp), per_b(o_win)]
        + [blk_spec(k) for k in range(N_SEL)],
        out_specs=pl.BlockSpec((None, n_h, d), lambda b, g, idx, val, pt: (b, 0, 0)),
    )
    return pl.pallas_call(
        functools.partial(_attn_sample_b_kernel, scale=scale, q_pos=q_pos, past_len=past_len),
        out_shape=jax.ShapeDtypeStruct((nb, n_h, d), F32),
        grid_spec=grid_spec,
        compiler_params=_cparams(2, 32),
        name="attn_sample_b",
    )(sel_idx, sel_val, page_table, q, new_rows, gates, o_cmp, o_win, *([pool] * N_SEL))


def kernel(x_prompt, x_sample, cache_cmp_kv, cache_sel_kv, cache_win_kv, page_table, c_prompt, c_sample, w_ada, b_ada, ln_g, ln_b, a_w_in, a_v_g, a_v_b, a_w_s, a_b_s, a_w_out, w_kv, cmp_pe, cmp_w1, cmp_b1, cmp_w2, cmp_b2, b_w_qg, b_w_o, ffn_w_in, ffn_w_out):
    nb, t, d = x_prompt.shape
    ns, tq, _ = x_sample.shape
    assert tq == 1, "the sample path handles one new token per sequence"
    depth = w_ada.shape[0]
    n_a = a_w_in.shape[0]
    alpha = float((2 * depth) ** 0.25)
    scale = float(HEAD_DIM ** -0.5)
    n_pool = cache_cmp_kv.shape[0]
    past_len = page_table.shape[1] * PAGE_SIZE
    kvw = N_KV_HEADS * 2 * HEAD_DIM
    q_width = b_w_o.shape[1]
    n_heads = q_width // HEAD_DIM
    rq = n_heads // N_KV_HEADS

    mod = _ada_modulation(jnp.concatenate([c_prompt, c_sample], axis=0), w_ada, b_ada)

    def mods(layer, sub):
        m = mod[layer * 2 + sub]
        out = []
        for part in range(3):
            sl = m[:, part * d:(part + 1) * d]
            out.append((sl[:nb].reshape(nb, 1, d), sl[nb:].reshape(1, ns, d)))
        return out

    wb = lambda w: w.astype(BF)
    hidden = cmp_w1.shape[-1]
    halves = CMP_LEN // CMP_STRIDE
    assert halves == 2, "a compression block spans two stride-chunks"
    w1all = wb(cmp_w1.reshape(2, halves, CMP_STRIDE, HEAD_DIM, hidden)
               .transpose(2, 3, 0, 1, 4).reshape(CMP_STRIDE * HEAD_DIM, 2 * halves * hidden))
    pe16 = jnp.pad(cmp_pe.reshape(2 * halves, CMP_STRIDE * HEAD_DIM), ((0, 16 - 2 * halves), (0, 0)))
    w2all = wb(cmp_w2.transpose(1, 0, 2).reshape(hidden, 2 * HEAD_DIM))

    a_w_out_b, b_w_o_b, ffn_w_in_b, ffn_w_out_b = wb(a_w_out), wb(b_w_o), wb(ffn_w_in), wb(ffn_w_out)

    tm_p = 512
    tpg_p = t // tm_p
    prompt = dict(tm=tm_p, tiles_per_group=tpg_p)
    sample = dict(tm=ns, tiles_per_group=1)

    xp = x_prompt.reshape(nb * t, d)
    xs = x_sample.reshape(ns, d)
    hp = hs = None
    chunk_v = []
    outs = {}
    for layer in range(depth):
        (sh_p, sh_s), (sc_p, sc_s), (gt_p, gt_s) = mods(layer, 0)
        (fsh_p, fsh_s), (fsc_p, fsc_s), (fgt_p, fgt_s) = mods(layer, 1)
        if layer < n_a:
            w_in = wb(a_w_in[layer])
            ap = _gmlp_in(xp, sc_p, sh_p, w_in, a_v_g[layer], a_v_b[layer], a_w_s[layer], a_b_s[layer],
                          chunked=True, **prompt)
            as_, v_rows = _gmlp_in(xs, sc_s, sh_s, w_in, a_v_g[layer], a_v_b[layer], a_w_s[layer], a_b_s[layer],
                                   chunked=False, **sample)
            chunk_v.append(v_rows.reshape(ns, tq, -1))
            xp, hp = _proj_res_ln(ap, a_w_out_b, xp, gt_p, ln_g[layer, 0], ln_b[layer, 0], (fsc_p, fsh_p),
                                  alpha=alpha, layer=layer, **prompt)
            xs, hs = _proj_res_ln(as_, a_w_out_b, xs, gt_s, ln_g[layer, 0], ln_b[layer, 0], (fsc_s, fsh_s),
                                  alpha=alpha, layer=layer, **sample)
        else:
            if layer == n_a:
                w_kvb = wb(w_kv)
                cmp_p, sel_p, win_p, sel_pb, win_pb = _kv_proj(xp, w_kvb, tm=tm_p)
                cmp_s, sel_s, win_s, _, _ = _kv_proj(xs, w_kvb, tm=ns)
                pages_per_seq_p = t // PAGE_SIZE
                ident = jnp.arange(nb * pages_per_seq_p, dtype=jnp.int32).reshape(nb, pages_per_seq_p)
                cmp_blocks_p = _compress(cmp_p, ident, w1all, pe16, cmp_b1, w2all, cmp_b2)
                assert (past_len + tq) // CMP_STRIDE == past_len // CMP_STRIDE
                cmp_blocks_s = _compress(cache_cmp_kv.reshape(n_pool * PAGE_FLAT_ROWS, HEAD_DIM), page_table,
                                         w1all, pe16, cmp_b1, w2all, cmp_b2)
                n_win = cache_win_kv.shape[1]
                win_keys_s = jnp.concatenate([cache_win_kv.reshape(ns, n_win * KV_SLABS, HEAD_DIM),
                                              win_s.reshape(ns, tq * KV_SLABS, HEAD_DIM)],
                                             axis=1)[:, -n_win * KV_SLABS:]
                outs["kv_p"] = (cmp_p, sel_p, win_p)
                outs["kv_pb"] = (sel_pb.reshape(nb, t, kvw), win_pb.reshape(nb, t, kvw))
                outs["kv_s"] = (cmp_s, sel_s, win_keys_s)
            j = layer - n_a
            w_qg = b_w_qg[j]
            w_q = wb(w_qg[:, :q_width])
            n_gates = w_qg.shape[1] - q_width
            w_gate = wb(jnp.pad(w_qg[:, q_width:], ((0, 0), (0, LANES - n_gates))))
            assert layer > 0, "an attention layer must follow another layer"
            q_p, g_p = _qg_proj(hp, w_q, w_gate, tm=tm_p, gates_per_group=rq * 3)
            q_s, g_s = _qg_proj(hs, w_q, w_gate, tm=ns, gates_per_group=rq * 3)
            o_p = _attn_prompt(q_p.reshape(nb, t, q_width), g_p.reshape(nb, t, N_KV_HEADS * LANES),
                               cmp_blocks_p, outs["kv_pb"][0], outs["kv_pb"][1], scale=scale)
            n_cmp_s = (past_len + tq) // CMP_STRIDE - CMP_LEN // CMP_STRIDE + 1
            n_slc_s = -(-(past_len + tq) // SEL_BLOCK)
            q_s3 = q_s.reshape(ns, n_heads, HEAD_DIM)
            o_cmp_s, o_win_s, idx_s, val_s = _attn_sample_a(
                q_s3, cmp_blocks_s, outs["kv_s"][2], scale=scale, q_pos=past_len, n_cmp=n_cmp_s, n_slc=n_slc_s,
                win_pos0=past_len + tq - outs["kv_s"][2].shape[1] // KV_SLABS)
            sel_idx = idx_s[:, ::rq, :N_SEL].reshape(-1)
            sel_val = val_s[:, ::rq, :N_SEL].reshape(-1)
            gates_s = g_s.reshape(ns, N_KV_HEADS, LANES)[:, :, :rq * 3].reshape(ns, n_heads, 3)
            o_s = _attn_sample_b(sel_idx, sel_val, page_table, q_s3, outs["kv_s"][1], gates_s, o_cmp_s, o_win_s,
                                 cache_sel_kv.reshape(n_pool * PAGE_FLAT_ROWS, HEAD_DIM), scale=scale,
                                 q_pos=past_len, past_len=past_len)
            xp, hp = _proj_res_ln(o_p.reshape(nb * t, q_width), b_w_o_b, xp, gt_p, ln_g[layer, 0], ln_b[layer, 0],
                                  (fsc_p, fsh_p), alpha=alpha, layer=j, **prompt)
            xs, hs = _proj_res_ln(o_s.reshape(ns, q_width), b_w_o_b, xs, gt_s, ln_g[layer, 0], ln_b[layer, 0],
                                  (fsc_s, fsh_s), alpha=alpha, layer=j, **sample)
        fp = _ffn_in(hp, ffn_w_in_b, layer, tm=2 * tm_p)
        fs = _ffn_in(hs, ffn_w_in_b, layer, tm=ns)
        nxt_p = nxt_s = None
        if n_a <= layer + 1 < depth:
            (nsh_p, nsh_s), (nsc_p, nsc_s), _ = mods(layer + 1, 0)
            nxt_p, nxt_s = (nsc_p, nsh_p), (nsc_s, nsh_s)
        tm_f = tm_p // 2
        xp, hp = _proj_res_ln(fp, ffn_w_out_b, xp, fgt_p, ln_g[layer, 1], ln_b[layer, 1], nxt_p, alpha=alpha,
                              layer=layer, tm=tm_f, tiles_per_group=t // tm_f)
        xs, hs = _proj_res_ln(fs, ffn_w_out_b, xs, fgt_s, ln_g[layer, 1], ln_b[layer, 1], nxt_s, alpha=alpha,
                              layer=layer, **sample)

    cmp_p, sel_p, win_p = outs["kv_p"]
    cmp_s, sel_s, win_keys_s = outs["kv_s"]
    kv_shape = (N_KV_HEADS, 2, HEAD_DIM)
    n_win_p = min(WINDOW, t)
    return (xp.reshape(nb, t, d), xs.reshape(ns, tq, d),
            cmp_p.reshape((nb, t) + kv_shape), sel_p.reshape((nb, t) + kv_shape),
            win_p.reshape((nb, t) + kv_shape)[:, -n_win_p:],
            cmp_s.reshape((ns, tq) + kv_shape), sel_s.reshape((ns, tq) + kv_shape),
            win_keys_s.reshape((ns, -1) + kv_shape),
            jnp.stack(chunk_v))
```

```python
import functools

import numpy as np
import jax
import jax.numpy as jnp
from jax import lax
from jax.experimental import pallas as pl
from jax.experimental.pallas import tpu as pltpu

BF = jnp.bfloat16
F32 = jnp.float32

LN_EPS = 1e-5
CHUNK = 128
HEAD_DIM = 128
N_KV_HEADS = 4
CMP_LEN = 32
CMP_STRIDE = 16
SEL_BLOCK = 64
N_SEL = 16
N_LOCAL_SEL = 2
FORCE_BONUS = 1e4
LOG2_E = 1.4426950408889634
WINDOW = 512
Q_BLOCK = 256
PAGE_SIZE = 128

LANES = 128
MIB = 1024 * 1024


def _cparams(n_axes, vmem_mib):
    return pltpu.CompilerParams(dimension_semantics=("arbitrary",) * n_axes,
                                vmem_limit_bytes=vmem_mib * MIB)


def _dot(a, b):
    return jnp.dot(a, b, preferred_element_type=F32)


def _dot_t(a, b):
    return lax.dot_general(a, b, (((1,), (1,)), ((), ())), preferred_element_type=F32)


def _layer_norm(x, g, b):
    mu = jnp.mean(x, axis=-1, keepdims=True)
    xc = x - mu
    var = jnp.mean(xc * xc, axis=-1, keepdims=True)
    return xc * lax.rsqrt(var + LN_EPS) * g + b


def _masked_softmax(s, valid):
    s = jnp.where(valid, s, -jnp.inf)
    m = jnp.max(s, axis=-1, keepdims=True)
    m = jnp.where(m == -jnp.inf, 0.0, m)
    e = jnp.exp(s - m)
    d = jnp.maximum(jnp.sum(e, axis=-1, keepdims=True), 1e-30)
    return e * (1.0 / d)


def _log2(n):
    assert n & (n - 1) == 0, "power of two expected"
    return n.bit_length() - 1


def _div_pow2(x, n):
    return lax.shift_right_logical(x, jnp.int32(_log2(n)))


def _mod_pow2(x, n):
    assert n & (n - 1) == 0, "power of two expected"
    return lax.bitwise_and(x, jnp.int32(n - 1))


def _rank_desc(imp, n):
    j_io = lax.broadcasted_iota(jnp.int32, (1, imp.shape[1]), 1)
    rank = jnp.zeros(imp.shape, F32)
    for jp in range(n):
        col = imp[:, jp:jp + 1]
        before = jnp.where(col == imp, jnp.where(jp < j_io, 1.0, 0.0), jnp.where(col > imp, 1.0, 0.0))
        rank = rank + before
    return rank


def _ada_kernel(c_ref, w_ref, b_ref, o_ref):
    a = jax.nn.silu(c_ref[...]).astype(BF)
    o_ref[...] = _dot(a, w_ref[...].astype(BF)) + b_ref[...]


def _ada_modulation(c_all, w_ada, b_ada):
    n_sub = w_ada.shape[0] * w_ada.shape[1]
    d, n3 = w_ada.shape[2], w_ada.shape[3]
    m = c_all.shape[0]
    tn = 512
    w = w_ada.reshape(n_sub, d, n3)
    b = b_ada.reshape(n_sub, 1, n3)
    return pl.pallas_call(
        _ada_kernel,
        out_shape=jax.ShapeDtypeStruct((n_sub, m, n3), F32),
        grid=(n_sub, n3 // tn),
        in_specs=[pl.BlockSpec((m, d), lambda s, j: (0, 0)),
                  pl.BlockSpec((None, d, tn), lambda s, j: (s, 0, j)),
                  pl.BlockSpec((None, 1, tn), lambda s, j: (s, 0, j))],
        out_specs=pl.BlockSpec((None, m, tn), lambda s, j: (s, 0, j)),
        compiler_params=_cparams(2, 32),
        name="ada_modulation",
    )(c_all, w, b)


def _mod_spec(mod, tiles_per_group):
    rb, d = mod.shape[1], mod.shape[2]
    return pl.BlockSpec((None, rb, d), lambda i: (i // tiles_per_group, 0, 0))


def _gmlp_in_kernel(x_ref, sc_ref, sh_ref, w_ref, vg_ref, vb_ref, ws_ref, bs_ref, *rest,
                    n_j, tn, chunked):
    if chunked:
        o_ref, z_scr = rest
    else:
        o_ref, v_ref, z_scr = rest
    h = (x_ref[...] * (1.0 + sc_ref[...]) + sh_ref[...]).astype(BF)
    for jj in range(n_j):
        z_scr[jj] = jax.nn.gelu(_dot(h, w_ref[:, jj * tn:(jj + 1) * tn]))

    tm = z_scr.shape[1]
    half = n_j // 2
    e_a = half * tn
    s1 = jnp.zeros((tm, 1), F32)
    for jj in range(half):
        s1 = s1 + jnp.sum(z_scr[half + jj], axis=-1, keepdims=True)
    mu = s1 / e_a
    s2 = jnp.zeros((tm, 1), F32)
    for jj in range(half):
        vc = z_scr[half + jj] - mu
        s2 = s2 + jnp.sum(vc * vc, axis=-1, keepdims=True)
    rstd = lax.rsqrt(s2 / e_a + LN_EPS)
    gpt = tn // CHUNK
    if chunked:
        row = lax.broadcasted_iota(jnp.int32, (CHUNK, CHUNK), 0)
        col = lax.broadcasted_iota(jnp.int32, (CHUNK, CHUNK), 1)
        for jj in range(half):
            for gi in range(gpt):
                g = jj * gpt + gi
                cs = slice(gi * CHUNK, (gi + 1) * CHUNK)
                wm = jnp.where(row >= col, ws_ref[g], 0.0).astype(BF)
                bcol = bs_ref[:, g:g + 1]
                vg = vg_ref[:, g * CHUNK:(g + 1) * CHUNK]
                vb = vb_ref[:, g * CHUNK:(g + 1) * CHUNK]
                for c in range(tm // CHUNK):
                    rs = slice(c * CHUNK, (c + 1) * CHUNK)
                    vn = (z_scr[half + jj, rs, cs] - mu[rs]) * rstd[rs] * vg + vb
                    mixed = _dot(wm, vn.astype(BF)) + bcol
                    o_ref[rs, g * CHUNK:(g + 1) * CHUNK] = (z_scr[jj, rs, cs] * mixed).astype(BF)
    else:
        for jj in range(half):
            cs = slice(jj * tn, (jj + 1) * tn)
            vn = (z_scr[half + jj] - mu) * rstd * vg_ref[:, cs] + vb_ref[:, cs]
            v_ref[:, cs] = vn
            mixed = ws_ref[:, cs] * vn + bs_ref[:, cs]
            o_ref[:, cs] = (z_scr[jj] * mixed).astype(BF)


def _gmlp_in(x, sc, sh, w_in, v_g, v_b, w_s, b_s, *, tm, tiles_per_group, chunked):
    r, d = x.shape
    n2 = w_in.shape[1]
    e_a = n2 // 2
    tn = 512
    n_j = n2 // tn
    n_g = w_s.shape[0]
    full = lambda a: pl.BlockSpec(a.shape, lambda i: (0,) * a.ndim)
    row_tile = pl.BlockSpec((tm, e_a), lambda i: (i, 0))
    if chunked:
        ws_arr = w_s
        bs_arr = b_s.T
        out_shape = jax.ShapeDtypeStruct((r, e_a), BF)
        out_specs = row_tile
    else:
        ws_arr = jnp.repeat(w_s[:, 0, 0], e_a // n_g).reshape(1, e_a)
        bs_arr = jnp.repeat(b_s[:, 0], e_a // n_g).reshape(1, e_a)
        out_shape = (jax.ShapeDtypeStruct((r, e_a), BF), jax.ShapeDtypeStruct((r, e_a), F32))
        out_specs = (row_tile, row_tile)
    vg2, vb2 = v_g.reshape(1, e_a), v_b.reshape(1, e_a)
    return pl.pallas_call(
        functools.partial(_gmlp_in_kernel, n_j=n_j, tn=tn, chunked=chunked),
        out_shape=out_shape,
        grid=(r // tm,),
        in_specs=[pl.BlockSpec((tm, d), lambda i: (i, 0)),
                  _mod_spec(sc, tiles_per_group), _mod_spec(sh, tiles_per_group),
                  pl.BlockSpec(w_in.shape, lambda i: (0, 0), pipeline_mode=pl.Buffered(1)),
                  full(vg2), full(vb2), full(ws_arr), full(bs_arr)],
        out_specs=out_specs,
        scratch_shapes=[pltpu.VMEM((n_j, tm, tn), F32)],
        compiler_params=_cparams(1, 48),
        name="gmlp_in_chunked" if chunked else "gmlp_in_single",
    )(x, sc, sh, w_in, vg2, vb2, ws_arr, bs_arr)


LN_ROWS = 8


def _proj_res_ln_kernel(a_ref, w_ref, x_ref, gate_ref, lg_ref, lb_ref, *rest, alpha, emit_next):
    if emit_next:
        sc_ref, sh_ref, y_ref, h_ref, acc = rest
    else:
        y_ref, acc = rest
    acc[...] = _dot(a_ref[...].astype(BF), w_ref[...])
    tm = acc.shape[0]
    per_row_mod = gate_ref.shape[0] > 1
    step = min(LN_ROWS, tm)
    assert tm % step == 0
    for c in range(tm // step):
        rs = slice(c * step, (c + 1) * step)
        ms = rs if per_row_mod else slice(None)
        t = alpha * x_ref[rs, :] + (1.0 + gate_ref[ms, :]) * acc[rs, :]
        y = _layer_norm(t, lg_ref[...], lb_ref[...])
        y_ref[rs, :] = y
        if emit_next:
            h_ref[rs, :] = (y * (1.0 + sc_ref[ms, :]) + sh_ref[ms, :]).astype(BF)


def _proj_res_ln(a, w, x, gate, ln_g, ln_b, nxt, *, alpha, tm, tiles_per_group, layer=0):
    r, kdim = a.shape
    d = w.shape[2]
    emit_next = nxt is not None
    mod_spec = lambda m: _mod_spec(m, tiles_per_group)
    in_specs = [pl.BlockSpec((tm, kdim), lambda i: (i, 0)),
                pl.BlockSpec((None, kdim, d), lambda i: (layer, 0, 0), pipeline_mode=pl.Buffered(1)),
                pl.BlockSpec((tm, d), lambda i: (i, 0)),
                mod_spec(gate),
                pl.BlockSpec((1, d), lambda i: (0, 0)),
                pl.BlockSpec((1, d), lambda i: (0, 0))]
    args = [a, w, x, gate, ln_g.reshape(1, d), ln_b.reshape(1, d)]
    out_shape = [jax.ShapeDtypeStruct((r, d), F32)]
    out_specs = [pl.BlockSpec((tm, d), lambda i: (i, 0))]
    if emit_next:
        in_specs += [mod_spec(nxt[0]), mod_spec(nxt[1])]
        args += [nxt[0], nxt[1]]
        out_shape.append(jax.ShapeDtypeStruct((r, d), BF))
        out_specs.append(pl.BlockSpec((tm, d), lambda i: (i, 0)))
    res = pl.pallas_call(
        functools.partial(_proj_res_ln_kernel, alpha=alpha, emit_next=emit_next),
        out_shape=tuple(out_shape),
        grid=(r // tm,),
        in_specs=in_specs,
        out_specs=tuple(out_specs),
        scratch_shapes=[pltpu.VMEM((tm, d), F32)],
        compiler_params=_cparams(1, 56),
        name="proj_res_ln",
    )(*args)
    return res if emit_next else (res[0], None)


def _ffn_in_kernel(h_ref, wg_ref, wu_ref, o_ref):
    h = h_ref[...]
    o_ref[...] = (jax.nn.silu(_dot(h, wg_ref[...])) * _dot(h, wu_ref[...])).astype(BF)


def _ffn_in(h, w_in, layer, *, tm):
    r, d = h.shape
    d_ff = w_in.shape[2] // 2
    tn = 512
    n_j = d_ff // tn
    return pl.pallas_call(
        _ffn_in_kernel,
        out_shape=jax.ShapeDtypeStruct((r, d_ff), BF),
        grid=(r // tm, n_j),
        in_specs=[pl.BlockSpec((tm, d), lambda i, j: (i, 0)),
                  pl.BlockSpec((None, d, tn), lambda i, j: (layer, 0, j)),
                  pl.BlockSpec((None, d, tn), lambda i, j: (layer, 0, j + n_j))],
        out_specs=pl.BlockSpec((tm, tn), lambda i, j: (i, j)),
        compiler_params=_cparams(2, 40),
        name="ffn_in",
    )(h, w_in, w_in)


KV_SLABS = N_KV_HEADS * 2


def _kv_proj_kernel(y_ref, w_ref, f0_ref, f1_ref, f2_ref, b1_ref, b2_ref):
    xb = y_ref[...].astype(BF)
    tm = xb.shape[0]
    n = w_ref.shape[1] // 3
    for idx, (f_ref, b_ref) in enumerate(((f0_ref, None), (f1_ref, b1_ref), (f2_ref, b2_ref))):
        r = _dot(xb, w_ref[:, idx * n:(idx + 1) * n])
        for hk in range(KV_SLABS):
            f_ref[pl.ds(hk, tm, stride=KV_SLABS), :] = r[:, hk * HEAD_DIM:(hk + 1) * HEAD_DIM]
        if b_ref is not None:
            b_ref[...] = r.astype(BF)


def _kv_proj(y, w_kv, *, tm):
    r, d = y.shape
    n = w_kv.shape[1] // 3
    flat = jax.ShapeDtypeStruct((r * KV_SLABS, HEAD_DIM), F32)
    fspec = pl.BlockSpec((tm * KV_SLABS, HEAD_DIM), lambda i: (i, 0))
    wide = jax.ShapeDtypeStruct((r, n), BF)
    wspec = pl.BlockSpec((tm, n), lambda i: (i, 0))
    return pl.pallas_call(
        _kv_proj_kernel,
        out_shape=(flat, flat, flat, wide, wide),
        grid=(r // tm,),
        in_specs=[pl.BlockSpec((tm, d), lambda i: (i, 0)),
                  pl.BlockSpec(w_kv.shape, lambda i: (0, 0), pipeline_mode=pl.Buffered(1))],
        out_specs=(fspec, fspec, fspec, wspec, wspec),
        compiler_params=_cparams(1, 48),
        name="kv_proj",
    )(y, w_kv)


def _qg_proj_kernel(h_ref, wq_ref, wg_ref, q_ref, g_ref, *, gates_per_group):
    h = h_ref[...]
    q_ref[...] = _dot(h, wq_ref[...]).astype(BF)
    gates = jax.nn.sigmoid(_dot(h, wg_ref[...]))
    for grp in range(g_ref.shape[1] // LANES):
        shift = (LANES - grp * gates_per_group) % LANES
        g_ref[:, grp * LANES:(grp + 1) * LANES] = gates if shift == 0 else pltpu.roll(gates, shift, 1)


def _qg_proj(h, w_q, w_g, *, tm, gates_per_group):
    r, d = h.shape
    qw = w_q.shape[1]
    gw = N_KV_HEADS * LANES
    resident = lambda w: pl.BlockSpec(w.shape, lambda i: (0, 0), pipeline_mode=pl.Buffered(1))
    return pl.pallas_call(
        functools.partial(_qg_proj_kernel, gates_per_group=gates_per_group),
        out_shape=(jax.ShapeDtypeStruct((r, qw), BF), jax.ShapeDtypeStruct((r, gw), F32)),
        grid=(r // tm,),
        in_specs=[pl.BlockSpec((tm, d), lambda i: (i, 0)), resident(w_q), resident(w_g)],
        out_specs=(pl.BlockSpec((tm, qw), lambda i: (i, 0)),
                   pl.BlockSpec((tm, gw), lambda i: (i, 0))),
        compiler_params=_cparams(1, 40),
        name="qg_proj",
    )(h, w_q, w_g)


PAGES_PER_STEP = 16
CHUNKS_PER_PAGE = PAGE_SIZE // CMP_STRIDE


PAGE_FLAT_ROWS = PAGE_SIZE * KV_SLABS
CHUNK_FLAT_ROWS = CMP_STRIDE * KV_SLABS


def _compress_kernel(pt_ref, *refs, n_steps):
    page_refs = refs[:PAGES_PER_STEP]
    w1_ref, pe_ref, b1_ref, w2_ref, b2_ref, o_ref, p0_scr, p1_scr = refs[PAGES_PER_STEP:]
    t = pl.program_id(1)
    rows_step = PAGES_PER_STEP * CHUNKS_PER_PAGE * N_KV_HEADS
    n_rows = n_steps * rows_step
    hidden = w2_ref.shape[0]

    @pl.when(t == 0)
    def _():
        p1_scr[:, pl.ds(n_rows, 8), :] = jnp.zeros((2, 8, hidden), F32)

    row0 = pl.multiple_of(t * rows_step, rows_step)
    for k in range(2):
        per_s = []
        for s in range(CMP_STRIDE):
            rows = [page_refs[p][pl.ds(c * CHUNK_FLAT_ROWS + s * KV_SLABS + k, N_KV_HEADS, stride=2), :]
                    for p in range(PAGES_PER_STEP) for c in range(CHUNKS_PER_PAGE)]
            per_s.append(jnp.concatenate(rows, axis=0))
        x = jnp.concatenate(per_s, axis=1).astype(BF)
        pr = _dot(x, w1_ref[:, k * 2 * hidden:(k + 1) * 2 * hidden])
        p0_scr[k, pl.ds(row0, rows_step), :] = pr[:, :hidden]
        p1_scr[k, pl.ds(row0, rows_step), :] = pr[:, hidden:]

    @pl.when(t == n_steps - 1)
    def _():
        cst = _dot(pe_ref[...].astype(BF), w1_ref[...])
        for k in range(2):
            c0 = 2 * k * hidden
            const = (cst[2 * k:2 * k + 1, c0:c0 + hidden] + cst[2 * k + 1:2 * k + 2, c0 + hidden:c0 + 2 * hidden]
                     + b1_ref[k:k + 1, :])
            pre = p0_scr[k, 0:n_rows, :] + p1_scr[k, N_KV_HEADS:n_rows + N_KV_HEADS, :] + const
            res = _dot(jax.nn.gelu(pre).astype(BF), w2_ref[:, k * HEAD_DIM:(k + 1) * HEAD_DIM]) + b2_ref[k:k + 1, :]
            o_ref[pl.ds(k, n_rows, stride=2), :] = res


def _compress(pages_flat, page_table, w1all, pe16, b1, w2all, b2):
    nb, n_pages = page_table.shape
    n_steps = n_pages // PAGES_PER_STEP
    n_rows = n_pages * CHUNKS_PER_PAGE * KV_SLABS
    hidden = w2all.shape[0]

    def page_spec(p):
        return pl.BlockSpec((PAGE_FLAT_ROWS, HEAD_DIM), lambda b, t, pt: (pt[b, t * PAGES_PER_STEP + p], 0))

    full = lambda a: pl.BlockSpec(a.shape, lambda b, t, pt: (0,) * a.ndim)
    grid_spec = pltpu.PrefetchScalarGridSpec(
        num_scalar_prefetch=1,
        grid=(nb, n_steps),
        in_specs=[page_spec(p) for p in range(PAGES_PER_STEP)] + [
            full(w1all), full(pe16), full(b1), full(w2all), full(b2)],
        out_specs=pl.BlockSpec((None, n_rows, HEAD_DIM), lambda b, t, pt: (b, 0, 0)),
        scratch_shapes=[pltpu.VMEM((2, n_rows // 2, hidden), F32), pltpu.VMEM((2, n_rows // 2 + 8, hidden), F32)],
    )
    return pl.pallas_call(
        functools.partial(_compress_kernel, n_steps=n_steps),
        out_shape=jax.ShapeDtypeStruct((nb, n_rows, HEAD_DIM), F32),
        grid_spec=grid_spec,
        compiler_params=_cparams(2, 48),
        name="compress_blocks",
    )(page_table, *([pages_flat] * PAGES_PER_STEP), w1all, pe16, b1, w2all, b2)


def _overlap_map(n_cmp_rows, n_slc, n_cols):
    cmp_lo = np.arange(n_cmp_rows)[:, None] * CMP_STRIDE
    slc_lo = np.arange(n_cols)[None, :] * SEL_BLOCK
    ov = np.minimum(cmp_lo + CMP_LEN, slc_lo + SEL_BLOCK) - np.maximum(cmp_lo, slc_lo)
    w = np.clip(ov, 0, None).astype(np.float32) / CMP_LEN
    w[:, n_slc:] = 0.0
    return w


SEL_CHUNK = 512


def _attn_prompt_kernel(q_ref, gt_ref, cmp_ref, sel_ref, win_ref, wmap_t_ref, exp_ref, o_ref,
                        s_sel, m_run, e_scr, s_win, m_win, e_win, *, scale, n_slc, n_qb, n_widths):
    g = pl.program_id(1)
    qb = pl.program_id(2)
    q4 = q_ref[...]
    nq = q4.shape[0]
    rq = q4.shape[1] // HEAD_DIM
    qs = jnp.concatenate([q4[:, r * HEAD_DIM:(r + 1) * HEAD_DIM] for r in range(rq)], axis=0)
    qpos = qb * nq + lax.broadcasted_iota(jnp.int32, (nq, 1), 0)
    rows = [slice(r * nq, (r + 1) * nq) for r in range(rq)]

    head_groups = [rows[:rq // 2], rows[rq // 2:]] if rq % 2 == 0 else [rows]

    def attend(n_chunks, chunk, s_scr, m_run, e_scr, keys, mask, values):
        width = n_chunks * chunk
        values_and_ones = jnp.concatenate([values, jnp.ones(values.shape, BF)], axis=1)
        m_run[...] = jnp.full(m_run.shape, -jnp.inf, F32)
        for grp in head_groups:
            g_rows = slice(grp[0].start, grp[-1].stop)
            for c in range(n_chunks):
                s = _dot_t(qs[g_rows], keys(c))
                ok = mask(c)
                for i, rs in enumerate(grp):
                    sm = jnp.where(ok, s[i * nq:(i + 1) * nq], -jnp.inf) * (scale * LOG2_E)
                    s_scr[c, rs, :] = sm
                    mx = m_run[rs, :]
                    for j in range(chunk // LANES):
                        mx = jnp.maximum(mx, sm[:, j * LANES:(j + 1) * LANES])
                    m_run[rs, :] = mx
        outs = []
        for grp in head_groups:
            g_rows = slice(grp[0].start, grp[-1].stop)
            m = jnp.max(m_run[g_rows, :], axis=-1, keepdims=True)
            m = jnp.where(m == -jnp.inf, 0.0, m)
            for c in range(n_chunks):
                for i, rs in enumerate(grp):
                    e = jnp.exp2(s_scr[c, rs, :] - m[i * nq:(i + 1) * nq])
                    e_scr[rs, c * chunk:(c + 1) * chunk] = e.astype(BF)
            pv = _dot(e_scr[g_rows, :width], values_and_ones)
            outs.append(pv[:, :HEAD_DIM] * (1.0 / jnp.maximum(pv[:, HEAD_DIM:], 1e-30)))
        return jnp.concatenate(outs, axis=0)

    def step(n_sel_chunks, first_q):
        span = WINDOW + nq
        k0 = pl.multiple_of(jnp.maximum(qb * nq - WINDOW, 0), nq)
        kp = k0 + lax.broadcasted_iota(jnp.int32, (1, span), 1)
        valid_w = lax.bitcast_convert_type(qpos - kp, jnp.uint32) < jnp.uint32(WINDOW)
        o_win = attend(1, span, s_win, m_win, e_win, lambda c: win_ref[pl.ds(k0, span), :HEAD_DIM],
                       lambda c: valid_w, win_ref[pl.ds(k0, span), HEAD_DIM:])

        n_cmp_rows = cmp_ref.shape[0] // KV_SLABS
        kc = cmp_ref[pl.ds(g * 2, n_cmp_rows, stride=KV_SLABS), :].astype(BF)
        vc = cmp_ref[pl.ds(g * 2 + 1, n_cmp_rows, stride=KV_SLABS), :].astype(BF)
        n_io = lax.broadcasted_iota(jnp.int32, (1, n_cmp_rows), 1)
        valid_c = (n_io * CMP_STRIDE + (CMP_LEN - 1)) <= qpos
        s_c = _dot_t(qs, kc) * scale
        ps = [_masked_softmax(s_c[rs], valid_c).astype(BF) for rs in rows]
        o_cmp = _dot(jnp.concatenate(ps, axis=0), vc)

        imp = _dot_t(wmap_t_ref[...], jnp.concatenate(ps, axis=1))
        n_rows = imp.shape[0]
        j_io = lax.broadcasted_iota(jnp.int32, (n_rows, 1), 0)
        q_blk = _div_pow2(qb * nq + lax.broadcasted_iota(jnp.int32, (1, nq), 1), SEL_BLOCK)
        causal = j_io <= q_blk
        forced = (j_io == 0) | (causal & (j_io > q_blk - N_LOCAL_SEL))
        imp = jnp.where(forced, imp + FORCE_BONUS, imp)
        imp = jnp.where(causal, imp, -jnp.inf)
        rank = jnp.zeros(imp.shape, F32)
        for jp in range(n_slc):
            other = imp[jp:jp + 1, :]
            rank = rank + jnp.where(other == imp, jnp.where(jp < j_io, 1.0, 0.0),
                                    jnp.where(other > imp, 1.0, 0.0))
        chosen_t = jnp.where(rank < min(N_SEL, n_slc), jnp.where(imp > -jnp.inf, 1.0, 0.0), 0.0)
        chosen_t = jnp.concatenate([chosen_t, jnp.zeros((LANES - n_rows, nq), F32)], axis=0)
        chosen = chosen_t.T.astype(BF)

        def sel_mask(c):
            in_sel = _dot(chosen, exp_ref[c])
            if (c + 1) * SEL_CHUNK <= first_q:
                return in_sel > 0.5
            t_io = c * SEL_CHUNK + lax.broadcasted_iota(jnp.int32, (1, SEL_CHUNK), 1)
            return jnp.where(t_io <= qpos, in_sel, 0.0) > 0.5

        o_sel = attend(n_sel_chunks, SEL_CHUNK, s_sel, m_run, e_scr,
                       lambda c: sel_ref[c * SEL_CHUNK:(c + 1) * SEL_CHUNK, :HEAD_DIM], sel_mask,
                       sel_ref[:n_sel_chunks * SEL_CHUNK, HEAD_DIM:])

        gt = gt_ref[...]
        outs = []
        for r, rs in enumerate(rows):
            outs.append(gt[:, 3 * r:3 * r + 1] * o_cmp[rs] + gt[:, 3 * r + 1:3 * r + 2] * o_sel[rs]
                        + gt[:, 3 * r + 2:3 * r + 3] * o_win[rs])
        o_ref[...] = jnp.concatenate(outs, axis=1).astype(BF)

    qb_per_width = n_qb // n_widths
    for wi in range(n_widths):
        n_sel_chunks = -(-(wi + 1) * qb_per_width * nq // SEL_CHUNK)
        pl.when((qb >= wi * qb_per_width) & (qb < (wi + 1) * qb_per_width))(
            functools.partial(step, n_sel_chunks, wi * qb_per_width * nq))


def _attn_prompt(q, gates, cmp_flat, kv_sel, kv_win, *, scale):
    nb, t, qw = q.shape
    n_g = N_KV_HEADS
    gw = qw // n_g
    kvw = 2 * HEAD_DIM
    cmp_rows = cmp_flat.shape[1]
    n_slc = t // SEL_BLOCK
    n_slc_rows = -(-n_slc // 8) * 8
    rq = gw // HEAD_DIM
    n_qb = t // Q_BLOCK
    n_chunks = t // SEL_CHUNK
    wmap_t = jnp.asarray(np.tile(_overlap_map(cmp_rows // KV_SLABS, n_slc, n_slc_rows), (rq, 1)).T, BF)
    key_blk = (np.arange(t) // SEL_BLOCK).reshape(n_chunks, 1, SEL_CHUNK)
    expand = jnp.asarray((key_blk == np.arange(LANES)[None, :, None]).astype(np.float32), BF)
    rows = rq * Q_BLOCK
    n_widths = 4 if n_qb % 4 == 0 else 1
    return pl.pallas_call(
        functools.partial(_attn_prompt_kernel, scale=scale, n_slc=n_slc, n_qb=n_qb, n_widths=n_widths),
        out_shape=jax.ShapeDtypeStruct((nb, t, qw), BF),
        grid=(nb, n_g, n_qb),
        in_specs=[pl.BlockSpec((None, Q_BLOCK, gw), lambda b, g, i: (b, i, g)),
                  pl.BlockSpec((None, Q_BLOCK, LANES), lambda b, g, i: (b, i, g)),
                  pl.BlockSpec((None, cmp_rows, HEAD_DIM), lambda b, g, i: (b, 0, 0)),
                  pl.BlockSpec((None, t, kvw), lambda b, g, i: (b, 0, g)),
                  pl.BlockSpec((None, t, kvw), lambda b, g, i: (b, 0, g)),
                  pl.BlockSpec(wmap_t.shape, lambda b, g, i: (0, 0)),
                  pl.BlockSpec(expand.shape, lambda b, g, i: (0, 0, 0))],
        out_specs=pl.BlockSpec((None, Q_BLOCK, gw), lambda b, g, i: (b, i, g)),
        scratch_shapes=[pltpu.VMEM((n_chunks, rows, SEL_CHUNK), F32),
                        pltpu.VMEM((rows, LANES), F32),
                        pltpu.VMEM((rows, t), BF),
                        pltpu.VMEM((1, rows, WINDOW + Q_BLOCK), F32),
                        pltpu.VMEM((rows, LANES), F32),
                        pltpu.VMEM((rows, WINDOW + Q_BLOCK), BF)],
        compiler_params=_cparams(3, 48),
        name="attn_prompt",
    )(q, gates, cmp_flat, kv_sel, kv_win, wmap_t, expand)


def _attn_sample_a_kernel(q_ref, cmp_ref, win_ref, wmap_ref, ocmp_ref, owin_ref, idx_ref, val_ref, *,
                          scale, q_pos, n_cmp, n_slc, win_pos0):
    q = q_ref[...]
    n_h = q.shape[0]
    rq = n_h // N_KV_HEADS
    row_g = _div_pow2(lax.broadcasted_iota(jnp.int32, (n_h, 1), 0), rq)
    cmp_rows = cmp_ref.shape[0] // KV_SLABS
    win_rows = win_ref.shape[0] // KV_SLABS
    n_io = lax.broadcasted_iota(jnp.int32, (1, cmp_rows), 1)
    valid_c = (n_io < n_cmp) & ((n_io * CMP_STRIDE + (CMP_LEN - 1)) <= q_pos)
    kp = win_pos0 + lax.broadcasted_iota(jnp.int32, (1, win_rows), 1)
    valid_w = (kp <= q_pos) & (kp > q_pos - WINDOW)
    o_cmp = jnp.zeros((n_h, HEAD_DIM), F32)
    o_win = jnp.zeros((n_h, HEAD_DIM), F32)
    imp = jnp.zeros((n_h, wmap_ref.shape[1]), F32)
    for g in range(N_KV_HEADS):
        in_g = row_g == g
        kc = cmp_ref[pl.ds(2 * g, cmp_rows, stride=KV_SLABS), :].astype(BF)
        vc = cmp_ref[pl.ds(2 * g + 1, cmp_rows, stride=KV_SLABS), :].astype(BF)
        p = _masked_softmax(_dot_t(q, kc) * scale, valid_c).astype(BF)
        o_cmp = jnp.where(in_g, _dot(p, vc), o_cmp)
        imp_rows = _dot(p, wmap_ref[...])
        imp_g = jnp.sum(jnp.where(in_g, imp_rows, 0.0), axis=0, keepdims=True)
        imp = jnp.where(in_g, imp_g, imp)
        kw = win_ref[pl.ds(2 * g, win_rows, stride=KV_SLABS), :].astype(BF)
        vw = win_ref[pl.ds(2 * g + 1, win_rows, stride=KV_SLABS), :].astype(BF)
        pw = _masked_softmax(_dot_t(q, kw) * scale, valid_w).astype(BF)
        o_win = jnp.where(in_g, _dot(pw, vw), o_win)
    ocmp_ref[...] = o_cmp
    owin_ref[...] = o_win

    j_io = lax.broadcasted_iota(jnp.int32, (1, imp.shape[1]), 1)
    q_blk = q_pos // SEL_BLOCK
    causal = j_io <= q_blk
    forced = (j_io == 0) | (causal & (j_io > q_blk - N_LOCAL_SEL))
    imp = jnp.where(forced, imp + FORCE_BONUS, imp)
    imp = jnp.where(causal, imp, -jnp.inf)
    rank = _rank_desc(imp, n_slc)
    jf = j_io.astype(F32)
    k_io = lax.broadcasted_iota(jnp.int32, (1, idx_ref.shape[1]), 1)
    idx = jnp.zeros(idx_ref.shape, F32)
    val = jnp.zeros(idx_ref.shape, F32)
    for k in range(min(N_SEL, n_slc)):
        hit = rank == k
        ik = jnp.sum(jnp.where(hit, jf, 0.0), axis=1, keepdims=True)
        vk = jnp.sum(jnp.where(hit, jnp.where(imp > -jnp.inf, 1.0, 0.0), 0.0), axis=1, keepdims=True)
        idx = jnp.where(k_io == k, ik, idx)
        val = jnp.where(k_io == k, vk, val)
    idx_ref[...] = idx.astype(jnp.int32)
    val_ref[...] = val.astype(jnp.int32)


def _attn_sample_a(q, cmp_blocks, win_rows, *, scale, q_pos, n_cmp, n_slc, win_pos0):
    nb, n_h, d = q.shape
    n_cols = -(-n_slc // LANES) * LANES
    wmap = jnp.asarray(_overlap_map(cmp_blocks.shape[1] // KV_SLABS, n_slc, n_cols), BF)
    o_sds = jax.ShapeDtypeStruct((nb, n_h, d), F32)
    i_sds = jax.ShapeDtypeStruct((nb, n_h, LANES), jnp.int32)
    blk = lambda a: pl.BlockSpec((None,) + a.shape[1:], lambda b: (b, 0, 0))
    ospec = pl.BlockSpec((None, n_h, d), lambda b: (b, 0, 0))
    ispec = pl.BlockSpec((None, n_h, LANES), lambda b: (b, 0, 0))
    return pl.pallas_call(
        functools.partial(_attn_sample_a_kernel, scale=scale, q_pos=q_pos, n_cmp=n_cmp, n_slc=n_slc,
                          win_pos0=win_pos0),
        out_shape=(o_sds, o_sds, i_sds, i_sds),
        grid=(nb,),
        in_specs=[blk(q), blk(cmp_blocks), blk(win_rows), pl.BlockSpec(wmap.shape, lambda b: (0, 0))],
        out_specs=(ospec, ospec, ispec, ispec),
        compiler_params=_cparams(1, 32),
        name="attn_sample_a",
    )(q, cmp_blocks, win_rows, wmap)


def _attn_sample_b_kernel(idx_ref, val_ref, pt_ref, q_ref, new_ref, gt_ref, ocmp_ref, owin_ref, *rest,
                          scale, q_pos, past_len):
    blk_refs = rest[:N_SEL]
    o_ref = rest[N_SEL]
    b = pl.program_id(0)
    g = pl.program_id(1)
    q = q_ref[...]
    n_h = q.shape[0]
    rq = n_h // N_KV_HEADS
    base = (b * N_KV_HEADS + g) * N_SEL
    n_keys = N_SEL * SEL_BLOCK
    lane = lax.broadcasted_iota(jnp.int32, (1, n_keys), 1)
    k_new = new_ref[pl.ds(2 * g, 1), :]
    v_new = new_ref[pl.ds(2 * g + 1, 1), :]
    ks, vs = [], []
    kpos = _mod_pow2(lane, SEL_BLOCK)
    kval = jnp.zeros((1, n_keys), jnp.int32)
    lane_blk = _div_pow2(lane, SEL_BLOCK)
    for k in range(N_SEL):
        blk = idx_ref[base + k]
        is_new = blk * SEL_BLOCK >= past_len
        ks.append(jnp.where(is_new, k_new, blk_refs[k][pl.ds(2 * g, SEL_BLOCK, stride=KV_SLABS), :]).astype(BF))
        vs.append(jnp.where(is_new, v_new, blk_refs[k][pl.ds(2 * g + 1, SEL_BLOCK, stride=KV_SLABS), :]).astype(BF))
        in_k = lane_blk == k
        kpos = kpos + jnp.where(in_k, blk * SEL_BLOCK, 0)
        kval = jnp.where(in_k, val_ref[base + k], kval)
    valid = (kval > 0) & (kpos <= q_pos)
    p = _masked_softmax(_dot_t(q, jnp.concatenate(ks, axis=0)) * scale, valid).astype(BF)
    o_sel = _dot(p, jnp.concatenate(vs, axis=0))
    gt = gt_ref[...]
    in_g = _div_pow2(lax.broadcasted_iota(jnp.int32, (n_h, 1), 0), rq) == g

    @pl.when(g == 0)
    def _():
        o_ref[...] = gt[:, 0:1] * ocmp_ref[...] + gt[:, 2:3] * owin_ref[...]

    o_ref[...] += jnp.where(in_g, gt[:, 1:2] * o_sel, 0.0)


def _attn_sample_b(sel_idx, sel_val, page_table, q, kv_sel_new, gates, o_cmp, o_win, pool_sel, *,
                   scale, q_pos, past_len):
    nb, n_h, d = q.shape
    halves = PAGE_SIZE // SEL_BLOCK
    blk_rows = SEL_BLOCK * KV_SLABS
    pool = pool_sel.reshape(pool_sel.shape[0] // blk_rows, blk_rows, d)

    def blk_spec(k):
        def index_map(b, g, idx, val, pt):
            row0 = jnp.minimum(idx[(b * N_KV_HEADS + g) * N_SEL + k] * SEL_BLOCK, past_len - 1)
            page = lax.shift_right_logical(row0, _log2(PAGE_SIZE))
            half = lax.shift_right_logical(lax.bitwise_and(row0, PAGE_SIZE - 1), _log2(SEL_BLOCK))
            return (pt[b, page] * halves + half, 0, 0)
        return pl.BlockSpec((None, blk_rows, d), index_map)

    per_b = lambda a: pl.BlockSpec((None,) + a.shape[1:], lambda b, g, idx, val, pt: (b, 0, 0))
    new_rows = kv_sel_new.reshape(nb, KV_SLABS, d)
    grid_spec = pltpu.PrefetchScalarGridSpec(
        num_scalar_prefetch=3,
        grid=(nb, N_KV_HEADS),
        in_specs=[per_b(q), per_b(new_rows), per_b(gates), per_b(o_cmp), per_b(o_win)]
        + [blk_spec(k) for k in range(N_SEL)],
        out_specs=pl.BlockSpec((None, n_h, d), lambda b, g, idx, val, pt: (b, 0, 0)),
    )
    return pl.pallas_call(
        functools.partial(_attn_sample_b_kernel, scale=scale, q_pos=q_pos, past_len=past_len),
        out_shape=jax.ShapeDtypeStruct((nb, n_h, d), F32),
        grid_spec=grid_spec,
        compiler_params=_cparams(2, 32),
        name="attn_sample_b",
    )(sel_idx, sel_val, page_table, q, new_rows, gates, o_cmp, o_win, *([pool] * N_SEL))


def kernel(x_prompt, x_sample, cache_cmp_kv, cache_sel_kv, cache_win_kv, page_table, c_prompt, c_sample, w_ada, b_ada, ln_g, ln_b, a_w_in, a_v_g, a_v_b, a_w_s, a_b_s, a_w_out, w_kv, cmp_pe, cmp_w1, cmp_b1, cmp_w2, cmp_b2, b_w_qg, b_w_o, ffn_w_in, ffn_w_out):
    nb, t, d = x_prompt.shape
    ns, tq, _ = x_sample.shape
    assert tq == 1, "the sample path handles one new token per sequence"
    depth = w_ada.shape[0]
    n_a = a_w_in.shape[0]
    alpha = float((2 * depth) ** 0.25)
    scale = float(HEAD_DIM ** -0.5)
    n_pool = cache_cmp_kv.shape[0]
    past_len = page_table.shape[1] * PAGE_SIZE
    kvw = N_KV_HEADS * 2 * HEAD_DIM
    q_width = b_w_o.shape[1]
    n_heads = q_width // HEAD_DIM
    rq = n_heads // N_KV_HEADS

    mod = _ada_modulation(jnp.concatenate([c_prompt, c_sample], axis=0), w_ada, b_ada)

    def mods(layer, sub):
        m = mod[layer * 2 + sub]
        out = []
        for part in range(3):
            sl = m[:, part * d:(part + 1) * d]
            out.append((sl[:nb].reshape(nb, 1, d), sl[nb:].reshape(1, ns, d)))
        return out

    wb = lambda w: w.astype(BF)
    hidden = cmp_w1.shape[-1]
    halves = CMP_LEN // CMP_STRIDE
    assert halves == 2, "a compression block spans two stride-chunks"
    w1all = wb(cmp_w1.reshape(2, halves, CMP_STRIDE, HEAD_DIM, hidden)
               .transpose(2, 3, 0, 1, 4).reshape(CMP_STRIDE * HEAD_DIM, 2 * halves * hidden))
    pe16 = jnp.pad(cmp_pe.reshape(2 * halves, CMP_STRIDE * HEAD_DIM), ((0, 16 - 2 * halves), (0, 0)))
    w2all = wb(cmp_w2.transpose(1, 0, 2).reshape(hidden, 2 * HEAD_DIM))

    a_w_out_b, b_w_o_b, ffn_w_in_b, ffn_w_out_b = wb(a_w_out), wb(b_w_o), wb(ffn_w_in), wb(ffn_w_out)

    tm_p = 512
    tpg_p = t // tm_p
    prompt = dict(tm=tm_p, tiles_per_group=tpg_p)
    sample = dict(tm=ns, tiles_per_group=1)

    xp = x_prompt.reshape(nb * t, d)
    xs = x_sample.reshape(ns, d)
    hp = hs = None
    chunk_v = []
    outs = {}
    for layer in range(depth):
        (sh_p, sh_s), (sc_p, sc_s), (gt_p, gt_s) = mods(layer, 0)
        (fsh_p, fsh_s), (fsc_p, fsc_s), (fgt_p, fgt_s) = mods(layer, 1)
        if layer < n_a:
            w_in = wb(a_w_in[layer])
            ap = _gmlp_in(xp, sc_p, sh_p, w_in, a_v_g[layer], a_v_b[layer], a_w_s[layer], a_b_s[layer],
                          chunked=True, **prompt)
            as_, v_rows = _gmlp_in(xs, sc_s, sh_s, w_in, a_v_g[layer], a_v_b[layer], a_w_s[layer], a_b_s[layer],
                                   chunked=False, **sample)
            chunk_v.append(v_rows.reshape(ns, tq, -1))
            xp, hp = _proj_res_ln(ap, a_w_out_b, xp, gt_p, ln_g[layer, 0], ln_b[layer, 0], (fsc_p, fsh_p),
                                  alpha=alpha, layer=layer, **prompt)
            xs, hs = _proj_res_ln(as_, a_w_out_b, xs, gt_s, ln_g[layer, 0], ln_b[layer, 0], (fsc_s, fsh_s),
                                  alpha=alpha, layer=layer, **sample)
        else:
            if layer == n_a:
                w_kvb = wb(w_kv)
                cmp_p, sel_p, win_p, sel_pb, win_pb = _kv_proj(xp, w_kvb, tm=tm_p)
                cmp_s, sel_s, win_s, _, _ = _kv_proj(xs, w_kvb, tm=ns)
                pages_per_seq_p = t // PAGE_SIZE
                ident = jnp.arange(nb * pages_per_seq_p, dtype=jnp.int32).reshape(nb, pages_per_seq_p)
                cmp_blocks_p = _compress(cmp_p, ident, w1all, pe16, cmp_b1, w2all, cmp_b2)
                assert (past_len + tq) // CMP_STRIDE == past_len // CMP_STRIDE
                cmp_blocks_s = _compress(cache_cmp_kv.reshape(n_pool * PAGE_FLAT_ROWS, HEAD_DIM), page_table,
                                         w1all, pe16, cmp_b1, w2all, cmp_b2)
                n_win = cache_win_kv.shape[1]
                win_keys_s = jnp.concatenate([cache_win_kv.reshape(ns, n_win * KV_SLABS, HEAD_DIM),
                                              win_s.reshape(ns, tq * KV_SLABS, HEAD_DIM)],
                                             axis=1)[:, -n_win * KV_SLABS:]
                outs["kv_p"] = (cmp_p, sel_p, win_p)
                outs["kv_pb"] = (sel_pb.reshape(nb, t, kvw), win_pb.reshape(nb, t, kvw))
                outs["kv_s"] = (cmp_s, sel_s, win_keys_s)
            j = layer - n_a
            w_qg = b_w_qg[j]
            w_q = wb(w_qg[:, :q_width])
            n_gates = w_qg.shape[1] - q_width
            w_gate = wb(jnp.pad(w_qg[:, q_width:], ((0, 0), (0, LANES - n_gates))))
            assert layer > 0, "an attention layer must follow another layer"
            q_p, g_p = _qg_proj(hp, w_q, w_gate, tm=tm_p, gates_per_group=rq * 3)
            q_s, g_s = _qg_proj(hs, w_q, w_gate, tm=ns, gates_per_group=rq * 3)
            o_p = _attn_prompt(q_p.reshape(nb, t, q_width), g_p.reshape(nb, t, N_KV_HEADS * LANES),
                               cmp_blocks_p, outs["kv_pb"][0], outs["kv_pb"][1], scale=scale)
            n_cmp_s = (past_len + tq) // CMP_STRIDE - CMP_LEN // CMP_STRIDE + 1
            n_slc_s = -(-(past_len + tq) // SEL_BLOCK)
            q_s3 = q_s.reshape(ns, n_heads, HEAD_DIM)
            o_cmp_s, o_win_s, idx_s, val_s = _attn_sample_a(
                q_s3, cmp_blocks_s, outs["kv_s"][2], scale=scale, q_pos=past_len, n_cmp=n_cmp_s, n_slc=n_slc_s,
                win_pos0=past_len + tq - outs["kv_s"][2].shape[1] // KV_SLABS)
            sel_idx = idx_s[:, ::rq, :N_SEL].reshape(-1)
            sel_val = val_s[:, ::rq, :N_SEL].reshape(-1)
            gates_s = g_s.reshape(ns, N_KV_HEADS, LANES)[:, :, :rq * 3].reshape(ns, n_heads, 3)
            o_s = _attn_sample_b(sel_idx, sel_val, page_table, q_s3, outs["kv_s"][1], gates_s, o_cmp_s, o_win_s,
                                 cache_sel_kv.reshape(n_pool * PAGE_FLAT_ROWS, HEAD_DIM), scale=scale,
                                 q_pos=past_len, past_len=past_len)
            xp, hp = _proj_res_ln(o_p.reshape(nb * t, q_width), b_w_o_b, xp, gt_p, ln_g[layer, 0], ln_b[layer, 0],
                                  (fsc_p, fsh_p), alpha=alpha, layer=j, **prompt)
            xs, hs = _proj_res_ln(o_s.reshape(ns, q_width), b_w_o_b, xs, gt_s, ln_g[layer, 0], ln_b[layer, 0],
                                  (fsc_s, fsh_s), alpha=alpha, layer=j, **sample)
        fp = _ffn_in(hp, ffn_w_in_b, layer, tm=2 * tm_p)
        fs = _ffn_in(hs, ffn_w_in_b, layer, tm=ns)
        nxt_p = nxt_s = None
        if n_a <= layer + 1 < depth:
            (nsh_p, nsh_s), (nsc_p, nsc_s), _ = mods(layer + 1, 0)
            nxt_p, nxt_s = (nsc_p, nsh_p), (nsc_s, nsh_s)
        tm_f = tm_p // 2
        xp, hp = _proj_res_ln(fp, ffn_w_out_b, xp, fgt_p, ln_g[layer, 1], ln_b[layer, 1], nxt_p, alpha=alpha,
                              layer=layer, tm=tm_f, tiles_per_group=t // tm_f)
        xs, hs = _proj_res_ln(fs, ffn_w_out_b, xs, fgt_s, ln_g[layer, 1], ln_b[layer, 1], nxt_s, alpha=alpha,
                              layer=layer, **sample)

    cmp_p, sel_p, win_p = outs["kv_p"]
    cmp_s, sel_s, win_keys_s = outs["kv_s"]
    kv_shape = (N_KV_HEADS, 2, HEAD_DIM)
    n_win_p = min(WINDOW, t)
    return (xp.reshape(nb, t, d), xs.reshape(ns, tq, d),
            cmp_p.reshape((nb, t) + kv_shape), sel_p.reshape((nb, t) + kv_shape),
            win_p.reshape((nb, t) + kv_shape)[:, -n_win_p:],
            cmp_s.reshape((ns, tq) + kv_shape), sel_s.reshape((ns, tq) + kv_shape),
            win_keys_s.reshape((ns, -1) + kv_shape),
            jnp.stack(chunk_v))
```

```python
import functools

import numpy as np
import jax
import jax.numpy as jnp
from jax import lax
from jax.experimental import pallas as pl
from jax.experimental.pallas import tpu as pltpu

BF = jnp.bfloat16
F32 = jnp.float32

LN_EPS = 1e-5
CHUNK = 128
HEAD_DIM = 128
N_KV_HEADS = 4
CMP_LEN = 32
CMP_STRIDE = 16
SEL_BLOCK = 64
N_SEL = 16
N_LOCAL_SEL = 2
FORCE_BONUS = 1e4
LOG2_E = 1.4426950408889634
WINDOW = 512
Q_BLOCK = 256
PAGE_SIZE = 128

LANES = 128
MIB = 1024 * 1024


def _cparams(n_axes, vmem_mib):
    return pltpu.CompilerParams(dimension_semantics=("arbitrary",) * n_axes,
                                vmem_limit_bytes=vmem_mib * MIB)


def _dot(a, b):
    return jnp.dot(a, b, preferred_element_type=F32)


def _dot_t(a, b):
    return lax.dot_general(a, b, (((1,), (1,)), ((), ())), preferred_element_type=F32)


def _layer_norm(x, g, b):
    mu = jnp.mean(x, axis=-1, keepdims=True)
    xc = x - mu
    var = jnp.mean(xc * xc, axis=-1, keepdims=True)
    return xc * lax.rsqrt(var + LN_EPS) * g + b


def _masked_softmax(s, valid):
    s = jnp.where(valid, s, -jnp.inf)
    m = jnp.max(s, axis=-1, keepdims=True)
    m = jnp.where(m == -jnp.inf, 0.0, m)
    e = jnp.exp(s - m)
    d = jnp.maximum(jnp.sum(e, axis=-1, keepdims=True), 1e-30)
    return e * (1.0 / d)


def _log2(n):
    assert n & (n - 1) == 0, "power of two expected"
    return n.bit_length() - 1


def _div_pow2(x, n):
    return lax.shift_right_logical(x, jnp.int32(_log2(n)))


def _mod_pow2(x, n):
    assert n & (n - 1) == 0, "power of two expected"
    return lax.bitwise_and(x, jnp.int32(n - 1))


def _rank_desc(imp, n):
    j_io = lax.broadcasted_iota(jnp.int32, (1, imp.shape[1]), 1)
    rank = jnp.zeros(imp.shape, F32)
    for jp in range(n):
        col = imp[:, jp:jp + 1]
        before = jnp.where(col == imp, jnp.where(jp < j_io, 1.0, 0.0), jnp.where(col > imp, 1.0, 0.0))
        rank = rank + before
    return rank


def _ada_kernel(c_ref, w_ref, b_ref, o_ref):
    a = jax.nn.silu(c_ref[...]).astype(BF)
    o_ref[...] = _dot(a, w_ref[...].astype(BF)) + b_ref[...]


def _ada_modulation(c_all, w_ada, b_ada):
    n_sub = w_ada.shape[0] * w_ada.shape[1]
    d, n3 = w_ada.shape[2], w_ada.shape[3]
    m = c_all.shape[0]
    tn = 512
    w = w_ada.reshape(n_sub, d, n3)
    b = b_ada.reshape(n_sub, 1, n3)
    return pl.pallas_call(
        _ada_kernel,
        out_shape=jax.ShapeDtypeStruct((n_sub, m, n3), F32),
        grid=(n_sub, n3 // tn),
        in_specs=[pl.BlockSpec((m, d), lambda s, j: (0, 0)),
                  pl.BlockSpec((None, d, tn), lambda s, j: (s, 0, j)),
                  pl.BlockSpec((None, 1, tn), lambda s, j: (s, 0, j))],
        out_specs=pl.BlockSpec((None, m, tn), lambda s, j: (s, 0, j)),
        compiler_params=_cparams(2, 32),
        name="ada_modulation",
    )(c_all, w, b)


def _mod_spec(mod, tiles_per_group):
    rb, d = mod.shape[1], mod.shape[2]
    return pl.BlockSpec((None, rb, d), lambda i: (i // tiles_per_group, 0, 0))


def _gmlp_in_kernel(x_ref, sc_ref, sh_ref, w_ref, vg_ref, vb_ref, ws_ref, bs_ref, *rest,
                    n_j, tn, chunked):
    if chunked:
        o_ref, z_scr = rest
    else:
        o_ref, v_ref, z_scr = rest
    h = (x_ref[...] * (1.0 + sc_ref[...]) + sh_ref[...]).astype(BF)
    for jj in range(n_j):
        z_scr[jj] = jax.nn.gelu(_dot(h, w_ref[:, jj * tn:(jj + 1) * tn]))

    tm = z_scr.shape[1]
    half = n_j // 2
    e_a = half * tn
    s1 = jnp.zeros((tm, 1), F32)
    for jj in range(half):
        s1 = s1 + jnp.sum(z_scr[half + jj], axis=-1, keepdims=True)
    mu = s1 / e_a
    s2 = jnp.zeros((tm, 1), F32)
    for jj in range(half):
        vc = z_scr[half + jj] - mu
        s2 = s2 + jnp.sum(vc * vc, axis=-1, keepdims=True)
    rstd = lax.rsqrt(s2 / e_a + LN_EPS)
    gpt = tn // CHUNK
    if chunked:
        row = lax.broadcasted_iota(jnp.int32, (CHUNK, CHUNK), 0)
        col = lax.broadcasted_iota(jnp.int32, (CHUNK, CHUNK), 1)
        for jj in range(half):
            for gi in range(gpt):
                g = jj * gpt + gi
                cs = slice(gi * CHUNK, (gi + 1) * CHUNK)
                wm = jnp.where(row >= col, ws_ref[g], 0.0).astype(BF)
                bcol = bs_ref[:, g:g + 1]
                vg = vg_ref[:, g * CHUNK:(g + 1) * CHUNK]
                vb = vb_ref[:, g * CHUNK:(g + 1) * CHUNK]
                for c in range(tm // CHUNK):
                    rs = slice(c * CHUNK, (c + 1) * CHUNK)
                    vn = (z_scr[half + jj, rs, cs] - mu[rs]) * rstd[rs] * vg + vb
                    mixed = _dot(wm, vn.astype(BF)) + bcol
                    o_ref[rs, g * CHUNK:(g + 1) * CHUNK] = (z_scr[jj, rs, cs] * mixed).astype(BF)
    else:
        for jj in range(half):
            cs = slice(jj * tn, (jj + 1) * tn)
            vn = (z_scr[half + jj] - mu) * rstd * vg_ref[:, cs] + vb_ref[:, cs]
            v_ref[:, cs] = vn
            mixed = ws_ref[:, cs] * vn + bs_ref[:, cs]
            o_ref[:, cs] = (z_scr[jj] * mixed).astype(BF)


def _gmlp_in(x, sc, sh, w_in, v_g, v_b, w_s, b_s, *, tm, tiles_per_group, chunked):
    r, d = x.shape
    n2 = w_in.shape[1]
    e_a = n2 // 2
    tn = 512
    n_j = n2 // tn
    n_g = w_s.shape[0]
    full = lambda a: pl.BlockSpec(a.shape, lambda i: (0,) * a.ndim)
    row_tile = pl.BlockSpec((tm, e_a), lambda i: (i, 0))
    if chunked:
        ws_arr = w_s
        bs_arr = b_s.T
        out_shape = jax.ShapeDtypeStruct((r, e_a), BF)
        out_specs = row_tile
    else:
        ws_arr = jnp.repeat(w_s[:, 0, 0], e_a // n_g).reshape(1, e_a)
        bs_arr = jnp.repeat(b_s[:, 0], e_a // n_g).reshape(1, e_a)
        out_shape = (jax.ShapeDtypeStruct((r, e_a), BF), jax.ShapeDtypeStruct((r, e_a), F32))
        out_specs = (row_tile, row_tile)
    vg2, vb2 = v_g.reshape(1, e_a), v_b.reshape(1, e_a)
    return pl.pallas_call(
        functools.partial(_gmlp_in_kernel, n_j=n_j, tn=tn, chunked=chunked),
        out_shape=out_shape,
        grid=(r // tm,),
        in_specs=[pl.BlockSpec((tm, d), lambda i: (i, 0)),
                  _mod_spec(sc, tiles_per_group), _mod_spec(sh, tiles_per_group),
                  pl.BlockSpec(w_in.shape, lambda i: (0, 0), pipeline_mode=pl.Buffered(1)),
                  full(vg2), full(vb2), full(ws_arr), full(bs_arr)],
        out_specs=out_specs,
        scratch_shapes=[pltpu.VMEM((n_j, tm, tn), F32)],
        compiler_params=_cparams(1, 48),
        name="gmlp_in_chunked" if chunked else "gmlp_in_single",
    )(x, sc, sh, w_in, vg2, vb2, ws_arr, bs_arr)


LN_ROWS = 8


def _proj_res_ln_kernel(a_ref, w_ref, x_ref, gate_ref, lg_ref, lb_ref, *rest, alpha, emit_next):
    if emit_next:
        sc_ref, sh_ref, y_ref, h_ref, acc = rest
    else:
        y_ref, acc = rest
    acc[...] = _dot(a_ref[...].astype(BF), w_ref[...])
    tm = acc.shape[0]
    per_row_mod = gate_ref.shape[0] > 1
    step = min(LN_ROWS, tm)
    assert tm % step == 0
    for c in range(tm // step):
        rs = slice(c * step, (c + 1) * step)
        ms = rs if per_row_mod else slice(None)
        t = alpha * x_ref[rs, :] + (1.0 + gate_ref[ms, :]) * acc[rs, :]
        y = _layer_norm(t, lg_ref[...], lb_ref[...])
        y_ref[rs, :] = y
        if emit_next:
            h_ref[rs, :] = (y * (1.0 + sc_ref[ms, :]) + sh_ref[ms, :]).astype(BF)


def _proj_res_ln(a, w, x, gate, ln_g, ln_b, nxt, *, alpha, tm, tiles_per_group, layer=0):
    r, kdim = a.shape
    d = w.shape[2]
    emit_next = nxt is not None
    mod_spec = lambda m: _mod_spec(m, tiles_per_group)
    in_specs = [pl.BlockSpec((tm, kdim), lambda i: (i, 0)),
                pl.BlockSpec((None, kdim, d), lambda i: (layer, 0, 0), pipeline_mode=pl.Buffered(1)),
                pl.BlockSpec((tm, d), lambda i: (i, 0)),
                mod_spec(gate),
                pl.BlockSpec((1, d), lambda i: (0, 0)),
                pl.BlockSpec((1, d), lambda i: (0, 0))]
    args = [a, w, x, gate, ln_g.reshape(1, d), ln_b.reshape(1, d)]
    out_shape = [jax.ShapeDtypeStruct((r, d), F32)]
    out_specs = [pl.BlockSpec((tm, d), lambda i: (i, 0))]
    if emit_next:
        in_specs += [mod_spec(nxt[0]), mod_spec(nxt[1])]
        args += [nxt[0], nxt[1]]
        out_shape.append(jax.ShapeDtypeStruct((r, d), BF))
        out_specs.append(pl.BlockSpec((tm, d), lambda i: (i, 0)))
    res = pl.pallas_call(
        functools.partial(_proj_res_ln_kernel, alpha=alpha, emit_next=emit_next),
        out_shape=tuple(out_shape),
        grid=(r // tm,),
        in_specs=in_specs,
        out_specs=tuple(out_specs),
        scratch_shapes=[pltpu.VMEM((tm, d), F32)],
        compiler_params=_cparams(1, 56),
        name="proj_res_ln",
    )(*args)
    return res if emit_next else (res[0], None)


def _ffn_in_kernel(h_ref, wg_ref, wu_ref, o_ref):
    h = h_ref[...]
    o_ref[...] = (jax.nn.silu(_dot(h, wg_ref[...])) * _dot(h, wu_ref[...])).astype(BF)


def _ffn_in(h, w_in, layer, *, tm):
    r, d = h.shape
    d_ff = w_in.shape[2] // 2
    tn = 512
    n_j = d_ff // tn
    return pl.pallas_call(
        _ffn_in_kernel,
        out_shape=jax.ShapeDtypeStruct((r, d_ff), BF),
        grid=(r // tm, n_j),
        in_specs=[pl.BlockSpec((tm, d), lambda i, j: (i, 0)),
                  pl.BlockSpec((None, d, tn), lambda i, j: (layer, 0, j)),
                  pl.BlockSpec((None, d, tn), lambda i, j: (layer, 0, j + n_j))],
        out_specs=pl.BlockSpec((tm, tn), lambda i, j: (i, j)),
        compiler_params=_cparams(2, 40),
        name="ffn_in",
    )(h, w_in, w_in)


KV_SLABS = N_KV_HEADS * 2


def _kv_proj_kernel(y_ref, w_ref, f0_ref, f1_ref, f2_ref, b1_ref, b2_ref):
    xb = y_ref[...].astype(BF)
    tm = xb.shape[0]
    n = w_ref.shape[1] // 3
    for idx, (f_ref, b_ref) in enumerate(((f0_ref, None), (f1_ref, b1_ref), (f2_ref, b2_ref))):
        r = _dot(xb, w_ref[:, idx * n:(idx + 1) * n])
        for hk in range(KV_SLABS):
            f_ref[pl.ds(hk, tm, stride=KV_SLABS), :] = r[:, hk * HEAD_DIM:(hk + 1) * HEAD_DIM]
        if b_ref is not None:
            b_ref[...] = r.astype(BF)


def _kv_proj(y, w_kv, *, tm):
    r, d = y.shape
    n = w_kv.shape[1] // 3
    flat = jax.ShapeDtypeStruct((r * KV_SLABS, HEAD_DIM), F32)
    fspec = pl.BlockSpec((tm * KV_SLABS, HEAD_DIM), lambda i: (i, 0))
    wide = jax.ShapeDtypeStruct((r, n), BF)
    wspec = pl.BlockSpec((tm, n), lambda i: (i, 0))
    return pl.pallas_call(
        _kv_proj_kernel,
        out_shape=(flat, flat, flat, wide, wide),
        grid=(r // tm,),
        in_specs=[pl.BlockSpec((tm, d), lambda i: (i, 0)),
                  pl.BlockSpec(w_kv.shape, lambda i: (0, 0), pipeline_mode=pl.Buffered(1))],
        out_specs=(fspec, fspec, fspec, wspec, wspec),
        compiler_params=_cparams(1, 48),
        name="kv_proj",
    )(y, w_kv)


def _qg_proj_kernel(h_ref, wq_ref, wg_ref, q_ref, g_ref, *, gates_per_group):
    h = h_ref[...]
    q_ref[...] = _dot(h, wq_ref[...]).astype(BF)
    gates = jax.nn.sigmoid(_dot(h, wg_ref[...]))
    for grp in range(g_ref.shape[1] // LANES):
        shift = (LANES - grp * gates_per_group) % LANES
        g_ref[:, grp * LANES:(grp + 1) * LANES] = gates if shift == 0 else pltpu.roll(gates, shift, 1)


def _qg_proj(h, w_q, w_g, *, tm, gates_per_group):
    r, d = h.shape
    qw = w_q.shape[1]
    gw = N_KV_HEADS * LANES
    resident = lambda w: pl.BlockSpec(w.shape, lambda i: (0, 0), pipeline_mode=pl.Buffered(1))
    return pl.pallas_call(
        functools.partial(_qg_proj_kernel, gates_per_group=gates_per_group),
        out_shape=(jax.ShapeDtypeStruct((r, qw), BF), jax.ShapeDtypeStruct((r, gw), F32)),
        grid=(r // tm,),
        in_specs=[pl.BlockSpec((tm, d), lambda i: (i, 0)), resident(w_q), resident(w_g)],
        out_specs=(pl.BlockSpec((tm, qw), lambda i: (i, 0)),
                   pl.BlockSpec((tm, gw), lambda i: (i, 0))),
        compiler_params=_cparams(1, 40),
        name="qg_proj",
    )(h, w_q, w_g)


PAGES_PER_STEP = 16
CHUNKS_PER_PAGE = PAGE_SIZE // CMP_STRIDE


PAGE_FLAT_ROWS = PAGE_SIZE * KV_SLABS
CHUNK_FLAT_ROWS = CMP_STRIDE * KV_SLABS


PAGE_SLOTS = 3


def _compress_kernel(pt_ref, pages_hbm, w1_ref, pe_ref, b1_ref, w2_ref, b2_ref, o_ref,
                     p0_scr, p1_scr, page_buf, sems, *, n_steps, n_total):
    t = pl.program_id(1)
    n = pl.program_id(0) * n_steps + t
    ahead = PAGE_SLOTS - 1
    rows_step = PAGES_PER_STEP * CHUNKS_PER_PAGE * N_KV_HEADS
    n_rows = n_steps * rows_step
    hidden = w2_ref.shape[0]

    def page_copy(m, p):
        slot = m % PAGE_SLOTS
        page = pt_ref[m // n_steps, (m % n_steps) * PAGES_PER_STEP + p]
        src = pages_hbm.at[pl.ds(pl.multiple_of(page * PAGE_FLAT_ROWS, PAGE_FLAT_ROWS), PAGE_FLAT_ROWS), :]
        return pltpu.make_async_copy(src, page_buf.at[slot, p], sems.at[slot])

    def start_step(m):
        for p in range(PAGES_PER_STEP):
            page_copy(m, p).start()

    @pl.when(n == 0)
    def _():
        for m in range(min(ahead, n_total)):
            start_step(m)

    @pl.when(n + ahead < n_total)
    def _():
        start_step(n + ahead)

    for p in range(PAGES_PER_STEP):
        page_copy(n, p).wait()
    slot = n % PAGE_SLOTS

    @pl.when(t == 0)
    def _():
        p1_scr[:, pl.ds(n_rows, 8), :] = jnp.zeros((2, 8, hidden), F32)

    row0 = pl.multiple_of(t * rows_step, rows_step)
    for k in range(2):
        per_s = []
        for s in range(CMP_STRIDE):
            rows = [page_buf[slot, p, pl.ds(c * CHUNK_FLAT_ROWS + s * KV_SLABS + k, N_KV_HEADS, stride=2), :]
                    for p in range(PAGES_PER_STEP) for c in range(CHUNKS_PER_PAGE)]
            per_s.append(jnp.concatenate(rows, axis=0))
        x = jnp.concatenate(per_s, axis=1).astype(BF)
        pr = _dot(x, w1_ref[:, k * 2 * hidden:(k + 1) * 2 * hidden])
        p0_scr[k, pl.ds(row0, rows_step), :] = pr[:, :hidden]
        p1_scr[k, pl.ds(row0, rows_step), :] = pr[:, hidden:]

    @pl.when(t == n_steps - 1)
    def _():
        cst = _dot(pe_ref[...].astype(BF), w1_ref[...])
        for k in range(2):
            c0 = 2 * k * hidden
            const = (cst[2 * k:2 * k + 1, c0:c0 + hidden] + cst[2 * k + 1:2 * k + 2, c0 + hidden:c0 + 2 * hidden]
                     + b1_ref[k:k + 1, :])
            pre = p0_scr[k, 0:n_rows, :] + p1_scr[k, N_KV_HEADS:n_rows + N_KV_HEADS, :] + const
            res = _dot(jax.nn.gelu(pre).astype(BF), w2_ref[:, k * HEAD_DIM:(k + 1) * HEAD_DIM]) + b2_ref[k:k + 1, :]
            o_ref[pl.ds(k, n_rows, stride=2), :] = res


def _compress(pages_flat, page_table, w1all, pe16, b1, w2all, b2):
    nb, n_pages = page_table.shape
    n_steps = n_pages // PAGES_PER_STEP
    n_rows = n_pages * CHUNKS_PER_PAGE * KV_SLABS
    hidden = w2all.shape[0]

    full = lambda a: pl.BlockSpec(a.shape, lambda b, t, pt: (0,) * a.ndim)
    grid_spec = pltpu.PrefetchScalarGridSpec(
        num_scalar_prefetch=1,
        grid=(nb, n_steps),
        in_specs=[pl.BlockSpec(memory_space=pl.ANY),
                  full(w1all), full(pe16), full(b1), full(w2all), full(b2)],
        out_specs=pl.BlockSpec((None, n_rows, HEAD_DIM), lambda b, t, pt: (b, 0, 0)),
        scratch_shapes=[pltpu.VMEM((2, n_rows // 2, hidden), F32), pltpu.VMEM((2, n_rows // 2 + 8, hidden), F32),
                        pltpu.VMEM((PAGE_SLOTS, PAGES_PER_STEP, PAGE_FLAT_ROWS, HEAD_DIM), F32),
                        pltpu.SemaphoreType.DMA((PAGE_SLOTS,))],
    )
    return pl.pallas_call(
        functools.partial(_compress_kernel, n_steps=n_steps, n_total=nb * n_steps),
        out_shape=jax.ShapeDtypeStruct((nb, n_rows, HEAD_DIM), F32),
        grid_spec=grid_spec,
        compiler_params=_cparams(2, 48),
        name="compress_blocks",
    )(page_table, pages_flat, w1all, pe16, b1, w2all, b2)


def _overlap_map(n_cmp_rows, n_slc, n_cols):
    cmp_lo = np.arange(n_cmp_rows)[:, None] * CMP_STRIDE
    slc_lo = np.arange(n_cols)[None, :] * SEL_BLOCK
    ov = np.minimum(cmp_lo + CMP_LEN, slc_lo + SEL_BLOCK) - np.maximum(cmp_lo, slc_lo)
    w = np.clip(ov, 0, None).astype(np.float32) / CMP_LEN
    w[:, n_slc:] = 0.0
    return w


SEL_CHUNK = 512


def _attn_prompt_kernel(q_ref, gt_ref, cmp_ref, sel_ref, win_ref, wmap_t_ref, exp_ref, o_ref,
                        s_sel, m_run, e_scr, s_win, m_win, e_win, *, scale, n_slc, n_qb, n_widths):
    g = pl.program_id(1)
    qb = pl.program_id(2)
    q4 = q_ref[...]
    nq = q4.shape[0]
    rq = q4.shape[1] // HEAD_DIM
    qs = jnp.concatenate([q4[:, r * HEAD_DIM:(r + 1) * HEAD_DIM] for r in range(rq)], axis=0)
    qpos = qb * nq + lax.broadcasted_iota(jnp.int32, (nq, 1), 0)
    rows = [slice(r * nq, (r + 1) * nq) for r in range(rq)]

    head_groups = [rows[:rq // 2], rows[rq // 2:]] if rq % 2 == 0 else [rows]

    def attend(n_chunks, chunk, s_scr, m_run, e_scr, keys, mask, values):
        width = n_chunks * chunk
        values_and_ones = jnp.concatenate([values, jnp.ones(values.shape, BF)], axis=1)
        m_run[...] = jnp.full(m_run.shape, -jnp.inf, F32)
        for grp in head_groups:
            g_rows = slice(grp[0].start, grp[-1].stop)
            for c in range(n_chunks):
                s = _dot_t(qs[g_rows], keys(c))
                ok = mask(c)
                for i, rs in enumerate(grp):
                    sm = jnp.where(ok, s[i * nq:(i + 1) * nq], -jnp.inf) * (scale * LOG2_E)
                    s_scr[c, rs, :] = sm
                    mx = m_run[rs, :]
                    for j in range(chunk // LANES):
                        mx = jnp.maximum(mx, sm[:, j * LANES:(j + 1) * LANES])
                    m_run[rs, :] = mx
        outs = []
        for grp in head_groups:
            g_rows = slice(grp[0].start, grp[-1].stop)
            m = jnp.max(m_run[g_rows, :], axis=-1, keepdims=True)
            m = jnp.where(m == -jnp.inf, 0.0, m)
            for c in range(n_chunks):
                for i, rs in enumerate(grp):
                    e = jnp.exp2(s_scr[c, rs, :] - m[i * nq:(i + 1) * nq])
                    e_scr[rs, c * chunk:(c + 1) * chunk] = e.astype(BF)
            pv = _dot(e_scr[g_rows, :width], values_and_ones)
            outs.append(pv[:, :HEAD_DIM] * (1.0 / jnp.maximum(pv[:, HEAD_DIM:], 1e-30)))
        return jnp.concatenate(outs, axis=0)

    def step(n_sel_chunks, first_q):
        span = WINDOW + nq
        k0 = pl.multiple_of(jnp.maximum(qb * nq - WINDOW, 0), nq)
        kp = k0 + lax.broadcasted_iota(jnp.int32, (1, span), 1)
        valid_w = lax.bitcast_convert_type(qpos - kp, jnp.uint32) < jnp.uint32(WINDOW)
        o_win = attend(1, span, s_win, m_win, e_win, lambda c: win_ref[pl.ds(k0, span), :HEAD_DIM],
                       lambda c: valid_w, win_ref[pl.ds(k0, span), HEAD_DIM:])

        n_cmp_rows = cmp_ref.shape[0] // KV_SLABS
        kc = cmp_ref[pl.ds(g * 2, n_cmp_rows, stride=KV_SLABS), :].astype(BF)
        vc = cmp_ref[pl.ds(g * 2 + 1, n_cmp_rows, stride=KV_SLABS), :].astype(BF)
        n_io = lax.broadcasted_iota(jnp.int32, (1, n_cmp_rows), 1)
        valid_c = (n_io * CMP_STRIDE + (CMP_LEN - 1)) <= qpos
        s_c = _dot_t(qs, kc) * scale
        ps = [_masked_softmax(s_c[rs], valid_c).astype(BF) for rs in rows]
        o_cmp = _dot(jnp.concatenate(ps, axis=0), vc)

        imp = _dot_t(wmap_t_ref[...], jnp.concatenate(ps, axis=1))
        n_rows = imp.shape[0]
        j_io = lax.broadcasted_iota(jnp.int32, (n_rows, 1), 0)
        q_blk = _div_pow2(qb * nq + lax.broadcasted_iota(jnp.int32, (1, nq), 1), SEL_BLOCK)
        causal = j_io <= q_blk
        forced = (j_io == 0) | (causal & (j_io > q_blk - N_LOCAL_SEL))
        imp = jnp.where(forced, imp + FORCE_BONUS, imp)
        imp = jnp.where(causal, imp, -jnp.inf)
        rank = jnp.zeros(imp.shape, F32)
        for jp in range(n_slc):
            other = imp[jp:jp + 1, :]
            rank = rank + jnp.where(other == imp, jnp.where(jp < j_io, 1.0, 0.0),
                                    jnp.where(other > imp, 1.0, 0.0))
        chosen_t = jnp.where(rank < min(N_SEL, n_slc), jnp.where(imp > -jnp.inf, 1.0, 0.0), 0.0)
        chosen_t = jnp.concatenate([chosen_t, jnp.zeros((LANES - n_rows, nq), F32)], axis=0)
        chosen = chosen_t.T.astype(BF)

        def sel_mask(c):
            in_sel = _dot(chosen, exp_ref[c])
            if (c + 1) * SEL_CHUNK <= first_q:
                return in_sel > 0.5
            t_io = c * SEL_CHUNK + lax.broadcasted_iota(jnp.int32, (1, SEL_CHUNK), 1)
            return jnp.where(t_io <= qpos, in_sel, 0.0) > 0.5

        o_sel = attend(n_sel_chunks, SEL_CHUNK, s_sel, m_run, e_scr,
                       lambda c: sel_ref[c * SEL_CHUNK:(c + 1) * SEL_CHUNK, :HEAD_DIM], sel_mask,
                       sel_ref[:n_sel_chunks * SEL_CHUNK, HEAD_DIM:])

        gt = gt_ref[...]
        outs = []
        for r, rs in enumerate(rows):
            outs.append(gt[:, 3 * r:3 * r + 1] * o_cmp[rs] + gt[:, 3 * r + 1:3 * r + 2] * o_sel[rs]
                        + gt[:, 3 * r + 2:3 * r + 3] * o_win[rs])
        o_ref[...] = jnp.concatenate(outs, axis=1).astype(BF)

    qb_per_width = n_qb // n_widths
    for wi in range(n_widths):
        n_sel_chunks = -(-(wi + 1) * qb_per_width * nq // SEL_CHUNK)
        pl.when((qb >= wi * qb_per_width) & (qb < (wi + 1) * qb_per_width))(
            functools.partial(step, n_sel_chunks, wi * qb_per_width * nq))


def _attn_prompt(q, gates, cmp_flat, kv_sel, kv_win, *, scale):
    nb, t, qw = q.shape
    n_g = N_KV_HEADS
    gw = qw // n_g
    kvw = 2 * HEAD_DIM
    cmp_rows = cmp_flat.shape[1]
    n_slc = t // SEL_BLOCK
    n_slc_rows = -(-n_slc // 8) * 8
    rq = gw // HEAD_DIM
    n_qb = t // Q_BLOCK
    n_chunks = t // SEL_CHUNK
    wmap_t = jnp.asarray(np.tile(_overlap_map(cmp_rows // KV_SLABS, n_slc, n_slc_rows), (rq, 1)).T, BF)
    key_blk = (np.arange(t) // SEL_BLOCK).reshape(n_chunks, 1, SEL_CHUNK)
    expand = jnp.asarray((key_blk == np.arange(LANES)[None, :, None]).astype(np.float32), BF)
    rows = rq * Q_BLOCK
    n_widths = 4 if n_qb % 4 == 0 else 1
    return pl.pallas_call(
        functools.partial(_attn_prompt_kernel, scale=scale, n_slc=n_slc, n_qb=n_qb, n_widths=n_widths),
        out_shape=jax.ShapeDtypeStruct((nb, t, qw), BF),
        grid=(nb, n_g, n_qb),
        in_specs=[pl.BlockSpec((None, Q_BLOCK, gw), lambda b, g, i: (b, i, g)),
                  pl.BlockSpec((None, Q_BLOCK, LANES), lambda b, g, i: (b, i, g)),
                  pl.BlockSpec((None, cmp_rows, HEAD_DIM), lambda b, g, i: (b, 0, 0)),
                  pl.BlockSpec((None, t, kvw), lambda b, g, i: (b, 0, g)),
                  pl.BlockSpec((None, t, kvw), lambda b, g, i: (b, 0, g)),
                  pl.BlockSpec(wmap_t.shape, lambda b, g, i: (0, 0)),
                  pl.BlockSpec(expand.shape, lambda b, g, i: (0, 0, 0))],
        out_specs=pl.BlockSpec((None, Q_BLOCK, gw), lambda b, g, i: (b, i, g)),
        scratch_shapes=[pltpu.VMEM((n_chunks, rows, SEL_CHUNK), F32),
                        pltpu.VMEM((rows, LANES), F32),
                        pltpu.VMEM((rows, t), BF),
                        pltpu.VMEM((1, rows, WINDOW + Q_BLOCK), F32),
                        pltpu.VMEM((rows, LANES), F32),
                        pltpu.VMEM((rows, WINDOW + Q_BLOCK), BF)],
        compiler_params=_cparams(3, 48),
        name="attn_prompt",
    )(q, gates, cmp_flat, kv_sel, kv_win, wmap_t, expand)


def _attn_sample_a_kernel(q_ref, cmp_ref, win_ref, wmap_ref, ocmp_ref, owin_ref, idx_ref, val_ref, *,
                          scale, q_pos, n_cmp, n_slc, win_pos0):
    q = q_ref[...]
    n_h = q.shape[0]
    rq = n_h // N_KV_HEADS
    row_g = _div_pow2(lax.broadcasted_iota(jnp.int32, (n_h, 1), 0), rq)
    cmp_rows = cmp_ref.shape[0] // KV_SLABS
    win_rows = win_ref.shape[0] // KV_SLABS
    n_io = lax.broadcasted_iota(jnp.int32, (1, cmp_rows), 1)
    valid_c = (n_io < n_cmp) & ((n_io * CMP_STRIDE + (CMP_LEN - 1)) <= q_pos)
    kp = win_pos0 + lax.broadcasted_iota(jnp.int32, (1, win_rows), 1)
    valid_w = (kp <= q_pos) & (kp > q_pos - WINDOW)
    o_cmp = jnp.zeros((n_h, HEAD_DIM), F32)
    o_win = jnp.zeros((n_h, HEAD_DIM), F32)
    imp = jnp.zeros((n_h, wmap_ref.shape[1]), F32)
    for g in range(N_KV_HEADS):
        in_g = row_g == g
        kc = cmp_ref[pl.ds(2 * g, cmp_rows, stride=KV_SLABS), :].astype(BF)
        vc = cmp_ref[pl.ds(2 * g + 1, cmp_rows, stride=KV_SLABS), :].astype(BF)
        p = _masked_softmax(_dot_t(q, kc) * scale, valid_c).astype(BF)
        o_cmp = jnp.where(in_g, _dot(p, vc), o_cmp)
        imp_rows = _dot(p, wmap_ref[...])
        imp_g = jnp.sum(jnp.where(in_g, imp_rows, 0.0), axis=0, keepdims=True)
        imp = jnp.where(in_g, imp_g, imp)
        kw = win_ref[pl.ds(2 * g, win_rows, stride=KV_SLABS), :].astype(BF)
        vw = win_ref[pl.ds(2 * g + 1, win_rows, stride=KV_SLABS), :].astype(BF)
        pw = _masked_softmax(_dot_t(q, kw) * scale, valid_w).astype(BF)
        o_win = jnp.where(in_g, _dot(pw, vw), o_win)
    ocmp_ref[...] = o_cmp
    owin_ref[...] = o_win

    j_io = lax.broadcasted_iota(jnp.int32, (1, imp.shape[1]), 1)
    q_blk = q_pos // SEL_BLOCK
    causal = j_io <= q_blk
    forced = (j_io == 0) | (causal & (j_io > q_blk - N_LOCAL_SEL))
    imp = jnp.where(forced, imp + FORCE_BONUS, imp)
    imp = jnp.where(causal, imp, -jnp.inf)
    rank = _rank_desc(imp, n_slc)
    jf = j_io.astype(F32)
    k_io = lax.broadcasted_iota(jnp.int32, (1, idx_ref.shape[1]), 1)
    idx = jnp.zeros(idx_ref.shape, F32)
    val = jnp.zeros(idx_ref.shape, F32)
    for k in range(min(N_SEL, n_slc)):
        hit = rank == k
        ik = jnp.sum(jnp.where(hit, jf, 0.0), axis=1, keepdims=True)
        vk = jnp.sum(jnp.where(hit, jnp.where(imp > -jnp.inf, 1.0, 0.0), 0.0), axis=1, keepdims=True)
        idx = jnp.where(k_io == k, ik, idx)
        val = jnp.where(k_io == k, vk, val)
    idx_ref[...] = idx.astype(jnp.int32)
    val_ref[...] = val.astype(jnp.int32)


def _attn_sample_a(q, cmp_blocks, win_rows, *, scale, q_pos, n_cmp, n_slc, win_pos0):
    nb, n_h, d = q.shape
    n_cols = -(-n_slc // LANES) * LANES
    wmap = jnp.asarray(_overlap_map(cmp_blocks.shape[1] // KV_SLABS, n_slc, n_cols), BF)
    o_sds = jax.ShapeDtypeStruct((nb, n_h, d), F32)
    i_sds = jax.ShapeDtypeStruct((nb, n_h, LANES), jnp.int32)
    blk = lambda a: pl.BlockSpec((None,) + a.shape[1:], lambda b: (b, 0, 0))
    ospec = pl.BlockSpec((None, n_h, d), lambda b: (b, 0, 0))
    ispec = pl.BlockSpec((None, n_h, LANES), lambda b: (b, 0, 0))
    return pl.pallas_call(
        functools.partial(_attn_sample_a_kernel, scale=scale, q_pos=q_pos, n_cmp=n_cmp, n_slc=n_slc,
                          win_pos0=win_pos0),
        out_shape=(o_sds, o_sds, i_sds, i_sds),
        grid=(nb,),
        in_specs=[blk(q), blk(cmp_blocks), blk(win_rows), pl.BlockSpec(wmap.shape, lambda b: (0, 0))],
        out_specs=(ospec, ospec, ispec, ispec),
        compiler_params=_cparams(1, 32),
        name="attn_sample_a",
    )(q, cmp_blocks, win_rows, wmap)


def _attn_sample_b_kernel(idx_ref, val_ref, pt_ref, q_ref, new_ref, gt_ref, ocmp_ref, owin_ref, pool_hbm, o_ref,
                          blk_buf, sems, *, scale, q_pos, past_len, n_total):
    b = pl.program_id(0)
    g = pl.program_id(1)
    n = b * N_KV_HEADS + g
    ahead = PAGE_SLOTS - 1
    halves = PAGE_SIZE // SEL_BLOCK

    def block_copy(m, k):
        slot = m % PAGE_SLOTS
        row0 = jnp.minimum(idx_ref[m * N_SEL + k] * SEL_BLOCK, past_len - 1)
        page = lax.shift_right_logical(row0, _log2(PAGE_SIZE))
        half = lax.shift_right_logical(lax.bitwise_and(row0, PAGE_SIZE - 1), _log2(SEL_BLOCK))
        phys = pt_ref[m // N_KV_HEADS, page] * halves + half
        return pltpu.make_async_copy(pool_hbm.at[phys], blk_buf.at[slot, k], sems.at[slot])

    def start_step(m):
        for k in range(N_SEL):
            block_copy(m, k).start()

    @pl.when(n == 0)
    def _():
        for m in range(min(ahead, n_total)):
            start_step(m)

    @pl.when(n + ahead < n_total)
    def _():
        start_step(n + ahead)

    for k in range(N_SEL):
        block_copy(n, k).wait()
    slot = n % PAGE_SLOTS
    blk_refs = [blk_buf.at[slot, k] for k in range(N_SEL)]

    q = q_ref[...]
    n_h = q.shape[0]
    rq = n_h // N_KV_HEADS
    base = n * N_SEL
    n_keys = N_SEL * SEL_BLOCK
    lane = lax.broadcasted_iota(jnp.int32, (1, n_keys), 1)
    k_new = new_ref[pl.ds(2 * g, 1), :]
    v_new = new_ref[pl.ds(2 * g + 1, 1), :]
    ks, vs = [], []
    kpos = _mod_pow2(lane, SEL_BLOCK)
    kval = jnp.zeros((1, n_keys), jnp.int32)
    lane_blk = _div_pow2(lane, SEL_BLOCK)
    for k in range(N_SEL):
        blk = idx_ref[base + k]
        is_new = blk * SEL_BLOCK >= past_len
        ks.append(jnp.where(is_new, k_new, blk_refs[k][pl.ds(2 * g, SEL_BLOCK, stride=KV_SLABS), :]).astype(BF))
        vs.append(jnp.where(is_new, v_new, blk_refs[k][pl.ds(2 * g + 1, SEL_BLOCK, stride=KV_SLABS), :]).astype(BF))
        in_k = lane_blk == k
        kpos = kpos + jnp.where(in_k, blk * SEL_BLOCK, 0)
        kval = jnp.where(in_k, val_ref[base + k], kval)
    valid = (kval > 0) & (kpos <= q_pos)
    p = _masked_softmax(_dot_t(q, jnp.concatenate(ks, axis=0)) * scale, valid).astype(BF)
    o_sel = _dot(p, jnp.concatenate(vs, axis=0))
    gt = gt_ref[...]
    in_g = _div_pow2(lax.broadcasted_iota(jnp.int32, (n_h, 1), 0), rq) == g

    @pl.when(g == 0)
    def _():
        o_ref[...] = gt[:, 0:1] * ocmp_ref[...] + gt[:, 2:3] * owin_ref[...]

    o_ref[...] += jnp.where(in_g, gt[:, 1:2] * o_sel, 0.0)


def _attn_sample_b(sel_idx, sel_val, page_table, q, kv_sel_new, gates, o_cmp, o_win, pool_sel, *,
                   scale, q_pos, past_len):
    nb, n_h, d = q.shape
    blk_rows = SEL_BLOCK * KV_SLABS
    pool = pool_sel.reshape(pool_sel.shape[0] // blk_rows, blk_rows, d)
    per_b = lambda a: pl.BlockSpec((None,) + a.shape[1:], lambda b, g, idx, val, pt: (b, 0, 0))
    new_rows = kv_sel_new.reshape(nb, KV_SLABS, d)
    grid_spec = pltpu.PrefetchScalarGridSpec(
        num_scalar_prefetch=3,
        grid=(nb, N_KV_HEADS),
        in_specs=[per_b(q), per_b(new_rows), per_b(gates), per_b(o_cmp), per_b(o_win),
                  pl.BlockSpec(memory_space=pl.ANY)],
        out_specs=pl.BlockSpec((None, n_h, d), lambda b, g, idx, val, pt: (b, 0, 0)),
        scratch_shapes=[pltpu.VMEM((PAGE_SLOTS, N_SEL, blk_rows, d), F32),
                        pltpu.SemaphoreType.DMA((PAGE_SLOTS,))],
    )
    return pl.pallas_call(
        functools.partial(_attn_sample_b_kernel, scale=scale, q_pos=q_pos, past_len=past_len,
                          n_total=nb * N_KV_HEADS),
        out_shape=jax.ShapeDtypeStruct((nb, n_h, d), F32),
        grid_spec=grid_spec,
        compiler_params=_cparams(2, 32),
        name="attn_sample_b",
    )(sel_idx, sel_val, page_table, q, new_rows, gates, o_cmp, o_win, pool)


def kernel(x_prompt, x_sample, cache_cmp_kv, cache_sel_kv, cache_win_kv, page_table, c_prompt, c_sample, w_ada, b_ada, ln_g, ln_b, a_w_in, a_v_g, a_v_b, a_w_s, a_b_s, a_w_out, w_kv, cmp_pe, cmp_w1, cmp_b1, cmp_w2, cmp_b2, b_w_qg, b_w_o, ffn_w_in, ffn_w_out):
    nb, t, d = x_prompt.shape
    ns, tq, _ = x_sample.shape
    assert tq == 1, "the sample path handles one new token per sequence"
    depth = w_ada.shape[0]
    n_a = a_w_in.shape[0]
    alpha = float((2 * depth) ** 0.25)
    scale = float(HEAD_DIM ** -0.5)
    n_pool = cache_cmp_kv.shape[0]
    past_len = page_table.shape[1] * PAGE_SIZE
    kvw = N_KV_HEADS * 2 * HEAD_DIM
    q_width = b_w_o.shape[1]
    n_heads = q_width // HEAD_DIM
    rq = n_heads // N_KV_HEADS

    mod = _ada_modulation(jnp.concatenate([c_prompt, c_sample], axis=0), w_ada, b_ada)

    def mods(layer, sub):
        m = mod[layer * 2 + sub]
        out = []
        for part in range(3):
            sl = m[:, part * d:(part + 1) * d]
            out.append((sl[:nb].reshape(nb, 1, d), sl[nb:].reshape(1, ns, d)))
        return out

    wb = lambda w: w.astype(BF)
    hidden = cmp_w1.shape[-1]
    halves = CMP_LEN // CMP_STRIDE
    assert halves == 2, "a compression block spans two stride-chunks"
    w1all = wb(cmp_w1.reshape(2, halves, CMP_STRIDE, HEAD_DIM, hidden)
               .transpose(2, 3, 0, 1, 4).reshape(CMP_STRIDE * HEAD_DIM, 2 * halves * hidden))
    pe16 = jnp.pad(cmp_pe.reshape(2 * halves, CMP_STRIDE * HEAD_DIM), ((0, 16 - 2 * halves), (0, 0)))
    w2all = wb(cmp_w2.transpose(1, 0, 2).reshape(hidden, 2 * HEAD_DIM))

    a_w_out_b, b_w_o_b, ffn_w_in_b, ffn_w_out_b = wb(a_w_out), wb(b_w_o), wb(ffn_w_in), wb(ffn_w_out)

    tm_p = 512
    tpg_p = t // tm_p
    prompt = dict(tm=tm_p, tiles_per_group=tpg_p)
    sample = dict(tm=ns, tiles_per_group=1)

    xp = x_prompt.reshape(nb * t, d)
    xs = x_sample.reshape(ns, d)
    hp = hs = None
    chunk_v = []
    outs = {}
    for layer in range(depth):
        (sh_p, sh_s), (sc_p, sc_s), (gt_p, gt_s) = mods(layer, 0)
        (fsh_p, fsh_s), (fsc_p, fsc_s), (fgt_p, fgt_s) = mods(layer, 1)
        if layer < n_a:
            w_in = wb(a_w_in[layer])
            ap = _gmlp_in(xp, sc_p, sh_p, w_in, a_v_g[layer], a_v_b[layer], a_w_s[layer], a_b_s[layer],
                          chunked=True, **prompt)
            as_, v_rows = _gmlp_in(xs, sc_s, sh_s, w_in, a_v_g[layer], a_v_b[layer], a_w_s[layer], a_b_s[layer],
                                   chunked=False, **sample)
            chunk_v.append(v_rows.reshape(ns, tq, -1))
            xp, hp = _proj_res_ln(ap, a_w_out_b, xp, gt_p, ln_g[layer, 0], ln_b[layer, 0], (fsc_p, fsh_p),
                                  alpha=alpha, layer=layer, **prompt)
            xs, hs = _proj_res_ln(as_, a_w_out_b, xs, gt_s, ln_g[layer, 0], ln_b[layer, 0], (fsc_s, fsh_s),
                                  alpha=alpha, layer=layer, **sample)
        else:
            if layer == n_a:
                w_kvb = wb(w_kv)
                cmp_p, sel_p, win_p, sel_pb, win_pb = _kv_proj(xp, w_kvb, tm=tm_p)
                cmp_s, sel_s, win_s, _, _ = _kv_proj(xs, w_kvb, tm=ns)
                pages_per_seq_p = t // PAGE_SIZE
                ident = jnp.arange(nb * pages_per_seq_p, dtype=jnp.int32).reshape(nb, pages_per_seq_p)
                cmp_blocks_p = _compress(cmp_p, ident, w1all, pe16, cmp_b1, w2all, cmp_b2)
                assert (past_len + tq) // CMP_STRIDE == past_len // CMP_STRIDE
                cmp_blocks_s = _compress(cache_cmp_kv.reshape(n_pool * PAGE_FLAT_ROWS, HEAD_DIM), page_table,
                                         w1all, pe16, cmp_b1, w2all, cmp_b2)
                n_win = cache_win_kv.shape[1]
                win_keys_s = jnp.concatenate([cache_win_kv.reshape(ns, n_win * KV_SLABS, HEAD_DIM),
                                              win_s.reshape(ns, tq * KV_SLABS, HEAD_DIM)],
                                             axis=1)[:, -n_win * KV_SLABS:]
                outs["kv_p"] = (cmp_p, sel_p, win_p)
                outs["kv_pb"] = (sel_pb.reshape(nb, t, kvw), win_pb.reshape(nb, t, kvw))
                outs["kv_s"] = (cmp_s, sel_s, win_keys_s)
            j = layer - n_a
            w_qg = b_w_qg[j]
            w_q = wb(w_qg[:, :q_width])
            n_gates = w_qg.shape[1] - q_width
            w_gate = wb(jnp.pad(w_qg[:, q_width:], ((0, 0), (0, LANES - n_gates))))
            assert layer > 0, "an attention layer must follow another layer"
            q_p, g_p = _qg_proj(hp, w_q, w_gate, tm=tm_p, gates_per_group=rq * 3)
            q_s, g_s = _qg_proj(hs, w_q, w_gate, tm=ns, gates_per_group=rq * 3)
            o_p = _attn_prompt(q_p.reshape(nb, t, q_width), g_p.reshape(nb, t, N_KV_HEADS * LANES),
                               cmp_blocks_p, outs["kv_pb"][0], outs["kv_pb"][1], scale=scale)
            n_cmp_s = (past_len + tq) // CMP_STRIDE - CMP_LEN // CMP_STRIDE + 1
            n_slc_s = -(-(past_len + tq) // SEL_BLOCK)
            q_s3 = q_s.reshape(ns, n_heads, HEAD_DIM)
            o_cmp_s, o_win_s, idx_s, val_s = _attn_sample_a(
                q_s3, cmp_blocks_s, outs["kv_s"][2], scale=scale, q_pos=past_len, n_cmp=n_cmp_s, n_slc=n_slc_s,
                win_pos0=past_len + tq - outs["kv_s"][2].shape[1] // KV_SLABS)
            sel_idx = idx_s[:, ::rq, :N_SEL].reshape(-1)
            sel_val = val_s[:, ::rq, :N_SEL].reshape(-1)
            gates_s = g_s.reshape(ns, N_KV_HEADS, LANES)[:, :, :rq * 3].reshape(ns, n_heads, 3)
            o_s = _attn_sample_b(sel_idx, sel_val, page_table, q_s3, outs["kv_s"][1], gates_s, o_cmp_s, o_win_s,
                                 cache_sel_kv.reshape(n_pool * PAGE_FLAT_ROWS, HEAD_DIM), scale=scale,
                                 q_pos=past_len, past_len=past_len)
            xp, hp = _proj_res_ln(o_p.reshape(nb * t, q_width), b_w_o_b, xp, gt_p, ln_g[layer, 0], ln_b[layer, 0],
                                  (fsc_p, fsh_p), alpha=alpha, layer=j, **prompt)
            xs, hs = _proj_res_ln(o_s.reshape(ns, q_width), b_w_o_b, xs, gt_s, ln_g[layer, 0], ln_b[layer, 0],
                                  (fsc_s, fsh_s), alpha=alpha, layer=j, **sample)
        fp = _ffn_in(hp, ffn_w_in_b, layer, tm=2 * tm_p)
        fs = _ffn_in(hs, ffn_w_in_b, layer, tm=ns)
        nxt_p = nxt_s = None
        if n_a <= layer + 1 < depth:
            (nsh_p, nsh_s), (nsc_p, nsc_s), _ = mods(layer + 1, 0)
            nxt_p, nxt_s = (nsc_p, nsh_p), (nsc_s, nsh_s)
        tm_f = tm_p // 2
        xp, hp = _proj_res_ln(fp, ffn_w_out_b, xp, fgt_p, ln_g[layer, 1], ln_b[layer, 1], nxt_p, alpha=alpha,
                              layer=layer, tm=tm_f, tiles_per_group=t // tm_f)
        xs, hs = _proj_res_ln(fs, ffn_w_out_b, xs, fgt_s, ln_g[layer, 1], ln_b[layer, 1], nxt_s, alpha=alpha,
                              layer=layer, **sample)

    cmp_p, sel_p, win_p = outs["kv_p"]
    cmp_s, sel_s, win_keys_s = outs["kv_s"]
    kv_shape = (N_KV_HEADS, 2, HEAD_DIM)
    n_win_p = min(WINDOW, t)
    return (xp.reshape(nb, t, d), xs.reshape(ns, tq, d),
            cmp_p.reshape((nb, t) + kv_shape), sel_p.reshape((nb, t) + kv_shape),
            win_p.reshape((nb, t) + kv_shape)[:, -n_win_p:],
            cmp_s.reshape((ns, tq) + kv_shape), sel_s.reshape((ns, tq) + kv_shape),
            win_keys_s.reshape((ns, -1) + kv_shape),
            jnp.stack(chunk_v))
```
